```python
import math
import jax, jax.numpy as jnp
from jax import lax
import numpy as np

D_MODEL = 2048
BATCH = 2
SEQ = 8192
DEPTH = 2
DEC_BATCH = 32
DEC_SEQ = 64
PAST_LEN = 1024

CHUNK = 64
D_PLE = 256
EPS = 1e-6

POOL_WIDTH = 1024
POOL_WINDOWS = (2, 4, 8, 16)
POOL_GROUPS = 4
POOL_GROUP_DIM = POOL_WIDTH // POOL_GROUPS
POOL_BUF = 15

ML_HEADS = 4
ML_HEAD_DIM = 256
ML_WIDTH = ML_HEADS * ML_HEAD_DIM

ATT_HEADS = 8
ATT_HEAD_DIM = 128
ATT_WIDTH = ATT_HEADS * ATT_HEAD_DIM
ATT_LEFT_CHUNKS = 8
ATT_WINDOW = ATT_LEFT_CHUNKS * CHUNK
REL_CLIP = 256
KV_LEN = min(ATT_WINDOW, PAST_LEN)

N_BRANCH = 3
BRANCH_WIDTH = 1024

IN_SIZES = (POOL_WIDTH, POOL_WIDTH,
            ML_WIDTH, ML_WIDTH, ML_WIDTH, ML_WIDTH, ML_WIDTH,
            ML_HEADS, ML_HEADS,
            ATT_WIDTH, ATT_WIDTH, ATT_WIDTH, ATT_WIDTH,
            N_BRANCH * D_MODEL)
IN_WIDTH = 17416

kernel_name = 'hybrid_pool_mlstm_chunkattn_step'


def _rmsnorm(x, g):
    xf = x.astype(jnp.float32)
    y = xf * lax.rsqrt(jnp.mean(xf * xf, axis=-1, keepdims=True) + EPS)
    return (y * g.astype(jnp.float32)).astype(x.dtype)


def _split_points():
    pts, acc = [], 0
    for s in IN_SIZES[:-1]:
        acc += s
        pts.append(acc)
    return pts


def _pool_mix(u, hist, pos0, w_group, scale):
    B, T, P = u.shape
    ext = jnp.concatenate([hist.astype(u.dtype), u], axis=1).astype(jnp.float32)
    cs = jnp.cumsum(ext, axis=1)
    cs = jnp.concatenate([jnp.zeros_like(cs[:, :1]), cs], axis=1)
    end = cs[:, POOL_BUF + 1:]
    pos = pos0 + jnp.arange(T)
    means = []
    for gi, w in enumerate(POOL_WINDOWS):
        sl = slice(gi * POOL_GROUP_DIM, (gi + 1) * POOL_GROUP_DIM)
        s = end[..., sl] - cs[:, POOL_BUF + 1 - w:POOL_BUF + 1 - w + T, sl]
        cnt = jnp.minimum(pos + 1, w).astype(jnp.float32)
        means.append(s / cnt[None, :, None])
    m = jnp.concatenate(means, axis=-1).astype(u.dtype) - u
    y = jnp.einsum('btgc,gcd->btgd', m.reshape(B, T, POOL_GROUPS, POOL_GROUP_DIM), w_group)
    return y.reshape(B, T, P) * scale


def _mlstm_chunk(carry, inp):
    c, n, m = carry
    q, k, v, ig, lf = inp
    L = q.shape[2]
    a = jnp.cumsum(lf, axis=-1)
    b = a[..., -1]
    causal = jnp.tril(jnp.ones((L, L), dtype=bool))
    logd = jnp.where(causal, a[..., :, None] - a[..., None, :] + ig[..., None, :], -jnp.inf)
    inter = a + m[..., None]
    m_row = jnp.maximum(inter, jnp.max(logd, axis=-1))
    dmat = jnp.exp(logd - m_row[..., None])
    w_inter = jnp.exp(inter - m_row)
    s = jnp.einsum('bhtd,bhsd->bhts', q, k) * dmat
    num = jnp.einsum('bhts,bhsd->bhtd', s, v) + w_inter[..., None] * jnp.einsum('bhvk,bhtk->bhtv', c, q)
    den = jnp.sum(s, axis=-1) + w_inter * jnp.einsum('bhk,bhtk->bht', n, q)
    h = num / jnp.maximum(jnp.abs(den), jnp.exp(-m_row))[..., None]
    g = b[..., None] - a + ig
    m_new = jnp.maximum(b + m, jnp.max(g, axis=-1))
    wk = jnp.exp(g - m_new[..., None])
    decay = jnp.exp(b + m - m_new)
    c_new = decay[..., None, None] * c + jnp.einsum('bhsv,bhsk->bhvk', v * wk[..., None], k)
    n_new = decay[..., None] * n + jnp.einsum('bhs,bhsk->bhk', wk, k)
    return (c_new, n_new, m_new), h


def _mlstm(q, k, v, ig, fg, c0, n0, m0):
    B, T, _ = q.shape
    L = min(CHUNK, T)
    NC = T // L
    f32 = jnp.float32

    def heads(t):
        return t.astype(f32).reshape(B, NC, L, ML_HEADS, ML_HEAD_DIM).transpose(1, 0, 3, 2, 4)

    def gates(t):
        return t.astype(f32).reshape(B, NC, L, ML_HEADS).transpose(1, 0, 3, 2)

    xs = (heads(q), heads(k) * (ML_HEAD_DIM ** -0.5), heads(v), gates(ig), jax.nn.log_sigmoid(gates(fg)))
    (c1, n1, m1), hs = lax.scan(_mlstm_chunk, (c0.astype(f32), n0.astype(f32), m0.astype(f32)), xs)
    h = hs.transpose(1, 0, 3, 2, 4).reshape(B, T, ML_HEADS, ML_HEAD_DIM)
    return h.astype(q.dtype), (c1, n1, m1)


def _band_attend(q, k, v, q_pos, k_pos, k_valid, rel_bias):
    s = jnp.einsum('bqhd,bkhd->bhqk', q, k).astype(jnp.float32) * (ATT_HEAD_DIM ** -0.5)
    dist = jnp.clip(q_pos[:, None] - k_pos[None, :], -REL_CLIP, REL_CLIP) + REL_CLIP
    s = s + rel_bias[:, dist].astype(jnp.float32)[None]
    s = jnp.where(k_valid[None, None, None, :], s, -jnp.inf)
    p = jax.nn.softmax(s, axis=-1).astype(v.dtype)
    return jnp.einsum('bhqk,bkhd->bqhd', p, v)


def _chunk_band_prompt(q, k, v, rel_bias):
    B, T, H, D = q.shape
    band = ATT_WINDOW + CHUNK
    pad = ((0, 0), (ATT_WINDOW, 0), (0, 0), (0, 0))
    kpad, vpad = jnp.pad(k, pad), jnp.pad(v, pad)

    def one_chunk(ci):
        start = ci * CHUNK
        qc = lax.dynamic_slice_in_dim(q, start, CHUNK, axis=1)
        kc = lax.dynamic_slice_in_dim(kpad, start, band, axis=1)
        vc = lax.dynamic_slice_in_dim(vpad, start, band, axis=1)
        q_pos = start + jnp.arange(CHUNK)
        k_pos = start - ATT_WINDOW + jnp.arange(band)
        return _band_attend(qc, kc, vc, q_pos, k_pos, k_pos >= 0, rel_bias)

    o = lax.map(one_chunk, jnp.arange(T // CHUNK))
    return jnp.moveaxis(o, 0, 1).reshape(B, T, H, D)


def _layer(x, p, pool_hist, c0, n0, m0, kv_cache, pos0,
           norm_g, w_in, w_pool_group, pool_scale, b_ig, b_fg, ml_head_norm,
           att_q_norm, att_k_norm, att_rel_bias, w_branch, w_out, ple_norm, w_ple_gate, w_ple_proj):
    B, T, _ = x.shape
    h = _rmsnorm(x, norm_g)
    proj = jnp.einsum('btd,dn->btn', h, w_in)
    (pu, pz, mq, mk, mv, mo, mz, mi, mf, aq, ak, av, az, gts) = jnp.split(proj, _split_points(), axis=-1)

    y_pool = _pool_mix(pu, pool_hist, pos0, w_pool_group, pool_scale) * jax.nn.silu(pz)
    new_pool = jnp.concatenate([pool_hist.astype(pu.dtype), pu], axis=1)[:, -POOL_BUF:]

    hm, (c1, n1, m1) = _mlstm(mq, mk, mv, mi + b_ig, mf + b_fg, c0, n0, m0)
    hm = hm * jax.nn.sigmoid(mo).reshape(B, T, ML_HEADS, ML_HEAD_DIM)
    hm = _rmsnorm(hm, ml_head_norm.reshape(ML_HEADS, ML_HEAD_DIM)).reshape(B, T, ML_WIDTH)
    y_ml = hm * jax.nn.silu(mz)

    qh = _rmsnorm(aq.reshape(B, T, ATT_HEADS, ATT_HEAD_DIM), att_q_norm)
    kh = _rmsnorm(ak.reshape(B, T, ATT_HEADS, ATT_HEAD_DIM), att_k_norm)
    vh = av.reshape(B, T, ATT_HEADS, ATT_HEAD_DIM)
    if kv_cache is None:
        ya = _chunk_band_prompt(qh, kh, vh, att_rel_bias)
        keep = min(ATT_WINDOW, T)
        new_k, new_v = kh[:, T - keep:], vh[:, T - keep:]
    else:
        ck, cv = kv_cache
        P = ck.shape[1]
        kk = jnp.concatenate([ck.astype(kh.dtype), kh], axis=1)
        vv = jnp.concatenate([cv.astype(vh.dtype), vh], axis=1)
        k_pos = jnp.concatenate([pos0 - P + jnp.arange(P), pos0 + jnp.arange(T)])
        q_pos = pos0 + jnp.arange(T)
        ya = _band_attend(qh, kk, vv, q_pos, k_pos, jnp.ones((P + T,), dtype=bool), att_rel_bias)
        new_k, new_v = kh, vh
    y_att = ya.reshape(B, T, ATT_WIDTH) * jax.nn.silu(az)

    g = jax.nn.sigmoid(gts.reshape(B, T, N_BRANCH, D_MODEL))
    merged = g[:, :, 0] * jnp.einsum('btc,cd->btd', y_pool, w_branch[0])
    merged = merged + g[:, :, 1] * jnp.einsum('btc,cd->btd', y_ml, w_branch[1])
    merged = merged + g[:, :, 2] * jnp.einsum('btc,cd->btd', y_att, w_branch[2])
    x = x + jnp.einsum('btd,de->bte', merged, w_out)

    pg = jax.nn.sigmoid(jnp.einsum('btd,de->bte', _rmsnorm(x, ple_norm), w_ple_gate))
    x = x + pg * jnp.einsum('btp,pd->btd', p, w_ple_proj)
    return x, (new_pool, c1, n1, m1, new_k, new_v)


def setup_inputs(seed: int = 0) -> dict:
    key = jax.random.key(seed)
    ks = jax.random.split(key, 32)
    f32 = jnp.float32

    def nrm(k, shape, s):
        return jax.random.normal(k, shape, f32) * s

    def gain(k, shape):
        return 1.0 + 0.02 * jax.random.normal(k, shape, f32)

    return {
        'x_prompt': nrm(ks[0], (BATCH, SEQ, D_MODEL), 1.0),
        'x_sample': nrm(ks[1], (DEC_BATCH, DEC_SEQ, D_MODEL), 1.0),
        'cache_att_k': nrm(ks[2], (DEPTH, DEC_BATCH, KV_LEN, ATT_HEADS, ATT_HEAD_DIM), 1.0),
        'cache_att_v': nrm(ks[3], (DEPTH, DEC_BATCH, KV_LEN, ATT_HEADS, ATT_HEAD_DIM), 1.0),
        'state_pool': nrm(ks[4], (DEPTH, DEC_BATCH, POOL_BUF, POOL_WIDTH), 1.0),
        'state_mlstm_c': nrm(ks[5], (DEPTH, DEC_BATCH, ML_HEADS, ML_HEAD_DIM, ML_HEAD_DIM), 0.1),
        'state_mlstm_n': nrm(ks[6], (DEPTH, DEC_BATCH, ML_HEADS, ML_HEAD_DIM), 0.1),
        'state_mlstm_m': nrm(ks[7], (DEPTH, DEC_BATCH, ML_HEADS), 0.5),
        'p_prompt': nrm(ks[8], (DEPTH, BATCH, SEQ, D_PLE), 1.0),
        'p_sample': nrm(ks[9], (DEPTH, DEC_BATCH, DEC_SEQ, D_PLE), 1.0),
        'norm_mix': gain(ks[10], (DEPTH, D_MODEL)),
        'w_in': nrm(ks[11], (DEPTH, D_MODEL, IN_WIDTH), D_MODEL ** -0.5),
        'w_pool_group': nrm(ks[12], (DEPTH, POOL_GROUPS, POOL_GROUP_DIM, POOL_GROUP_DIM), POOL_GROUP_DIM ** -0.5),
        'pool_scale': gain(ks[13], (DEPTH, POOL_WIDTH)),
        'b_ig': nrm(ks[14], (DEPTH, ML_HEADS), 0.1),
        'b_fg': jnp.linspace(3.0, 6.0, ML_HEADS, dtype=f32)[None] + nrm(ks[15], (DEPTH, ML_HEADS), 0.1),
        'ml_head_norm': gain(ks[16], (DEPTH, ML_WIDTH)),
        'att_q_norm': gain(ks[17], (DEPTH, ATT_HEAD_DIM)),
        'att_k_norm': gain(ks[18], (DEPTH, ATT_HEAD_DIM)),
        'att_rel_bias': nrm(ks[19], (DEPTH, ATT_HEADS, 2 * REL_CLIP + 1), 0.5),
        'w_branch': nrm(ks[20], (DEPTH, N_BRANCH, BRANCH_WIDTH, D_MODEL), BRANCH_WIDTH ** -0.5),
        'w_out': nrm(ks[21], (DEPTH, D_MODEL, D_MODEL), D_MODEL ** -0.5),
        'ple_norm': gain(ks[22], (DEPTH, D_MODEL)),
        'w_ple_gate': nrm(ks[23], (DEPTH, D_MODEL, D_MODEL), D_MODEL ** -0.5),
        'w_ple_proj': nrm(ks[24], (DEPTH, D_PLE, D_MODEL), D_PLE ** -0.5),
    }


def reference(x_prompt, x_sample, cache_att_k, cache_att_v, state_pool, state_mlstm_c,
              state_mlstm_n, state_mlstm_m, p_prompt, p_sample, norm_mix, w_in, w_pool_group,
              pool_scale, b_ig, b_fg, ml_head_norm, att_q_norm, att_k_norm, att_rel_bias,
              w_branch, w_out, ple_norm, w_ple_gate, w_ple_proj):
    xp, xs = x_prompt, x_sample
    Bp = x_prompt.shape[0]
    sp = [[] for _ in range(6)]
    ss = [[] for _ in range(6)]
    for i in range(DEPTH):
        lw = (norm_mix[i], w_in[i], w_pool_group[i], pool_scale[i], b_ig[i], b_fg[i], ml_head_norm[i],
              att_q_norm[i], att_k_norm[i], att_rel_bias[i], w_branch[i], w_out[i], ple_norm[i],
              w_ple_gate[i], w_ple_proj[i])
        hist0 = jnp.zeros((Bp, POOL_BUF, POOL_WIDTH), xp.dtype)
        c0 = jnp.zeros((Bp, ML_HEADS, ML_HEAD_DIM, ML_HEAD_DIM), jnp.float32)
        n0 = jnp.zeros((Bp, ML_HEADS, ML_HEAD_DIM), jnp.float32)
        m0 = jnp.zeros((Bp, ML_HEADS), jnp.float32)
        xp, st_p = _layer(xp, p_prompt[i], hist0, c0, n0, m0, None, 0, *lw)
        xs, st_s = _layer(xs, p_sample[i], state_pool[i], state_mlstm_c[i], state_mlstm_n[i],
                          state_mlstm_m[i], (cache_att_k[i], cache_att_v[i]), PAST_LEN, *lw)
        for j in range(6):
            sp[j].append(st_p[j])
            ss[j].append(st_s[j])
    pool_p, c_p, n_p, m_p, k_p, v_p = [jnp.stack(a) for a in sp]
    pool_s, c_s, n_s, m_s, k_s, v_s = [jnp.stack(a) for a in ss]
    return (xp, xs, pool_p, pool_s, c_p, c_s, n_p, n_s, m_p, m_s, k_p, k_s, v_p, v_s)
```

```python
import functools
import math

import numpy as np
import jax
import jax.numpy as jnp
from jax import lax
from jax.experimental import pallas as pl
from jax.experimental.pallas import tpu as pltpu

F32 = jnp.float32
BF16 = jnp.bfloat16

EPS = 1e-6
CHUNK = 64
PAST_LEN = 1024

POOL_WINDOWS = (2, 4, 8, 16)
POOL_GROUP_DIM = 256
POOL_BUF = 15
POOL_HALO = 16

ML_HEADS = 4
ML_HEAD_DIM = 256

ATT_HEADS = 8
ATT_HEAD_DIM = 128
ATT_WINDOW = 512
REL_CLIP = 256
ATT_KEYS = 640

BRANCH_WIDTH = 1024
LANES = 128

N_MERGE_GATES = 3
COL_PU, COL_PZ, COL_MQ, COL_MK, COL_MV, COL_MO, COL_MZ, COL_AQ, COL_AK, COL_AV, COL_AZ = range(6, 17)
W_IN_GATES_START = 7168
W_IN_GATES_END = 7176
W_IN_MERGE_START = 11272

PROJ_DTYPE = jnp.float32
NEG_BIG = -1e30
MIB = 1024 * 1024


def _params(semantics, vmem_mib):
    return pltpu.CompilerParams(dimension_semantics=semantics, vmem_limit_bytes=vmem_mib * MIB)


def _sigmoid(x):
    return 1.0 / (1.0 + jnp.exp(-x))


def _silu(x):
    return x * _sigmoid(x)


def _nt_dot(a, b):
    return lax.dot_general(a, b, (((1,), (1,)), ((), ())), preferred_element_type=F32)


def _tn_dot(a, b):
    return lax.dot_general(a, b, (((0,), (0,)), ((), ())), preferred_element_type=F32)


def _inproj_kernel(x_ref, g_ref, w_ref, wg_ref, o_ref, og_ref, h_ref):
    @pl.when(pl.program_id(1) == 0)
    def _():
        x = x_ref[...]
        ms = jnp.mean(x * x, axis=-1, keepdims=True)
        h = (x * lax.rsqrt(ms + EPS) * g_ref[...]).astype(BF16)
        h_ref[...] = h
        og_ref[...] = jnp.dot(h, wg_ref[...], preferred_element_type=F32)

    o_ref[...] = jnp.dot(h_ref[...], w_ref[...], preferred_element_type=F32).astype(o_ref.dtype)


def _inproj(x2, gain, w_main, w_gate, tm):
    n, d = x2.shape
    nw = w_main.shape[1]
    tn = BRANCH_WIDTH
    return pl.pallas_call(
        _inproj_kernel,
        grid=(n // tm, nw // tn),
        in_specs=[pl.BlockSpec((tm, d), lambda i, j: (i, 0)),
                  pl.BlockSpec((1, d), lambda i, j: (0, 0)),
                  pl.BlockSpec((d, tn), lambda i, j: (0, j)),
                  pl.BlockSpec((d, LANES), lambda i, j: (0, 0))],
        out_specs=[pl.BlockSpec((tm, tn), lambda i, j: (i, j)),
                   pl.BlockSpec((tm, LANES), lambda i, j: (i, 0))],
        out_shape=[jax.ShapeDtypeStruct((n, nw), PROJ_DTYPE),
                   jax.ShapeDtypeStruct((n, LANES), F32)],
        scratch_shapes=[pltpu.VMEM((tm, d), BF16)],
        compiler_params=_params(("parallel", "arbitrary"), 48),
        name="inproj",
    )(x2, gain, w_main, w_gate)


def _qknorm_kernel(q_ref, k_ref, gq_ref, gk_ref, qo_ref, ko_ref):
    for h in range(ATT_HEADS):
        sl = slice(h * ATT_HEAD_DIM, (h + 1) * ATT_HEAD_DIM)
        q = q_ref[:, sl].astype(F32)
        qn = q * lax.rsqrt(jnp.mean(q * q, axis=-1, keepdims=True) + EPS) * gq_ref[...]
        qo_ref[:, sl] = qn.astype(qo_ref.dtype)
        k = k_ref[:, sl].astype(F32)
        kn = k * lax.rsqrt(jnp.mean(k * k, axis=-1, keepdims=True) + EPS) * gk_ref[...]
        ko_ref[:, sl] = kn.astype(ko_ref.dtype)


def _qknorm(proj, gq, gk, tm):
    n = proj.shape[0]
    w = BRANCH_WIDTH
    return pl.pallas_call(
        _qknorm_kernel,
        grid=(n // tm,),
        in_specs=[pl.BlockSpec((tm, w), lambda i: (i, COL_AQ)),
                  pl.BlockSpec((tm, w), lambda i: (i, COL_AK)),
                  pl.BlockSpec((1, ATT_HEAD_DIM), lambda i: (0, 0)),
                  pl.BlockSpec((1, ATT_HEAD_DIM), lambda i: (0, 0))],
        out_specs=[pl.BlockSpec((tm, w), lambda i: (i, 0)),
                   pl.BlockSpec((tm, w), lambda i: (i, 0))],
        out_shape=[jax.ShapeDtypeStruct((n, w), BF16),
                   jax.ShapeDtypeStruct((n, w), F32)],
        compiler_params=_params(("parallel",), 32),
        name="qknorm",
    )(proj, proj, gq, gk)


def _attn_bias(rel_bias, group):
    i = np.arange(group)[:, None]
    j = np.arange(ATT_KEYS)[None, :]
    dist = np.clip(i - j + ATT_WINDOW, -REL_CLIP, REL_CLIP) + REL_CLIP
    lo = (i // CHUNK) * CHUNK
    band = (j >= lo) & (j < lo + ATT_WINDOW + CHUNK)
    bias = rel_bias[:, dist].astype(F32)
    return jnp.where(jnp.asarray(band)[None], bias, NEG_BIG)


def _attn_kernel(q_ref, kp_ref, kc_ref, vp_ref, vc_ref, az_ref, bias_ref, o_ref, kw_ref, vw_ref,
                 *, tq, group, first_prev_invalid):
    rows = kw_ref.shape[0]
    kw_ref[0:ATT_WINDOW, :] = kp_ref[0].astype(BF16)
    kw_ref[ATT_WINDOW:ATT_WINDOW + tq, :] = kc_ref[0].astype(BF16)
    vw_ref[0:ATT_WINDOW, :] = vp_ref[0].astype(BF16)
    vw_ref[ATT_WINDOW:ATT_WINDOW + tq, :] = vc_ref[0].astype(BF16)
    if rows > ATT_WINDOW + tq:
        pad = jnp.zeros((rows - ATT_WINDOW - tq, kw_ref.shape[1]), BF16)
        kw_ref[ATT_WINDOW + tq:rows, :] = pad
        vw_ref[ATT_WINDOW + tq:rows, :] = pad

    scale = ATT_HEAD_DIM ** -0.5
    first = pl.program_id(1) == 0
    col = lax.broadcasted_iota(jnp.int32, (group, ATT_KEYS), 1)
    for g in range(tq // group):
        r0 = g * group
        if first_prev_invalid:
            dead = jnp.logical_and(first, col + r0 < ATT_WINDOW)
        for h in range(ATT_HEADS):
            sl = slice(h * ATT_HEAD_DIM, (h + 1) * ATT_HEAD_DIM)
            q = q_ref[0, r0:r0 + group, sl]
            k = kw_ref[r0:r0 + ATT_KEYS, sl]
            v = vw_ref[r0:r0 + ATT_KEYS, sl]
            s = _nt_dot(q, k) * scale + bias_ref[h]
            if first_prev_invalid:
                s = jnp.where(dead, NEG_BIG, s)
            m = jnp.max(s, axis=-1, keepdims=True)
            p = jnp.exp(s - m)
            p = p / jnp.sum(p, axis=-1, keepdims=True)
            o = jnp.dot(p.astype(BF16), v, preferred_element_type=F32)
            z = az_ref[0, r0:r0 + group, sl].astype(F32)
            o_ref[0, r0:r0 + group, sl] = (o * _silu(z)).astype(o_ref.dtype)


def _attention(qn3, kprev, kn3, vprev, vprev_col, proj3, bias, *, tq, group, prompt):
    b, t, w = qn3.shape
    nt = t // tq
    if prompt:
        assert tq == ATT_WINDOW
        prev_map = lambda bi, i: (bi, jnp.maximum(i - 1, 0), 0)
        vprev_map = lambda bi, i: (bi, jnp.maximum(i - 1, 0), vprev_col)
    else:
        assert nt == 1
        prev_map = lambda bi, i: (bi, 0, 0)
        vprev_map = lambda bi, i: (bi, 0, vprev_col)
    rows = tq + ATT_KEYS - group
    kernel = functools.partial(_attn_kernel, tq=tq, group=group, first_prev_invalid=prompt)
    return pl.pallas_call(
        kernel,
        grid=(b, nt),
        in_specs=[pl.BlockSpec((1, tq, w), lambda bi, i: (bi, i, 0)),
                  pl.BlockSpec((1, ATT_WINDOW, w), prev_map),
                  pl.BlockSpec((1, tq, w), lambda bi, i: (bi, i, 0)),
                  pl.BlockSpec((1, ATT_WINDOW, w), vprev_map),
                  pl.BlockSpec((1, tq, w), lambda bi, i: (bi, i, COL_AV)),
                  pl.BlockSpec((1, tq, w), lambda bi, i: (bi, i, COL_AZ)),
                  pl.BlockSpec((ATT_HEADS, group, ATT_KEYS), lambda bi, i: (0, 0, 0))],
        out_specs=pl.BlockSpec((1, tq, w), lambda bi, i: (bi, i, 0)),
        out_shape=jax.ShapeDtypeStruct((b, t, w), BF16),
        scratch_shapes=[pltpu.VMEM((rows, w), BF16), pltpu.VMEM((rows, w), BF16)],
        compiler_params=_params(("parallel", "arbitrary"), 48),
        name="attention",
    )(qn3, kprev, kn3, vprev, proj3, proj3, bias)


def _mlstm_kernel(q_ref, k_ref, v_ref, o_ref, z_ref, g_ref, gb_ref, hn_ref, c0_ref, n0_ref, m0_ref,
                  y_ref, c_ref, n_ref, m_ref, *, chunk):
    L = chunk

    @pl.when(pl.program_id(1) == 0)
    def _():
        c_ref[...] = c0_ref[...]
        n_ref[...] = n0_ref[...]
        m_ref[...] = m0_ref[...]

    gates = g_ref[0] + gb_ref[...]
    logf = jnp.minimum(gates, 0.0) - jnp.log1p(jnp.exp(-jnp.abs(gates)))
    row = lax.broadcasted_iota(jnp.int32, (L, L), 0)
    col = lax.broadcasted_iota(jnp.int32, (L, L), 1)
    eye = row == col
    tril = col <= row
    triu = row <= col
    k_scale = ML_HEAD_DIM ** -0.5

    for hd in range(ML_HEADS):
        sl = slice(hd * ML_HEAD_DIM, (hd + 1) * ML_HEAD_DIM)
        ig_col = gates[:, hd:hd + 1]
        lf_col = logf[:, ML_HEADS + hd:ML_HEADS + hd + 1]
        lf_row = jnp.sum(jnp.where(eye, lf_col, 0.0), axis=0, keepdims=True)
        ig_row = jnp.sum(jnp.where(eye, ig_col, 0.0), axis=0, keepdims=True)
        a_col = jnp.sum(jnp.where(tril, lf_row, 0.0), axis=1, keepdims=True)
        a_row = jnp.sum(jnp.where(triu, lf_col, 0.0), axis=0, keepdims=True)
        b = a_col[L - 1:L, :]
        m_prev = m_ref[0, hd:hd + 1, 0:1]
        c_prev = c_ref[0, hd]
        n_prev = n_ref[0, hd:hd + 1, :]

        logd = jnp.where(tril, a_col - a_row + ig_row, NEG_BIG)
        inter = a_col + m_prev
        m_row = jnp.maximum(inter, jnp.max(logd, axis=1, keepdims=True))
        dmat = jnp.exp(logd - m_row)
        w_inter = jnp.exp(inter - m_row)

        q = q_ref[0, :, sl].astype(F32)
        qb = q.astype(BF16)
        kf = k_ref[0, :, sl].astype(F32) * k_scale
        kb = kf.astype(BF16)
        vb = v_ref[0, :, sl].astype(BF16)

        s = _nt_dot(qb, kb) * dmat
        num = jnp.dot(s.astype(BF16), vb, preferred_element_type=F32)
        num = num + w_inter * _nt_dot(qb, c_prev.astype(BF16))
        den = jnp.sum(s, axis=1, keepdims=True) + w_inter * jnp.sum(q * n_prev, axis=1, keepdims=True)
        h = num / jnp.maximum(jnp.abs(den), jnp.exp(-m_row))

        g_col = b - a_col + ig_col
        m_new = jnp.maximum(b + m_prev, jnp.max(g_col, axis=0, keepdims=True))
        wk = jnp.exp(g_col - m_new)
        decay = jnp.exp(b + m_prev - m_new)
        kw = kf * wk
        c_ref[0, hd] = decay * c_prev + _tn_dot(vb, kw.astype(BF16))
        n_ref[0, hd:hd + 1, :] = decay * n_prev + jnp.sum(kw, axis=0, keepdims=True)
        m_ref[0, hd:hd + 1, :] = jnp.broadcast_to(m_new, (1, LANES))

        hm = h * _sigmoid(o_ref[0, :, sl].astype(F32))
        hm = hm * lax.rsqrt(jnp.mean(hm * hm, axis=-1, keepdims=True) + EPS) * hn_ref[:, sl]
        y_ref[0, :, sl] = (hm * _silu(z_ref[0, :, sl].astype(F32))).astype(y_ref.dtype)


def _mlstm(proj3, gates3, gate_bias, head_norm, c0, n0, m0, chunk):
    b, t, _ = proj3.shape
    w = BRANCH_WIDTH
    h, d = ML_HEADS, ML_HEAD_DIM
    col = lambda c: pl.BlockSpec((1, chunk, w), lambda bi, i, c=c: (bi, i, c))
    state = lambda shape: pl.BlockSpec((1,) + shape, lambda bi, i: (bi,) + (0,) * len(shape))
    kernel = functools.partial(_mlstm_kernel, chunk=chunk)
    return pl.pallas_call(
        kernel,
        grid=(b, t // chunk),
        in_specs=[col(COL_MQ), col(COL_MK), col(COL_MV), col(COL_MO), col(COL_MZ),
                  pl.BlockSpec((1, chunk, LANES), lambda bi, i: (bi, i, 0)),
                  pl.BlockSpec((1, LANES), lambda bi, i: (0, 0)),
                  pl.BlockSpec((1, w), lambda bi, i: (0, 0)),
                  state((h, d, d)), state((h, d)), state((h, LANES))],
        out_specs=[pl.BlockSpec((1, chunk, w), lambda bi, i: (bi, i, 0)),
                   state((h, d, d)), state((h, d)), state((h, LANES))],
        out_shape=[jax.ShapeDtypeStruct((b, t, w), BF16),
                   jax.ShapeDtypeStruct((b, h, d, d), F32),
                   jax.ShapeDtypeStruct((b, h, d), F32),
                   jax.ShapeDtypeStruct((b, h, LANES), F32)],
        compiler_params=_params(("parallel", "arbitrary"), 48),
        name="mlstm",
    )(proj3, proj3, proj3, proj3, proj3, gates3, gate_bias, head_norm, c0, n0, m0)


def _pool_kernel(u_ref, halo_ref, hist_ref, z_ref, w_ref, sc_ref, y_ref, *, tp, pos0):
    i = pl.program_id(1)
    prev = jnp.where(i == 0, hist_ref[0].astype(F32), halo_ref[0].astype(F32))
    u = u_ref[0].astype(F32)
    pos = pos0 + i * tp + lax.broadcasted_iota(jnp.int32, (tp, 1), 0)
    for gi, win in enumerate(POOL_WINDOWS):
        sl = slice(gi * POOL_GROUP_DIM, (gi + 1) * POOL_GROUP_DIM)
        ext = jnp.concatenate([prev[:, sl], u[:, sl]], axis=0)
        acc = ext
        span = 1
        while span < win:
            acc = acc + pltpu.roll(acc, span, 0)
            span *= 2
        cnt = jnp.minimum(pos + 1, win).astype(F32)
        mean = acc[POOL_HALO:, :] / cnt
        m = (mean - u[:, sl]).astype(BF16)
        y = jnp.dot(m, w_ref[gi], preferred_element_type=F32) * sc_ref[:, sl]
        y_ref[0, :, sl] = (y * _silu(z_ref[0, :, sl].astype(F32))).astype(y_ref.dtype)


def _pool(proj3, hist, w_group, scale, *, tp, pos0):
    b, t, _ = proj3.shape
    w = BRANCH_WIDTH
    halo_blocks = tp // POOL_HALO
    kernel = functools.partial(_pool_kernel, tp=tp, pos0=pos0)
    return pl.pallas_call(
        kernel,
        grid=(b, t // tp),
        in_specs=[pl.BlockSpec((1, tp, w), lambda bi, i: (bi, i, COL_PU)),
                  pl.BlockSpec((1, POOL_HALO, w), lambda bi, i: (bi, jnp.maximum(i * halo_blocks - 1, 0), COL_PU)),
                  pl.BlockSpec((1, POOL_HALO, w), lambda bi, i: (bi, 0, 0)),
                  pl.BlockSpec((1, tp, w), lambda bi, i: (bi, i, COL_PZ)),
                  pl.BlockSpec((len(POOL_WINDOWS), POOL_GROUP_DIM, POOL_GROUP_DIM), lambda bi, i: (0, 0, 0)),
                  pl.BlockSpec((1, w), lambda bi, i: (0, 0))],
        out_specs=pl.BlockSpec((1, tp, w), lambda bi, i: (bi, i, 0)),
        out_shape=jax.ShapeDtypeStruct((b, t, w), BF16),
        compiler_params=_params(("parallel", "arbitrary"), 32),
        name="pool",
    )(proj3, proj3, hist, proj3, w_group, scale)


def _merge_kernel(yp_ref, ym_ref, ya_ref, g0_ref, g1_ref, g2_ref, x_ref, wb_ref, wo_ref, o_ref):
    merged = _sigmoid(g0_ref[...].astype(F32)) * jnp.dot(yp_ref[...], wb_ref[0], preferred_element_type=F32)
    merged = merged + _sigmoid(g1_ref[...].astype(F32)) * jnp.dot(ym_ref[...], wb_ref[1], preferred_element_type=F32)
    merged = merged + _sigmoid(g2_ref[...].astype(F32)) * jnp.dot(ya_ref[...], wb_ref[2], preferred_element_type=F32)
    o_ref[...] = x_ref[...] + jnp.dot(merged.astype(BF16), wo_ref[...], preferred_element_type=F32)


def _resident(shape):
    return pl.BlockSpec(shape, lambda i: (0,) * len(shape), pipeline_mode=pl.Buffered(1))


def _merge(y_pool, y_ml, y_att, proj, x2, w_branch, w_out, tm):
    n, d = x2.shape
    w = BRANCH_WIDTH
    row = lambda width: pl.BlockSpec((tm, width), lambda i: (i, 0))
    gate = lambda k: pl.BlockSpec((tm, d), lambda i, k=k: (i, k))
    return pl.pallas_call(
        _merge_kernel,
        grid=(n // tm,),
        in_specs=[row(w), row(w), row(w), gate(0), gate(1), gate(2), row(d),
                  _resident((N_MERGE_GATES, w, d)), _resident((d, d))],
        out_specs=row(d),
        out_shape=jax.ShapeDtypeStruct((n, d), F32),
        compiler_params=_params(("parallel",), 56),
        name="merge",
    )(y_pool, y_ml, y_att, proj, proj, proj, x2, w_branch, w_out)


def _ple_kernel(x_ref, p_ref, g_ref, wg_ref, wp_ref, o_ref):
    x = x_ref[...]
    h = (x * lax.rsqrt(jnp.mean(x * x, axis=-1, keepdims=True) + EPS) * g_ref[...]).astype(BF16)
    pg = _sigmoid(jnp.dot(h, wg_ref[...], preferred_element_type=F32))
    pp = jnp.dot(p_ref[...].astype(BF16), wp_ref[...], preferred_element_type=F32)
    o_ref[...] = x + pg * pp


def _ple(x2, p2, gain, w_gate, w_proj, tm):
    n, d = x2.shape
    dp = p2.shape[1]
    return pl.pallas_call(
        _ple_kernel,
        grid=(n // tm,),
        in_specs=[pl.BlockSpec((tm, d), lambda i: (i, 0)),
                  pl.BlockSpec((tm, dp), lambda i: (i, 0)),
                  _resident((1, d)), _resident((d, d)), _resident((dp, d))],
        out_specs=pl.BlockSpec((tm, d), lambda i: (i, 0)),
        out_shape=jax.ShapeDtypeStruct((n, d), F32),
        compiler_params=_params(("parallel",), 48),
        name="ple",
    )(x2, p2, gain, w_gate, w_proj)


def _tile(n, preferred):
    t = min(n, preferred)
    assert n % t == 0, (n, t)
    return t


def _layer_weights(norm_g, w_in, w_pool_group, pool_scale, b_ig, b_fg, ml_head_norm, att_q_norm, att_k_norm,
                   att_rel_bias, w_branch, w_out, ple_norm, w_ple_gate, w_ple_proj):
    d = w_in.shape[0]
    assert d == 2 * BRANCH_WIDTH and w_in.shape[1] - W_IN_MERGE_START == N_MERGE_GATES * d
    w_main = jnp.concatenate([w_in[:, W_IN_MERGE_START:], w_in[:, :W_IN_GATES_START],
                              w_in[:, W_IN_GATES_END:W_IN_MERGE_START]], axis=1).astype(BF16)
    n_gate = W_IN_GATES_END - W_IN_GATES_START
    w_gate = jnp.pad(w_in[:, W_IN_GATES_START:W_IN_GATES_END], ((0, 0), (0, LANES - n_gate))).astype(BF16)
    gate_bias = jnp.pad(jnp.concatenate([b_ig, b_fg]), (0, LANES - n_gate))[None].astype(F32)
    return dict(
        norm_g=norm_g[None], w_main=w_main, w_gate=w_gate, gate_bias=gate_bias,
        w_pool=w_pool_group.astype(BF16), pool_scale=pool_scale[None], head_norm=ml_head_norm[None],
        gq=att_q_norm[None], gk=att_k_norm[None], rel_bias=att_rel_bias,
        w_branch=w_branch.astype(BF16), w_out=w_out.astype(BF16), ple_norm=ple_norm[None],
        w_ple_gate=w_ple_gate.astype(BF16), w_ple_proj=w_ple_proj.astype(BF16))


def _layer(x3, p3, hist, c0, n0, m0, kv_cache, pos0, lw):
    b, t, d = x3.shape
    n = b * t
    w = BRANCH_WIDTH
    prompt = kv_cache is None
    assert t % CHUNK == 0 and t >= POOL_BUF

    proj, gates = _inproj(x3.reshape(n, d), lw["norm_g"], lw["w_main"], lw["w_gate"], _tile(n, 1024))
    qn, kn = _qknorm(proj, lw["gq"], lw["gk"], _tile(n, 512))
    proj3 = proj.reshape(b, t, -1)
    qn3, kn3 = qn.reshape(b, t, w), kn.reshape(b, t, w)

    if prompt:
        tq, group = _tile(t, ATT_WINDOW), 2 * CHUNK
        assert tq == ATT_WINDOW
        kprev, vprev, vprev_col = kn3, proj3, COL_AV
    else:
        tq, group = t, CHUNK
        assert t == CHUNK
        ck, cv = kv_cache
        assert ck.shape[1] == ATT_WINDOW
        kprev, vprev, vprev_col = ck.reshape(b, ATT_WINDOW, w), cv.reshape(b, ATT_WINDOW, w), 0
    bias = _attn_bias(lw["rel_bias"], group)
    y_att = _attention(qn3, kprev, kn3, vprev, vprev_col, proj3, bias, tq=tq, group=group, prompt=prompt)

    m0b = jnp.broadcast_to(m0[:, :, None], m0.shape + (LANES,))
    y_ml, c1, n1, m1 = _mlstm(proj3, gates.reshape(b, t, LANES), lw["gate_bias"], lw["head_norm"],
                              c0, n0, m0b, _tile(t, 256))

    hist_pad = jnp.pad(hist, ((0, 0), (POOL_HALO - POOL_BUF, 0), (0, 0)))
    y_pool = _pool(proj3, hist_pad, lw["w_pool"], lw["pool_scale"], tp=_tile(t, 512), pos0=pos0)

    x1 = _merge(y_pool.reshape(n, w), y_ml.reshape(n, w), y_att.reshape(n, w), proj, x3.reshape(n, d),
                lw["w_branch"], lw["w_out"], _tile(n, 256))
    x2 = _ple(x1, p3.reshape(n, -1), lw["ple_norm"], lw["w_ple_gate"], lw["w_ple_proj"], _tile(n, 512))

    keep = min(ATT_WINDOW, t) if prompt else t
    new_pool = proj3[:, t - POOL_BUF:, COL_PU * w:(COL_PU + 1) * w].astype(F32)
    new_k = kn3[:, t - keep:].reshape(b, keep, ATT_HEADS, ATT_HEAD_DIM)
    new_v = proj3[:, t - keep:, COL_AV * w:(COL_AV + 1) * w].astype(F32).reshape(b, keep, ATT_HEADS, ATT_HEAD_DIM)
    return x2.reshape(b, t, d), (new_pool, c1, n1, m1[:, :, 0], new_k, new_v)


def kernel(x_prompt, x_sample, cache_att_k, cache_att_v, state_pool, state_mlstm_c, state_mlstm_n, state_mlstm_m, p_prompt, p_sample, norm_mix, w_in, w_pool_group, pool_scale, b_ig, b_fg, ml_head_norm, att_q_norm, att_k_norm, att_rel_bias, w_branch, w_out, ple_norm, w_ple_gate, w_ple_proj):
    xp, xs = x_prompt, x_sample
    bp = x_prompt.shape[0]
    depth = w_in.shape[0]
    sp = [[] for _ in range(6)]
    ss = [[] for _ in range(6)]
    for i in range(depth):
        lw = _layer_weights(norm_mix[i], w_in[i], w_pool_group[i], pool_scale[i], b_ig[i], b_fg[i],
                            ml_head_norm[i], att_q_norm[i], att_k_norm[i], att_rel_bias[i], w_branch[i],
                            w_out[i], ple_norm[i], w_ple_gate[i], w_ple_proj[i])
        hist0 = jnp.zeros((bp, POOL_BUF, BRANCH_WIDTH), F32)
        c0 = jnp.zeros((bp, ML_HEADS, ML_HEAD_DIM, ML_HEAD_DIM), F32)
        n0 = jnp.zeros((bp, ML_HEADS, ML_HEAD_DIM), F32)
        m0 = jnp.zeros((bp, ML_HEADS), F32)
        xp, st_p = _layer(xp, p_prompt[i], hist0, c0, n0, m0, None, 0, lw)
        xs, st_s = _layer(xs, p_sample[i], state_pool[i], state_mlstm_c[i], state_mlstm_n[i], state_mlstm_m[i],
                          (cache_att_k[i], cache_att_v[i]), PAST_LEN, lw)
        for j in range(6):
            sp[j].append(st_p[j])
            ss[j].append(st_s[j])
    pool_p, c_p, n_p, m_p, k_p, v_p = [jnp.stack(a) for a in sp]
    pool_s, c_s, n_s, m_s, k_s, v_s = [jnp.stack(a) for a in ss]
    return (xp, xs, pool_p, pool_s, c_p, c_s, n_p, n_s, m_p, m_s, k_p, k_s, v_p, v_s)
```

```python
import functools
import math

import numpy as np
import jax
import jax.numpy as jnp
from jax import lax
from jax.experimental import pallas as pl
from jax.experimental.pallas import tpu as pltpu

F32 = jnp.float32
BF16 = jnp.bfloat16

EPS = 1e-6
CHUNK = 64
PAST_LEN = 1024

POOL_WINDOWS = (2, 4, 8, 16)
POOL_GROUP_DIM = 256
POOL_BUF = 15
POOL_HALO = 16

ML_HEADS = 4
ML_HEAD_DIM = 256

ATT_HEADS = 8
ATT_HEAD_DIM = 128
ATT_WINDOW = 512
REL_CLIP = 256
ATT_KEYS = 640

BRANCH_WIDTH = 1024
LANES = 128

N_BRANCH = 3
COL_PU, COL_PZ, COL_MQ, COL_MK, COL_MV, COL_MO, COL_MZ, COL_AQ, COL_AK, COL_AV, COL_AZ = range(11)
COL_GATES = 11
W_IN_GATES_START = 7168
W_IN_GATES_END = 7176

PROJ_DTYPE = jnp.float32
NEG_BIG = -1e30
MIB = 1024 * 1024


def _params(semantics, vmem_mib):
    return pltpu.CompilerParams(dimension_semantics=semantics, vmem_limit_bytes=vmem_mib * MIB)


def _sigmoid(x):
    return 1.0 / (1.0 + jnp.exp(-x))


def _silu(x):
    return x * _sigmoid(x)


def _nt_dot(a, b):
    return lax.dot_general(a, b, (((1,), (1,)), ((), ())), preferred_element_type=F32)


def _tn_dot(a, b):
    return lax.dot_general(a, b, (((0,), (0,)), ((), ())), preferred_element_type=F32)


def _inproj_kernel(x_ref, g_ref, wa_ref, wb_ref, wg_ref, o_ref, og_ref, h_ref, *, blocks_a):
    j = pl.program_id(1)

    @pl.when(j == 0)
    def _():
        x = x_ref[...]
        ms = jnp.mean(x * x, axis=-1, keepdims=True)
        h = (x * lax.rsqrt(ms + EPS) * g_ref[...]).astype(BF16)
        h_ref[...] = h
        og_ref[...] = jnp.dot(h, wg_ref[...], preferred_element_type=F32)

    @pl.when(j < blocks_a)
    def _():
        o_ref[...] = jnp.dot(h_ref[...], wa_ref[...], preferred_element_type=F32).astype(o_ref.dtype)

    @pl.when(j >= blocks_a)
    def _():
        o_ref[...] = jnp.dot(h_ref[...], wb_ref[...], preferred_element_type=F32).astype(o_ref.dtype)


def _inproj(x2, gain, w_a, w_b, w_gate, tm):
    n, d = x2.shape
    tn = BRANCH_WIDTH
    blocks_a, blocks_b = w_a.shape[1] // tn, w_b.shape[1] // tn
    kernel = functools.partial(_inproj_kernel, blocks_a=blocks_a)
    return pl.pallas_call(
        kernel,
        grid=(n // tm, blocks_a + blocks_b),
        in_specs=[pl.BlockSpec((tm, d), lambda i, j: (i, 0)),
                  pl.BlockSpec((1, d), lambda i, j: (0, 0)),
                  pl.BlockSpec((d, tn), lambda i, j: (0, jnp.minimum(j, blocks_a - 1))),
                  pl.BlockSpec((d, tn), lambda i, j: (0, jnp.maximum(j - blocks_a, 0))),
                  pl.BlockSpec((d, LANES), lambda i, j: (0, 0))],
        out_specs=[pl.BlockSpec((tm, tn), lambda i, j: (i, j)),
                   pl.BlockSpec((tm, LANES), lambda i, j: (i, 0))],
        out_shape=[jax.ShapeDtypeStruct((n, (blocks_a + blocks_b) * tn), PROJ_DTYPE),
                   jax.ShapeDtypeStruct((n, LANES), F32)],
        scratch_shapes=[pltpu.VMEM((tm, d), BF16)],
        compiler_params=_params(("parallel", "arbitrary"), 56),
        name="inproj",
    )(x2, gain, w_a, w_b, w_gate)


def _qknorm_kernel(q_ref, k_ref, gq_ref, gk_ref, qo_ref, ko_ref):
    for h in range(ATT_HEADS):
        sl = slice(h * ATT_HEAD_DIM, (h + 1) * ATT_HEAD_DIM)
        q = q_ref[:, sl].astype(F32)
        qn = q * lax.rsqrt(jnp.mean(q * q, axis=-1, keepdims=True) + EPS) * gq_ref[...]
        qo_ref[:, sl] = qn.astype(qo_ref.dtype)
        k = k_ref[:, sl].astype(F32)
        kn = k * lax.rsqrt(jnp.mean(k * k, axis=-1, keepdims=True) + EPS) * gk_ref[...]
        ko_ref[:, sl] = kn.astype(ko_ref.dtype)


def _qknorm(proj, gq, gk, tm):
    n = proj.shape[0]
    w = BRANCH_WIDTH
    return pl.pallas_call(
        _qknorm_kernel,
        grid=(n // tm,),
        in_specs=[pl.BlockSpec((tm, w), lambda i: (i, COL_AQ)),
                  pl.BlockSpec((tm, w), lambda i: (i, COL_AK)),
                  pl.BlockSpec((1, ATT_HEAD_DIM), lambda i: (0, 0)),
                  pl.BlockSpec((1, ATT_HEAD_DIM), lambda i: (0, 0))],
        out_specs=[pl.BlockSpec((tm, w), lambda i: (i, 0)),
                   pl.BlockSpec((tm, w), lambda i: (i, 0))],
        out_shape=[jax.ShapeDtypeStruct((n, w), BF16),
                   jax.ShapeDtypeStruct((n, w), F32)],
        compiler_params=_params(("parallel",), 32),
        name="qknorm",
    )(proj, proj, gq, gk)


def _attn_bias(rel_bias, group):
    period = ATT_KEYS + group
    m = np.arange(period)
    m = np.where(m <= ATT_KEYS, m, m - period)
    idx = np.clip(ATT_WINDOW - m, -REL_CLIP, REL_CLIP) + REL_CLIP
    vec = rel_bias[:, idx].astype(F32)
    heads = rel_bias.shape[0]
    bias = jnp.tile(vec, (1, group))[:, :group * (period - 1)].reshape(heads, group, period - 1)[:, :, :ATT_KEYS]
    i = np.arange(group)[:, None]
    j = np.arange(ATT_KEYS)[None, :]
    lo = (i // CHUNK) * CHUNK
    band = (j >= lo) & (j < lo + ATT_WINDOW + CHUNK)
    return jnp.where(jnp.asarray(band)[None], bias, NEG_BIG)


def _attn_kernel(q_ref, kp_ref, kc_ref, vp_ref, vc_ref, az_ref, bias_ref, o_ref, kw_ref, vw_ref,
                 *, tq, group, first_prev_invalid, prev_by_head):
    rows = kw_ref.shape[0]
    if prev_by_head:
        for h in range(ATT_HEADS):
            sl = slice(h * ATT_HEAD_DIM, (h + 1) * ATT_HEAD_DIM)
            kw_ref[0:ATT_WINDOW, sl] = kp_ref[:, h, :].astype(BF16)
            vw_ref[0:ATT_WINDOW, sl] = vp_ref[:, h, :].astype(BF16)
    else:
        kw_ref[0:ATT_WINDOW, :] = kp_ref[0].astype(BF16)
        vw_ref[0:ATT_WINDOW, :] = vp_ref[0].astype(BF16)
    kw_ref[ATT_WINDOW:ATT_WINDOW + tq, :] = kc_ref[0].astype(BF16)
    vw_ref[ATT_WINDOW:ATT_WINDOW + tq, :] = vc_ref[0].astype(BF16)
    if rows > ATT_WINDOW + tq:
        pad = jnp.zeros((rows - ATT_WINDOW - tq, kw_ref.shape[1]), BF16)
        kw_ref[ATT_WINDOW + tq:rows, :] = pad
        vw_ref[ATT_WINDOW + tq:rows, :] = pad

    scale = ATT_HEAD_DIM ** -0.5
    first = pl.program_id(1) == 0
    col = lax.broadcasted_iota(jnp.int32, (group, ATT_KEYS), 1)
    for g in range(tq // group):
        r0 = g * group
        if first_prev_invalid:
            dead = jnp.logical_and(first, col + r0 < ATT_WINDOW)
        for h in range(ATT_HEADS):
            sl = slice(h * ATT_HEAD_DIM, (h + 1) * ATT_HEAD_DIM)
            q = q_ref[0, r0:r0 + group, sl]
            k = kw_ref[r0:r0 + ATT_KEYS, sl]
            v = vw_ref[r0:r0 + ATT_KEYS, sl]
            s = _nt_dot(q, k) * scale + bias_ref[h]
            if first_prev_invalid:
                s = jnp.where(dead, NEG_BIG, s)
            m = jnp.max(s, axis=-1, keepdims=True)
            p = jnp.exp(s - m)
            p = p / jnp.sum(p, axis=-1, keepdims=True)
            o = jnp.dot(p.astype(BF16), v, preferred_element_type=F32)
            z = az_ref[0, r0:r0 + group, sl].astype(F32)
            o_ref[0, r0:r0 + group, sl] = (o * _silu(z)).astype(o_ref.dtype)


def _attention(qn3, kprev, kn3, vprev, proj3, bias, *, tq, group, cache_layer):
    b, t, w = qn3.shape
    nt = t // tq
    prompt = cache_layer is None
    if prompt:
        assert tq == ATT_WINDOW
        kprev_spec = pl.BlockSpec((1, ATT_WINDOW, w), lambda bi, i: (bi, jnp.maximum(i - 1, 0), 0))
        vprev_spec = pl.BlockSpec((1, ATT_WINDOW, w), lambda bi, i: (bi, jnp.maximum(i - 1, 0), COL_AV))
    else:
        assert nt == 1 and kprev.shape[2:] == (ATT_WINDOW, ATT_HEADS, ATT_HEAD_DIM)
        kprev_spec = pl.BlockSpec((None, None, ATT_WINDOW, ATT_HEADS, ATT_HEAD_DIM),
                                  lambda bi, i: (cache_layer, bi, 0, 0, 0))
        vprev_spec = kprev_spec
    rows = tq + ATT_KEYS - group
    kernel = functools.partial(_attn_kernel, tq=tq, group=group, first_prev_invalid=prompt, prev_by_head=not prompt)
    return pl.pallas_call(
        kernel,
        grid=(b, nt),
        in_specs=[pl.BlockSpec((1, tq, w), lambda bi, i: (bi, i, 0)),
                  kprev_spec,
                  pl.BlockSpec((1, tq, w), lambda bi, i: (bi, i, 0)),
                  vprev_spec,
                  pl.BlockSpec((1, tq, w), lambda bi, i: (bi, i, COL_AV)),
                  pl.BlockSpec((1, tq, w), lambda bi, i: (bi, i, COL_AZ)),
                  pl.BlockSpec((ATT_HEADS, group, ATT_KEYS), lambda bi, i: (0, 0, 0))],
        out_specs=pl.BlockSpec((1, tq, w), lambda bi, i: (bi, i, 0)),
        out_shape=jax.ShapeDtypeStruct((b, t, w), BF16),
        scratch_shapes=[pltpu.VMEM((rows, w), BF16), pltpu.VMEM((rows, w), BF16)],
        compiler_params=_params(("parallel", "arbitrary"), 48),
        name="attention",
    )(qn3, kprev, kn3, vprev, proj3, proj3, bias)


def _mlstm_kernel(q_ref, k_ref, v_ref, o_ref, z_ref, g_ref, gb_ref, hn_ref, c0_ref, n0_ref, m0_ref,
                  y_ref, c_ref, n_ref, m_ref, *, chunk):
    L = chunk

    @pl.when(pl.program_id(1) == 0)
    def _():
        c_ref[...] = c0_ref[...]
        n_ref[...] = n0_ref[...]
        m_ref[...] = m0_ref[...]

    gates = g_ref[0] + gb_ref[...]
    logf = jnp.minimum(gates, 0.0) - jnp.log1p(jnp.exp(-jnp.abs(gates)))
    row = lax.broadcasted_iota(jnp.int32, (L, L), 0)
    col = lax.broadcasted_iota(jnp.int32, (L, L), 1)
    eye = row == col
    tril = col <= row
    triu = row <= col
    k_scale = ML_HEAD_DIM ** -0.5

    for hd in range(ML_HEADS):
        sl = slice(hd * ML_HEAD_DIM, (hd + 1) * ML_HEAD_DIM)
        ig_col = gates[:, hd:hd + 1]
        lf_col = logf[:, ML_HEADS + hd:ML_HEADS + hd + 1]
        lf_row = jnp.sum(jnp.where(eye, lf_col, 0.0), axis=0, keepdims=True)
        ig_row = jnp.sum(jnp.where(eye, ig_col, 0.0), axis=0, keepdims=True)
        a_col = jnp.sum(jnp.where(tril, lf_row, 0.0), axis=1, keepdims=True)
        a_row = jnp.sum(jnp.where(triu, lf_col, 0.0), axis=0, keepdims=True)
        b = a_col[L - 1:L, :]
        m_prev = m_ref[0, hd:hd + 1, 0:1]
        c_prev = c_ref[0, hd]
        n_prev = n_ref[0, hd:hd + 1, :]

        logd = jnp.where(tril, a_col - a_row + ig_row, NEG_BIG)
        inter = a_col + m_prev
        m_row = jnp.maximum(inter, jnp.max(logd, axis=1, keepdims=True))
        dmat = jnp.exp(logd - m_row)
        w_inter = jnp.exp(inter - m_row)

        q = q_ref[0, :, sl].astype(F32)
        qb = q.astype(BF16)
        kf = k_ref[0, :, sl].astype(F32) * k_scale
        kb = kf.astype(BF16)
        vb = v_ref[0, :, sl].astype(BF16)

        s = _nt_dot(qb, kb) * dmat
        num = jnp.dot(s.astype(BF16), vb, preferred_element_type=F32)
        num = num + w_inter * _nt_dot(qb, c_prev.astype(BF16))
        den = jnp.sum(s, axis=1, keepdims=True) + w_inter * jnp.sum(q * n_prev, axis=1, keepdims=True)
        h = num / jnp.maximum(jnp.abs(den), jnp.exp(-m_row))

        g_col = b - a_col + ig_col
        m_new = jnp.maximum(b + m_prev, jnp.max(g_col, axis=0, keepdims=True))
        wk = jnp.exp(g_col - m_new)
        decay = jnp.exp(b + m_prev - m_new)
        kw = kf * wk
        c_ref[0, hd] = decay * c_prev + _tn_dot(vb, kw.astype(BF16))
        n_ref[0, hd:hd + 1, :] = decay * n_prev + jnp.sum(kw, axis=0, keepdims=True)
        m_ref[0, hd:hd + 1, :] = jnp.broadcast_to(m_new, (1, LANES))

        hm = h * _sigmoid(o_ref[0, :, sl].astype(F32))
        hm = hm * lax.rsqrt(jnp.mean(hm * hm, axis=-1, keepdims=True) + EPS) * hn_ref[:, sl]
        y_ref[0, :, sl] = (hm * _silu(z_ref[0, :, sl].astype(F32))).astype(y_ref.dtype)


def _mlstm(proj3, gates3, gate_bias, head_norm, c0, n0, m0, chunk):
    b, t, _ = proj3.shape
    w = BRANCH_WIDTH
    h, d = ML_HEADS, ML_HEAD_DIM
    col = lambda c: pl.BlockSpec((1, chunk, w), lambda bi, i, c=c: (bi, i, c))
    state = lambda shape: pl.BlockSpec((1,) + shape, lambda bi, i: (bi,) + (0,) * len(shape))
    kernel = functools.partial(_mlstm_kernel, chunk=chunk)
    return pl.pallas_call(
        kernel,
        grid=(b, t // chunk),
        in_specs=[col(COL_MQ), col(COL_MK), col(COL_MV), col(COL_MO), col(COL_MZ),
                  pl.BlockSpec((1, chunk, LANES), lambda bi, i: (bi, i, 0)),
                  pl.BlockSpec((1, LANES), lambda bi, i: (0, 0)),
                  pl.BlockSpec((1, w), lambda bi, i: (0, 0)),
                  state((h, d, d)), state((h, d)), state((h, LANES))],
        out_specs=[pl.BlockSpec((1, chunk, w), lambda bi, i: (bi, i, 0)),
                   state((h, d, d)), state((h, d)), state((h, LANES))],
        out_shape=[jax.ShapeDtypeStruct((b, t, w), BF16),
                   jax.ShapeDtypeStruct((b, h, d, d), F32),
                   jax.ShapeDtypeStruct((b, h, d), F32),
                   jax.ShapeDtypeStruct((b, h, LANES), F32)],
        compiler_params=_params(("parallel", "arbitrary"), 48),
        name="mlstm",
    )(proj3, proj3, proj3, proj3, proj3, gates3, gate_bias, head_norm, c0, n0, m0)


def _pool_kernel(u_ref, halo_ref, hist_ref, z_ref, w_ref, sc_ref, y_ref, *, tp, pos0):
    i = pl.program_id(1)
    prev = jnp.where(i == 0, hist_ref[0].astype(F32), halo_ref[0].astype(F32))
    u = u_ref[0].astype(F32)
    pos = pos0 + i * tp + lax.broadcasted_iota(jnp.int32, (tp, 1), 0)
    for gi, win in enumerate(POOL_WINDOWS):
        sl = slice(gi * POOL_GROUP_DIM, (gi + 1) * POOL_GROUP_DIM)
        ext = jnp.concatenate([prev[:, sl], u[:, sl]], axis=0)
        acc = ext
        span = 1
        while span < win:
            acc = acc + pltpu.roll(acc, span, 0)
            span *= 2
        cnt = jnp.minimum(pos + 1, win).astype(F32)
        mean = acc[POOL_HALO:, :] / cnt
        m = (mean - u[:, sl]).astype(BF16)
        y = jnp.dot(m, w_ref[gi], preferred_element_type=F32) * sc_ref[:, sl]
        y_ref[0, :, sl] = (y * _silu(z_ref[0, :, sl].astype(F32))).astype(y_ref.dtype)


def _pool(proj3, hist, w_group, scale, *, tp, pos0):
    b, t, _ = proj3.shape
    w = BRANCH_WIDTH
    halo_blocks = tp // POOL_HALO
    kernel = functools.partial(_pool_kernel, tp=tp, pos0=pos0)
    return pl.pallas_call(
        kernel,
        grid=(b, t // tp),
        in_specs=[pl.BlockSpec((1, tp, w), lambda bi, i: (bi, i, COL_PU)),
                  pl.BlockSpec((1, POOL_HALO, w), lambda bi, i: (bi, jnp.maximum(i * halo_blocks - 1, 0), COL_PU)),
                  pl.BlockSpec((1, POOL_HALO, w), lambda bi, i: (bi, 0, 0)),
                  pl.BlockSpec((1, tp, w), lambda bi, i: (bi, i, COL_PZ)),
                  pl.BlockSpec((len(POOL_WINDOWS), POOL_GROUP_DIM, POOL_GROUP_DIM), lambda bi, i: (0, 0, 0)),
                  pl.BlockSpec((1, w), lambda bi, i: (0, 0))],
        out_specs=pl.BlockSpec((1, tp, w), lambda bi, i: (bi, i, 0)),
        out_shape=jax.ShapeDtypeStruct((b, t, w), BF16),
        compiler_params=_params(("parallel", "arbitrary"), 32),
        name="pool",
    )(proj3, proj3, hist, proj3, w_group, scale)


def _merge_kernel(*refs):
    y_refs = refs[:N_BRANCH]
    gate_refs = refs[N_BRANCH:-4]
    x_ref, wb_ref, wo_ref, o_ref = refs[-4:]
    per_branch = len(gate_refs) // N_BRANCH
    merged = None
    for b in range(N_BRANCH):
        gate = jnp.concatenate([_sigmoid(r[...].astype(F32)) for r in gate_refs[b * per_branch:(b + 1) * per_branch]],
                               axis=1)
        term = gate * jnp.dot(y_refs[b][...], wb_ref[b], preferred_element_type=F32)
        merged = term if merged is None else merged + term
    o_ref[...] = x_ref[...] + jnp.dot(merged.astype(BF16), wo_ref[...], preferred_element_type=F32)


def _resident(shape):
    return pl.BlockSpec(shape, lambda i: (0,) * len(shape), pipeline_mode=pl.Buffered(1))


def _merge(y_pool, y_ml, y_att, proj, x2, w_branch, w_out, tm):
    n, d = x2.shape
    w = BRANCH_WIDTH
    row = lambda width: pl.BlockSpec((tm, width), lambda i: (i, 0))
    n_gate_blocks = N_BRANCH * (d // w)
    gate_specs = [pl.BlockSpec((tm, w), lambda i, k=k: (i, COL_GATES + k)) for k in range(n_gate_blocks)]
    return pl.pallas_call(
        _merge_kernel,
        grid=(n // tm,),
        in_specs=[row(w)] * N_BRANCH + gate_specs + [row(d), _resident((N_BRANCH, w, d)), _resident((d, d))],
        out_specs=row(d),
        out_shape=jax.ShapeDtypeStruct((n, d), F32),
        compiler_params=_params(("parallel",), 56),
        name="merge",
    )(y_pool, y_ml, y_att, *([proj] * n_gate_blocks), x2, w_branch, w_out)


def _ple_kernel(x_ref, p_ref, g_ref, wg_ref, wp_ref, o_ref):
    x = x_ref[...]
    h = (x * lax.rsqrt(jnp.mean(x * x, axis=-1, keepdims=True) + EPS) * g_ref[...]).astype(BF16)
    pg = _sigmoid(jnp.dot(h, wg_ref[...], preferred_element_type=F32))
    pp = jnp.dot(p_ref[...].astype(BF16), wp_ref[...], preferred_element_type=F32)
    o_ref[...] = x + pg * pp


def _ple(x2, p2, gain, w_gate, w_proj, tm):
    n, d = x2.shape
    dp = p2.shape[1]
    return pl.pallas_call(
        _ple_kernel,
        grid=(n // tm,),
        in_specs=[pl.BlockSpec((tm, d), lambda i: (i, 0)),
                  pl.BlockSpec((tm, dp), lambda i: (i, 0)),
                  _resident((1, d)), _resident((d, d)), _resident((dp, d))],
        out_specs=pl.BlockSpec((tm, d), lambda i: (i, 0)),
        out_shape=jax.ShapeDtypeStruct((n, d), F32),
        compiler_params=_params(("parallel",), 48),
        name="ple",
    )(x2, p2, gain, w_gate, w_proj)


def _tile(n, preferred):
    t = min(n, preferred)
    assert n % t == 0, (n, t)
    return t


def _layer_weights(norm_g, w_in, w_pool_group, pool_scale, b_ig, b_fg, ml_head_norm, att_q_norm, att_k_norm,
                   att_rel_bias, w_branch, w_out, ple_norm, w_ple_gate, w_ple_proj):
    d = w_in.shape[0]
    assert w_in.shape[1] == W_IN_GATES_END + (COL_GATES - COL_AQ) * BRANCH_WIDTH + N_BRANCH * d
    w_a = w_in[:, :W_IN_GATES_START].astype(BF16)
    w_b = w_in[:, W_IN_GATES_END:].astype(BF16)
    n_gate = W_IN_GATES_END - W_IN_GATES_START
    w_gate = jnp.pad(w_in[:, W_IN_GATES_START:W_IN_GATES_END], ((0, 0), (0, LANES - n_gate))).astype(BF16)
    gate_bias = jnp.pad(jnp.concatenate([b_ig, b_fg]), (0, LANES - n_gate))[None].astype(F32)
    return dict(
        norm_g=norm_g[None], w_a=w_a, w_b=w_b, w_gate=w_gate, gate_bias=gate_bias,
        w_pool=w_pool_group.astype(BF16), pool_scale=pool_scale[None], head_norm=ml_head_norm[None],
        gq=att_q_norm[None], gk=att_k_norm[None], rel_bias=att_rel_bias,
        w_branch=w_branch.astype(BF16), w_out=w_out.astype(BF16), ple_norm=ple_norm[None],
        w_ple_gate=w_ple_gate.astype(BF16), w_ple_proj=w_ple_proj.astype(BF16))


def _layer(x3, p3, hist, c0, n0, m0, kv_cache, pos0, lw):
    b, t, d = x3.shape
    n = b * t
    w = BRANCH_WIDTH
    prompt = kv_cache is None
    assert t % CHUNK == 0 and t >= POOL_BUF

    proj, gates = _inproj(x3.reshape(n, d), lw["norm_g"], lw["w_a"], lw["w_b"], lw["w_gate"], _tile(n, 1024))
    qn, kn = _qknorm(proj, lw["gq"], lw["gk"], _tile(n, 512))
    proj3 = proj.reshape(b, t, -1)
    qn3, kn3 = qn.reshape(b, t, w), kn.reshape(b, t, w)

    if prompt:
        tq, group = _tile(t, ATT_WINDOW), 2 * CHUNK
        kprev, vprev, cache_layer = kn3, proj3, None
    else:
        tq, group = t, CHUNK
        assert t == CHUNK
        kprev, vprev, cache_layer = kv_cache
    bias = _attn_bias(lw["rel_bias"], group)
    y_att = _attention(qn3, kprev, kn3, vprev, proj3, bias, tq=tq, group=group, cache_layer=cache_layer)

    m0b = jnp.broadcast_to(m0[:, :, None], m0.shape + (LANES,))
    y_ml, c1, n1, m1 = _mlstm(proj3, gates.reshape(b, t, LANES), lw["gate_bias"], lw["head_norm"],
                              c0, n0, m0b, _tile(t, 256))

    hist_pad = jnp.pad(hist, ((0, 0), (POOL_HALO - POOL_BUF, 0), (0, 0)))
    y_pool = _pool(proj3, hist_pad, lw["w_pool"], lw["pool_scale"], tp=_tile(t, 512), pos0=pos0)

    x1 = _merge(y_pool.reshape(n, w), y_ml.reshape(n, w), y_att.reshape(n, w), proj, x3.reshape(n, d),
                lw["w_branch"], lw["w_out"], _tile(n, 256))
    x2 = _ple(x1, p3.reshape(n, -1), lw["ple_norm"], lw["w_ple_gate"], lw["w_ple_proj"], _tile(n, 512))

    keep = min(ATT_WINDOW, t) if prompt else t
    new_pool = proj3[:, t - POOL_BUF:, COL_PU * w:(COL_PU + 1) * w].astype(F32)
    new_k = kn3[:, t - keep:].reshape(b, keep, ATT_HEADS, ATT_HEAD_DIM)
    new_v = proj3[:, t - keep:, COL_AV * w:(COL_AV + 1) * w].astype(F32).reshape(b, keep, ATT_HEADS, ATT_HEAD_DIM)
    return x2.reshape(b, t, d), (new_pool, c1, n1, m1[:, :, 0], new_k, new_v)


def kernel(x_prompt, x_sample, cache_att_k, cache_att_v, state_pool, state_mlstm_c, state_mlstm_n, state_mlstm_m, p_prompt, p_sample, norm_mix, w_in, w_pool_group, pool_scale, b_ig, b_fg, ml_head_norm, att_q_norm, att_k_norm, att_rel_bias, w_branch, w_out, ple_norm, w_ple_gate, w_ple_proj):
    xp, xs = x_prompt, x_sample
    bp = x_prompt.shape[0]
    depth = w_in.shape[0]
    sp = [[] for _ in range(6)]
    ss = [[] for _ in range(6)]
    for i in range(depth):
        lw = _layer_weights(norm_mix[i], w_in[i], w_pool_group[i], pool_scale[i], b_ig[i], b_fg[i],
                            ml_head_norm[i], att_q_norm[i], att_k_norm[i], att_rel_bias[i], w_branch[i],
                            w_out[i], ple_norm[i], w_ple_gate[i], w_ple_proj[i])
        hist0 = jnp.zeros((bp, POOL_BUF, BRANCH_WIDTH), F32)
        c0 = jnp.zeros((bp, ML_HEADS, ML_HEAD_DIM, ML_HEAD_DIM), F32)
        n0 = jnp.zeros((bp, ML_HEADS, ML_HEAD_DIM), F32)
        m0 = jnp.zeros((bp, ML_HEADS), F32)
        xp, st_p = _layer(xp, p_prompt[i], hist0, c0, n0, m0, None, 0, lw)
        xs, st_s = _layer(xs, p_sample[i], state_pool[i], state_mlstm_c[i], state_mlstm_n[i], state_mlstm_m[i],
                          (cache_att_k, cache_att_v, i), PAST_LEN, lw)
        for j in range(6):
            sp[j].append(st_p[j])
            ss[j].append(st_s[j])
    pool_p, c_p, n_p, m_p, k_p, v_p = [jnp.stack(a) for a in sp]
    pool_s, c_s, n_s, m_s, k_s, v_s = [jnp.stack(a) for a in ss]
    return (xp, xs, pool_p, pool_s, c_p, c_s, n_p, n_s, m_p, m_s, k_p, k_s, v_p, v_s)
```

```python
import functools

import numpy as np
import jax
import jax.numpy as jnp
from jax import lax
from jax.experimental import pallas as pl
from jax.experimental.pallas import tpu as pltpu

F32 = jnp.float32
BF16 = jnp.bfloat16

EPS = 1e-6
CHUNK = 64
PAST_LEN = 1024

POOL_WINDOWS = (2, 4, 8, 16)
POOL_GROUP_DIM = 256
POOL_BUF = 15
POOL_HALO = 16

ML_HEADS = 4
ML_HEAD_DIM = 256

ATT_HEADS = 8
ATT_HEAD_DIM = 128
ATT_WINDOW = 512
REL_CLIP = 256
ATT_KEYS = 640

BRANCH_WIDTH = 1024
LANES = 128

N_BRANCH = 3
COL_PU, COL_PZ, COL_MQ, COL_MK, COL_MV, COL_MO, COL_MZ, COL_AQ, COL_AK, COL_AV, COL_AZ = range(11)
COL_GATES = 11
W_IN_GATES_START = 7168
W_IN_GATES_END = 7176
W_IN_GATE_COLS = W_IN_GATES_END - W_IN_GATES_START

PROJ_DTYPE = jnp.float32
NEG_BIG = -1e30
MIB = 1024 * 1024


def _params(semantics, vmem_mib):
    return pltpu.CompilerParams(dimension_semantics=semantics, vmem_limit_bytes=vmem_mib * MIB)


def _layer_spec(tail, layer, single_buffer=False):
    index_map = lambda *_: (layer,) + (0,) * len(tail)
    if single_buffer:
        return pl.BlockSpec((None,) + tuple(tail), index_map, pipeline_mode=pl.Buffered(1))
    return pl.BlockSpec((None,) + tuple(tail), index_map)


def _sigmoid(x):
    return 1.0 / (1.0 + jnp.exp(-x))


def _silu(x):
    return x * _sigmoid(x)


def _nt_dot(a, b):
    return lax.dot_general(a, b, (((1,), (1,)), ((), ())), preferred_element_type=F32)


def _tn_dot(a, b):
    return lax.dot_general(a, b, (((0,), (0,)), ((), ())), preferred_element_type=F32)


def _repack_kernel(a_ref, b_ref, o_ref, *, blocks_before):
    j = pl.program_id(2)

    @pl.when(j < blocks_before)
    def _():
        o_ref[...] = a_ref[...].astype(BF16)

    @pl.when(j >= blocks_before)
    def _():
        shifted = jnp.concatenate([a_ref[:, W_IN_GATE_COLS:], b_ref[:, :W_IN_GATE_COLS]], axis=1)
        o_ref[...] = shifted.astype(BF16)


def _repack_w_in(w_in, tr=512):
    depth, d, width = w_in.shape
    tn = BRANCH_WIDTH
    out_width = width - W_IN_GATE_COLS
    assert out_width % tn == 0 and W_IN_GATES_START % tn == 0 and d % tr == 0
    kernel = functools.partial(_repack_kernel, blocks_before=W_IN_GATES_START // tn)
    return pl.pallas_call(
        kernel,
        grid=(depth, d // tr, out_width // tn),
        in_specs=[pl.BlockSpec((None, tr, tn), lambda l, r, j: (l, r, j)),
                  pl.BlockSpec((None, tr, LANES), lambda l, r, j: (l, r, (j + 1) * (tn // LANES)))],
        out_specs=pl.BlockSpec((None, tr, tn), lambda l, r, j: (l, r, j)),
        out_shape=jax.ShapeDtypeStruct((depth, d, out_width), BF16),
        compiler_params=_params(("parallel", "parallel", "arbitrary"), 32),
        name="repack_w_in",
    )(w_in, w_in)


def _inproj_kernel(x_ref, g_ref, w_ref, wg_ref, o_ref, og_ref, h_ref):
    @pl.when(pl.program_id(1) == 0)
    def _():
        x = x_ref[...]
        ms = jnp.mean(x * x, axis=-1, keepdims=True)
        h = (x * lax.rsqrt(ms + EPS) * g_ref[...]).astype(BF16)
        h_ref[...] = h
        og_ref[...] = jnp.dot(h, wg_ref[...], preferred_element_type=F32)

    o_ref[...] = jnp.dot(h_ref[...], w_ref[...], preferred_element_type=F32).astype(o_ref.dtype)


def _inproj(x2, gain, w_main, w_gate, layer, tm):
    n, d = x2.shape
    nw = w_main.shape[2]
    tn = BRANCH_WIDTH
    return pl.pallas_call(
        _inproj_kernel,
        grid=(n // tm, nw // tn),
        in_specs=[pl.BlockSpec((tm, d), lambda i, j: (i, 0)),
                  _layer_spec((1, d), layer),
                  pl.BlockSpec((None, d, tn), lambda i, j: (layer, 0, j)),
                  _layer_spec((d, LANES), layer)],
        out_specs=[pl.BlockSpec((tm, tn), lambda i, j: (i, j)),
                   pl.BlockSpec((tm, LANES), lambda i, j: (i, 0))],
        out_shape=[jax.ShapeDtypeStruct((n, nw), PROJ_DTYPE),
                   jax.ShapeDtypeStruct((n, LANES), F32)],
        scratch_shapes=[pltpu.VMEM((tm, d), BF16)],
        compiler_params=_params(("parallel", "arbitrary"), 48),
        name="inproj",
    )(x2, gain, w_main, w_gate)


def _qknorm_kernel(q_ref, k_ref, gq_ref, gk_ref, qo_ref, ko_ref):
    for h in range(ATT_HEADS):
        sl = slice(h * ATT_HEAD_DIM, (h + 1) * ATT_HEAD_DIM)
        q = q_ref[:, sl].astype(F32)
        qn = q * lax.rsqrt(jnp.mean(q * q, axis=-1, keepdims=True) + EPS) * gq_ref[...]
        qo_ref[:, sl] = qn.astype(qo_ref.dtype)
        k = k_ref[:, sl].astype(F32)
        kn = k * lax.rsqrt(jnp.mean(k * k, axis=-1, keepdims=True) + EPS) * gk_ref[...]
        ko_ref[:, sl] = kn.astype(ko_ref.dtype)


def _qknorm(proj, gq, gk, layer, tm):
    n = proj.shape[0]
    w = BRANCH_WIDTH
    return pl.pallas_call(
        _qknorm_kernel,
        grid=(n // tm,),
        in_specs=[pl.BlockSpec((tm, w), lambda i: (i, COL_AQ)),
                  pl.BlockSpec((tm, w), lambda i: (i, COL_AK)),
                  _layer_spec((1, ATT_HEAD_DIM), layer),
                  _layer_spec((1, ATT_HEAD_DIM), layer)],
        out_specs=[pl.BlockSpec((tm, w), lambda i: (i, 0)),
                   pl.BlockSpec((tm, w), lambda i: (i, 0))],
        out_shape=[jax.ShapeDtypeStruct((n, w), BF16),
                   jax.ShapeDtypeStruct((n, w), F32)],
        compiler_params=_params(("parallel",), 32),
        name="qknorm",
    )(proj, proj, gq, gk)


def _attn_bias(rel_bias, group):
    period = ATT_KEYS + group
    m = np.arange(period)
    m = np.where(m <= ATT_KEYS, m, m - period)
    idx = np.clip(ATT_WINDOW - m, -REL_CLIP, REL_CLIP) + REL_CLIP
    vec = rel_bias[:, idx].astype(F32)
    heads = rel_bias.shape[0]
    bias = jnp.tile(vec, (1, group))[:, :group * (period - 1)].reshape(heads, group, period - 1)[:, :, :ATT_KEYS]
    i = np.arange(group)[:, None]
    j = np.arange(ATT_KEYS)[None, :]
    lo = (i // CHUNK) * CHUNK
    band = (j >= lo) & (j < lo + ATT_WINDOW + CHUNK)
    return jnp.where(jnp.asarray(band)[None], bias, NEG_BIG)


def _attn_kernel(q_ref, kp_ref, kc_ref, vp_ref, vc_ref, az_ref, bias_ref, o_ref, kw_ref, vw_ref,
                 *, tq, group, first_prev_invalid, prev_by_head):
    rows = kw_ref.shape[0]
    if prev_by_head:
        for h in range(ATT_HEADS):
            sl = slice(h * ATT_HEAD_DIM, (h + 1) * ATT_HEAD_DIM)
            kw_ref[0:ATT_WINDOW, sl] = kp_ref[pl.ds(h, ATT_WINDOW, stride=ATT_HEADS), :].astype(BF16)
            vw_ref[0:ATT_WINDOW, sl] = vp_ref[pl.ds(h, ATT_WINDOW, stride=ATT_HEADS), :].astype(BF16)
    else:
        kw_ref[0:ATT_WINDOW, :] = kp_ref[0].astype(BF16)
        vw_ref[0:ATT_WINDOW, :] = vp_ref[0].astype(BF16)
    kw_ref[ATT_WINDOW:ATT_WINDOW + tq, :] = kc_ref[0].astype(BF16)
    vw_ref[ATT_WINDOW:ATT_WINDOW + tq, :] = vc_ref[0].astype(BF16)
    if rows > ATT_WINDOW + tq:
        pad = jnp.zeros((rows - ATT_WINDOW - tq, kw_ref.shape[1]), BF16)
        kw_ref[ATT_WINDOW + tq:rows, :] = pad
        vw_ref[ATT_WINDOW + tq:rows, :] = pad

    first = pl.program_id(1) == 0
    col = lax.broadcasted_iota(jnp.int32, (group, ATT_KEYS), 1)
    for g in range(tq // group):
        r0 = g * group
        if first_prev_invalid:
            dead = jnp.logical_and(first, col + r0 < ATT_WINDOW)
        for h in range(ATT_HEADS):
            sl = slice(h * ATT_HEAD_DIM, (h + 1) * ATT_HEAD_DIM)
            q = q_ref[0, r0:r0 + group, sl]
            k = kw_ref[r0:r0 + ATT_KEYS, sl]
            v = vw_ref[r0:r0 + ATT_KEYS, sl]
            s = _nt_dot(q, k) + bias_ref[h]
            if first_prev_invalid:
                s = jnp.where(dead, NEG_BIG, s)
            m = jnp.max(s, axis=-1, keepdims=True)
            p = jnp.exp(s - m)
            denom = jnp.sum(p, axis=-1, keepdims=True)
            o = jnp.dot(p.astype(BF16), v, preferred_element_type=F32) / denom
            z = az_ref[0, r0:r0 + group, sl].astype(F32)
            o_ref[0, r0:r0 + group, sl] = (o * _silu(z)).astype(o_ref.dtype)


def _attention(qn3, kprev, kn3, vprev, proj3, bias, *, tq, group, cache_layer):
    b, t, w = qn3.shape
    nt = t // tq
    prompt = cache_layer is None
    if prompt:
        assert tq == ATT_WINDOW
        kprev_spec = pl.BlockSpec((1, ATT_WINDOW, w), lambda bi, i: (bi, jnp.maximum(i - 1, 0), 0))
        vprev_spec = pl.BlockSpec((1, ATT_WINDOW, w), lambda bi, i: (bi, jnp.maximum(i - 1, 0), COL_AV))
    else:
        assert nt == 1 and kprev.shape[2:] == (ATT_WINDOW, ATT_HEADS, ATT_HEAD_DIM)
        depth = kprev.shape[0]
        kprev = kprev.reshape(depth, b, ATT_WINDOW * ATT_HEADS, ATT_HEAD_DIM)
        vprev = vprev.reshape(depth, b, ATT_WINDOW * ATT_HEADS, ATT_HEAD_DIM)
        kprev_spec = pl.BlockSpec((None, None, ATT_WINDOW * ATT_HEADS, ATT_HEAD_DIM),
                                  lambda bi, i: (cache_layer, bi, 0, 0))
        vprev_spec = kprev_spec
    rows = tq + ATT_KEYS - group
    kernel = functools.partial(_attn_kernel, tq=tq, group=group, first_prev_invalid=prompt, prev_by_head=not prompt)
    return pl.pallas_call(
        kernel,
        grid=(b, nt),
        in_specs=[pl.BlockSpec((1, tq, w), lambda bi, i: (bi, i, 0)),
                  kprev_spec,
                  pl.BlockSpec((1, tq, w), lambda bi, i: (bi, i, 0)),
                  vprev_spec,
                  pl.BlockSpec((1, tq, w), lambda bi, i: (bi, i, COL_AV)),
                  pl.BlockSpec((1, tq, w), lambda bi, i: (bi, i, COL_AZ)),
                  pl.BlockSpec((ATT_HEADS, group, ATT_KEYS), lambda bi, i: (0, 0, 0))],
        out_specs=pl.BlockSpec((1, tq, w), lambda bi, i: (bi, i, 0)),
        out_shape=jax.ShapeDtypeStruct((b, t, w), BF16),
        scratch_shapes=[pltpu.VMEM((rows, w), BF16), pltpu.VMEM((rows, w), BF16)],
        compiler_params=_params(("parallel", "arbitrary"), 48),
        name="attention",
    )(qn3, kprev, kn3, vprev, proj3, proj3, bias)


def _mlstm_kernel(q_ref, k_ref, v_ref, o_ref, z_ref, g_ref, gb_ref, hn_ref, c0_ref, n0_ref, m0_ref,
                  y_ref, c_ref, n_ref, m_ref, *, chunk):
    L = chunk

    @pl.when(pl.program_id(1) == 0)
    def _():
        c_ref[...] = c0_ref[...]
        n_ref[...] = n0_ref[...]
        m_ref[...] = m0_ref[...]

    gates = g_ref[0] + gb_ref[...]
    logf = jnp.minimum(gates, 0.0) - jnp.log1p(jnp.exp(-jnp.abs(gates)))
    row = lax.broadcasted_iota(jnp.int32, (L, L), 0)
    col = lax.broadcasted_iota(jnp.int32, (L, L), 1)
    eye = row == col
    tril = col <= row
    triu = row <= col
    k_scale = ML_HEAD_DIM ** -0.5

    for hd in range(ML_HEADS):
        sl = slice(hd * ML_HEAD_DIM, (hd + 1) * ML_HEAD_DIM)
        ig_col = gates[:, hd:hd + 1]
        lf_col = logf[:, ML_HEADS + hd:ML_HEADS + hd + 1]
        lf_row = jnp.sum(jnp.where(eye, lf_col, 0.0), axis=0, keepdims=True)
        ig_row = jnp.sum(jnp.where(eye, ig_col, 0.0), axis=0, keepdims=True)
        a_col = jnp.sum(jnp.where(tril, lf_row, 0.0), axis=1, keepdims=True)
        a_row = jnp.sum(jnp.where(triu, lf_col, 0.0), axis=0, keepdims=True)
        b = a_col[L - 1:L, :]
        m_prev = m_ref[0, hd:hd + 1, 0:1]
        c_prev = c_ref[0, hd]
        n_prev = n_ref[0, hd:hd + 1, :]

        logd = jnp.where(tril, a_col - a_row + ig_row, NEG_BIG)
        inter = a_col + m_prev
        m_row = jnp.maximum(inter, jnp.max(logd, axis=1, keepdims=True))
        dmat = jnp.exp(logd - m_row)
        w_inter = jnp.exp(inter - m_row)

        q = q_ref[0, :, sl].astype(F32)
        qb = q.astype(BF16)
        kf = k_ref[0, :, sl].astype(F32) * k_scale
        kb = kf.astype(BF16)
        vb = v_ref[0, :, sl].astype(BF16)

        s = _nt_dot(qb, kb) * dmat
        num = jnp.dot(s.astype(BF16), vb, preferred_element_type=F32)
        num = num + w_inter * _nt_dot(qb, c_prev.astype(BF16))
        den = jnp.sum(s, axis=1, keepdims=True) + w_inter * jnp.sum(q * n_prev, axis=1, keepdims=True)
        h = num / jnp.maximum(jnp.abs(den), jnp.exp(-m_row))

        g_col = b - a_col + ig_col
        m_new = jnp.maximum(b + m_prev, jnp.max(g_col, axis=0, keepdims=True))
        wk = jnp.exp(g_col - m_new)
        decay = jnp.exp(b + m_prev - m_new)
        kw = kf * wk
        c_ref[0, hd] = decay * c_prev + _tn_dot(vb, kw.astype(BF16))
        n_ref[0, hd:hd + 1, :] = decay * n_prev + jnp.sum(kw, axis=0, keepdims=True)
        m_ref[0, hd:hd + 1, :] = jnp.broadcast_to(m_new, (1, LANES))

        hm = h * _sigmoid(o_ref[0, :, sl].astype(F32))
        hm = hm * lax.rsqrt(jnp.mean(hm * hm, axis=-1, keepdims=True) + EPS) * hn_ref[:, sl]
        y_ref[0, :, sl] = (hm * _silu(z_ref[0, :, sl].astype(F32))).astype(y_ref.dtype)


def _mlstm(proj3, gates3, gate_bias, head_norm, c0, n0, m0, layer, state_layer, chunk):
    b, t, _ = proj3.shape
    w = BRANCH_WIDTH
    h, d = ML_HEADS, ML_HEAD_DIM
    col = lambda c: pl.BlockSpec((1, chunk, w), lambda bi, i, c=c: (bi, i, c))
    state_in = lambda shape: pl.BlockSpec((None, 1) + shape, lambda bi, i: (state_layer, bi) + (0,) * len(shape))
    state_out = lambda shape: pl.BlockSpec((1,) + shape, lambda bi, i: (bi,) + (0,) * len(shape))
    kernel = functools.partial(_mlstm_kernel, chunk=chunk)
    return pl.pallas_call(
        kernel,
        grid=(b, t // chunk),
        in_specs=[col(COL_MQ), col(COL_MK), col(COL_MV), col(COL_MO), col(COL_MZ),
                  pl.BlockSpec((1, chunk, LANES), lambda bi, i: (bi, i, 0)),
                  _layer_spec((1, LANES), layer),
                  _layer_spec((1, w), layer),
                  state_in((h, d, d)), state_in((h, d)), state_in((h, LANES))],
        out_specs=[pl.BlockSpec((1, chunk, w), lambda bi, i: (bi, i, 0)),
                   state_out((h, d, d)), state_out((h, d)), state_out((h, LANES))],
        out_shape=[jax.ShapeDtypeStruct((b, t, w), BF16),
                   jax.ShapeDtypeStruct((b, h, d, d), F32),
                   jax.ShapeDtypeStruct((b, h, d), F32),
                   jax.ShapeDtypeStruct((b, h, LANES), F32)],
        compiler_params=_params(("parallel", "arbitrary"), 48),
        name="mlstm",
    )(proj3, proj3, proj3, proj3, proj3, gates3, gate_bias, head_norm, c0, n0, m0)


def _pool_kernel(u_ref, halo_ref, hist_ref, z_ref, w_ref, sc_ref, y_ref, *, tp, pos0):
    i = pl.program_id(1)
    width = u_ref.shape[2]
    hist = jnp.concatenate([jnp.zeros((POOL_HALO - POOL_BUF, width), F32), hist_ref[0].astype(F32)], axis=0)
    prev = jnp.where(i == 0, hist, halo_ref[0].astype(F32))
    u = u_ref[0].astype(F32)
    pos = pos0 + i * tp + lax.broadcasted_iota(jnp.int32, (tp, 1), 0)
    for gi, win in enumerate(POOL_WINDOWS):
        sl = slice(gi * POOL_GROUP_DIM, (gi + 1) * POOL_GROUP_DIM)
        ext = jnp.concatenate([prev[:, sl], u[:, sl]], axis=0)
        acc = ext
        span = 1
        while span < win:
            acc = acc + pltpu.roll(acc, span, 0)
            span *= 2
        cnt = jnp.minimum(pos + 1, win).astype(F32)
        mean = acc[POOL_HALO:, :] / cnt
        m = (mean - u[:, sl]).astype(BF16)
        y = jnp.dot(m, w_ref[gi], preferred_element_type=F32) * sc_ref[:, sl]
        y_ref[0, :, sl] = (y * _silu(z_ref[0, :, sl].astype(F32))).astype(y_ref.dtype)


def _pool(proj3, hist, w_group, scale, layer, hist_layer, *, tp, pos0):
    b, t, _ = proj3.shape
    w = BRANCH_WIDTH
    halo_blocks = tp // POOL_HALO
    kernel = functools.partial(_pool_kernel, tp=tp, pos0=pos0)
    return pl.pallas_call(
        kernel,
        grid=(b, t // tp),
        in_specs=[pl.BlockSpec((1, tp, w), lambda bi, i: (bi, i, COL_PU)),
                  pl.BlockSpec((1, POOL_HALO, w), lambda bi, i: (bi, jnp.maximum(i * halo_blocks - 1, 0), COL_PU)),
                  pl.BlockSpec((None, 1, POOL_BUF, w), lambda bi, i: (hist_layer, bi, 0, 0)),
                  pl.BlockSpec((1, tp, w), lambda bi, i: (bi, i, COL_PZ)),
                  _layer_spec((len(POOL_WINDOWS), POOL_GROUP_DIM, POOL_GROUP_DIM), layer),
                  _layer_spec((1, w), layer)],
        out_specs=pl.BlockSpec((1, tp, w), lambda bi, i: (bi, i, 0)),
        out_shape=jax.ShapeDtypeStruct((b, t, w), BF16),
        compiler_params=_params(("parallel", "arbitrary"), 32),
        name="pool",
    )(proj3, proj3, hist, proj3, w_group, scale)


def _merge_kernel(*refs):
    y_refs = refs[:N_BRANCH]
    gate_refs = refs[N_BRANCH:-4]
    x_ref, wb_ref, wo_ref, o_ref = refs[-4:]
    per_branch = len(gate_refs) // N_BRANCH
    merged = None
    for b in range(N_BRANCH):
        gate = jnp.concatenate([_sigmoid(r[...].astype(F32)) for r in gate_refs[b * per_branch:(b + 1) * per_branch]],
                               axis=1)
        term = gate * jnp.dot(y_refs[b][...], wb_ref[b], preferred_element_type=F32)
        merged = term if merged is None else merged + term
    o_ref[...] = x_ref[...] + jnp.dot(merged.astype(BF16), wo_ref[...], preferred_element_type=F32)


def _merge(y_pool, y_ml, y_att, proj, x2, w_branch, w_out, layer, tm):
    n, d = x2.shape
    w = BRANCH_WIDTH
    row = lambda width: pl.BlockSpec((tm, width), lambda i: (i, 0))
    n_gate_blocks = N_BRANCH * (d // w)
    gate_specs = [pl.BlockSpec((tm, w), lambda i, k=k: (i, COL_GATES + k)) for k in range(n_gate_blocks)]
    return pl.pallas_call(
        _merge_kernel,
        grid=(n // tm,),
        in_specs=[row(w)] * N_BRANCH + gate_specs + [row(d), _layer_spec((N_BRANCH, w, d), layer, True),
                                                    _layer_spec((d, d), layer, True)],
        out_specs=row(d),
        out_shape=jax.ShapeDtypeStruct((n, d), F32),
        compiler_params=_params(("parallel",), 56),
        name="merge",
    )(y_pool, y_ml, y_att, *([proj] * n_gate_blocks), x2, w_branch, w_out)


def _ple_kernel(x_ref, p_ref, g_ref, wg_ref, wp_ref, o_ref):
    x = x_ref[...]
    h = (x * lax.rsqrt(jnp.mean(x * x, axis=-1, keepdims=True) + EPS) * g_ref[...]).astype(BF16)
    pg = _sigmoid(jnp.dot(h, wg_ref[...], preferred_element_type=F32))
    pp = jnp.dot(p_ref[...].astype(BF16), wp_ref[...], preferred_element_type=F32)
    o_ref[...] = x + pg * pp


def _ple(x2, p_all, gain, w_gate, w_proj, layer, tm):
    n, d = x2.shape
    dp = p_all.shape[2]
    return pl.pallas_call(
        _ple_kernel,
        grid=(n // tm,),
        in_specs=[pl.BlockSpec((tm, d), lambda i: (i, 0)),
                  pl.BlockSpec((None, tm, dp), lambda i: (layer, i, 0)),
                  _layer_spec((1, d), layer, True), _layer_spec((d, d), layer, True),
                  _layer_spec((dp, d), layer, True)],
        out_specs=pl.BlockSpec((tm, d), lambda i: (i, 0)),
        out_shape=jax.ShapeDtypeStruct((n, d), F32),
        compiler_params=_params(("parallel",), 48),
        name="ple",
    )(x2, p_all, gain, w_gate, w_proj)


def _tile(n, preferred):
    t = min(n, preferred)
    assert n % t == 0, (n, t)
    return t


def _prepare_weights(norm_mix, w_in, w_pool_group, pool_scale, b_ig, b_fg, ml_head_norm, att_q_norm, att_k_norm,
                     att_rel_bias, w_branch, w_out, ple_norm, w_ple_gate, w_ple_proj):
    depth, d, width = w_in.shape
    assert width == W_IN_GATES_END + (COL_GATES - COL_AQ) * BRANCH_WIDTH + N_BRANCH * d
    pad = LANES - W_IN_GATE_COLS
    w_gate = jnp.pad(w_in[:, :, W_IN_GATES_START:W_IN_GATES_END], ((0, 0), (0, 0), (0, pad))).astype(BF16)
    gate_bias = jnp.pad(jnp.concatenate([b_ig, b_fg], axis=1), ((0, 0), (0, pad)))[:, None, :].astype(F32)
    gq = att_q_norm * (ATT_HEAD_DIM ** -0.5)
    return dict(
        norm_g=norm_mix[:, None, :], w_main=_repack_w_in(w_in), w_gate=w_gate, gate_bias=gate_bias,
        w_pool=w_pool_group.astype(BF16), pool_scale=pool_scale[:, None, :], head_norm=ml_head_norm[:, None, :],
        gq=gq[:, None, :], gk=att_k_norm[:, None, :], rel_bias=att_rel_bias,
        w_branch=w_branch.astype(BF16), w_out=w_out.astype(BF16), ple_norm=ple_norm[:, None, :],
        w_ple_gate=w_ple_gate.astype(BF16), w_ple_proj=w_ple_proj.astype(BF16))


def _layer(x3, p_all, hist, c0, n0, m0, kv_cache, state_layer, pos0, layer, lw):
    b, t, d = x3.shape
    n = b * t
    w = BRANCH_WIDTH
    prompt = kv_cache is None
    assert t % CHUNK == 0 and t >= POOL_BUF

    proj, gates = _inproj(x3.reshape(n, d), lw["norm_g"], lw["w_main"], lw["w_gate"], layer, _tile(n, 1024))
    qn, kn = _qknorm(proj, lw["gq"], lw["gk"], layer, _tile(n, 512))
    proj3 = proj.reshape(b, t, -1)
    qn3, kn3 = qn.reshape(b, t, w), kn.reshape(b, t, w)

    if prompt:
        tq, group = _tile(t, ATT_WINDOW), 2 * CHUNK
        kprev, vprev, cache_layer = kn3, proj3, None
    else:
        tq, group = t, CHUNK
        assert t == CHUNK
        kprev, vprev = kv_cache
        cache_layer = state_layer
    bias = _attn_bias(lw["rel_bias"][layer], group)
    y_att = _attention(qn3, kprev, kn3, vprev, proj3, bias, tq=tq, group=group, cache_layer=cache_layer)

    m0b = jnp.broadcast_to(m0[..., None], m0.shape + (LANES,))
    y_ml, c1, n1, m1 = _mlstm(proj3, gates.reshape(b, t, LANES), lw["gate_bias"], lw["head_norm"],
                              c0, n0, m0b, layer, state_layer, _tile(t, 256))

    y_pool = _pool(proj3, hist, lw["w_pool"], lw["pool_scale"], layer, state_layer, tp=_tile(t, 512), pos0=pos0)

    x1 = _merge(y_pool.reshape(n, w), y_ml.reshape(n, w), y_att.reshape(n, w), proj, x3.reshape(n, d),
                lw["w_branch"], lw["w_out"], layer, _tile(n, 256))
    x2 = _ple(x1, p_all.reshape(p_all.shape[0], n, -1), lw["ple_norm"], lw["w_ple_gate"], lw["w_ple_proj"],
              layer, _tile(n, 512))

    keep = min(ATT_WINDOW, t) if prompt else t
    new_pool = proj3[:, t - POOL_BUF:, COL_PU * w:(COL_PU + 1) * w].astype(F32)
    new_k = kn3[:, t - keep:].reshape(b, keep, ATT_HEADS, ATT_HEAD_DIM)
    new_v = proj3[:, t - keep:, COL_AV * w:(COL_AV + 1) * w].astype(F32).reshape(b, keep, ATT_HEADS, ATT_HEAD_DIM)
    return x2.reshape(b, t, d), (new_pool, c1, n1, m1[:, :, 0], new_k, new_v)


def kernel(x_prompt, x_sample, cache_att_k, cache_att_v, state_pool, state_mlstm_c, state_mlstm_n, state_mlstm_m, p_prompt, p_sample, norm_mix, w_in, w_pool_group, pool_scale, b_ig, b_fg, ml_head_norm, att_q_norm, att_k_norm, att_rel_bias, w_branch, w_out, ple_norm, w_ple_gate, w_ple_proj):
    xp, xs = x_prompt, x_sample
    bp = x_prompt.shape[0]
    depth = w_in.shape[0]
    lw = _prepare_weights(norm_mix, w_in, w_pool_group, pool_scale, b_ig, b_fg, ml_head_norm, att_q_norm,
                          att_k_norm, att_rel_bias, w_branch, w_out, ple_norm, w_ple_gate, w_ple_proj)
    hist0 = jnp.zeros((1, bp, POOL_BUF, BRANCH_WIDTH), F32)
    c0 = jnp.zeros((1, bp, ML_HEADS, ML_HEAD_DIM, ML_HEAD_DIM), F32)
    n0 = jnp.zeros((1, bp, ML_HEADS, ML_HEAD_DIM), F32)
    m0 = jnp.zeros((1, bp, ML_HEADS), F32)
    sp = [[] for _ in range(6)]
    ss = [[] for _ in range(6)]
    for i in range(depth):
        xp, st_p = _layer(xp, p_prompt, hist0, c0, n0, m0, None, 0, 0, i, lw)
        xs, st_s = _layer(xs, p_sample, state_pool, state_mlstm_c, state_mlstm_n, state_mlstm_m,
                          (cache_att_k, cache_att_v), i, PAST_LEN, i, lw)
        for j in range(6):
            sp[j].append(st_p[j])
            ss[j].append(st_s[j])
    pool_p, c_p, n_p, m_p, k_p, v_p = [jnp.stack(a) for a in sp]
    pool_s, c_s, n_s, m_s, k_s, v_s = [jnp.stack(a) for a in ss]
    return (xp, xs, pool_p, pool_s, c_p, c_s, n_p, n_s, m_p, m_s, k_p, k_s, v_p, v_s)
```

```python
import functools

import numpy as np
import jax
import jax.numpy as jnp
from jax import lax
from jax.experimental import pallas as pl
from jax.experimental.pallas import tpu as pltpu

F32 = jnp.float32
BF16 = jnp.bfloat16

EPS = 1e-6
CHUNK = 64
PAST_LEN = 1024

POOL_WINDOWS = (2, 4, 8, 16)
POOL_GROUP_DIM = 256
POOL_BUF = 15
POOL_HALO = 16

ML_HEADS = 4
ML_HEAD_DIM = 256

ATT_HEADS = 8
ATT_HEAD_DIM = 128
ATT_WINDOW = 512
REL_CLIP = 256
ATT_KEYS = 640

BRANCH_WIDTH = 1024
LANES = 128

N_BRANCH = 3
COL_PU, COL_PZ, COL_MQ, COL_MK, COL_MV, COL_MO, COL_MZ, COL_AQ, COL_AK, COL_AV, COL_AZ = range(11)
COL_GATES = 11
W_IN_GATES_START = 7168
W_IN_GATES_END = 7176
W_IN_GATE_COLS = W_IN_GATES_END - W_IN_GATES_START

PROJ_DTYPE = jnp.float32
NEG_BIG = -1e30
MIB = 1024 * 1024


def _params(semantics, vmem_mib):
    return pltpu.CompilerParams(dimension_semantics=semantics, vmem_limit_bytes=vmem_mib * MIB)


def _layer_spec(tail, layer, single_buffer=False):
    index_map = lambda *_: (layer,) + (0,) * len(tail)
    if single_buffer:
        return pl.BlockSpec((None,) + tuple(tail), index_map, pipeline_mode=pl.Buffered(1))
    return pl.BlockSpec((None,) + tuple(tail), index_map)


def _sigmoid(x):
    return 1.0 / (1.0 + jnp.exp(-x))


def _silu(x):
    return x * _sigmoid(x)


def _nt_dot(a, b):
    return lax.dot_general(a, b, (((1,), (1,)), ((), ())), preferred_element_type=F32)


def _tn_dot(a, b):
    return lax.dot_general(a, b, (((0,), (0,)), ((), ())), preferred_element_type=F32)


def _repack_kernel(a_ref, b_ref, o_ref, *, blocks_before):
    j = pl.program_id(1)

    @pl.when(j < blocks_before)
    def _():
        o_ref[...] = a_ref[...].astype(BF16)

    @pl.when(j >= blocks_before)
    def _():
        shifted = jnp.concatenate([a_ref[W_IN_GATE_COLS:, :], b_ref[...]], axis=0)
        o_ref[...] = shifted.astype(BF16)


def _repack_w_in_t(w_in_t):
    depth, width, d = w_in_t.shape
    tn = BRANCH_WIDTH
    out_width = width - W_IN_GATE_COLS
    assert out_width % tn == 0 and W_IN_GATES_START % tn == 0
    kernel = functools.partial(_repack_kernel, blocks_before=W_IN_GATES_START // tn)
    return pl.pallas_call(
        kernel,
        grid=(depth, out_width // tn),
        in_specs=[pl.BlockSpec((None, tn, d), lambda l, j: (l, j, 0)),
                  pl.BlockSpec((None, W_IN_GATE_COLS, d), lambda l, j: (l, (j + 1) * (tn // W_IN_GATE_COLS), 0))],
        out_specs=pl.BlockSpec((None, tn, d), lambda l, j: (l, j, 0)),
        out_shape=jax.ShapeDtypeStruct((depth, out_width, d), BF16),
        compiler_params=_params(("parallel", "arbitrary"), 40),
        name="repack_w_in",
    )(w_in_t, w_in_t)


def _inproj_kernel(x_ref, g_ref, w_ref, wg_ref, o_ref, og_ref, h_ref):
    @pl.when(pl.program_id(1) == 0)
    def _():
        x = x_ref[...]
        ms = jnp.mean(x * x, axis=-1, keepdims=True)
        h = (x * lax.rsqrt(ms + EPS) * g_ref[...]).astype(BF16)
        h_ref[...] = h
        og_ref[...] = _nt_dot(h, wg_ref[...])

    o_ref[...] = _nt_dot(h_ref[...], w_ref[...]).astype(o_ref.dtype)


def _inproj(x2, gain, w_main_t, w_gate_t, layer, tm):
    n, d = x2.shape
    nw = w_main_t.shape[1]
    tn = BRANCH_WIDTH
    return pl.pallas_call(
        _inproj_kernel,
        grid=(n // tm, nw // tn),
        in_specs=[pl.BlockSpec((tm, d), lambda i, j: (i, 0)),
                  _layer_spec((1, d), layer),
                  pl.BlockSpec((None, tn, d), lambda i, j: (layer, j, 0)),
                  _layer_spec((LANES, d), layer)],
        out_specs=[pl.BlockSpec((tm, tn), lambda i, j: (i, j)),
                   pl.BlockSpec((tm, LANES), lambda i, j: (i, 0))],
        out_shape=[jax.ShapeDtypeStruct((n, nw), PROJ_DTYPE),
                   jax.ShapeDtypeStruct((n, LANES), F32)],
        scratch_shapes=[pltpu.VMEM((tm, d), BF16)],
        compiler_params=_params(("parallel", "arbitrary"), 48),
        name="inproj",
    )(x2, gain, w_main_t, w_gate_t)


def _qknorm_kernel(q_ref, k_ref, gq_ref, gk_ref, qo_ref, ko_ref):
    for h in range(ATT_HEADS):
        sl = slice(h * ATT_HEAD_DIM, (h + 1) * ATT_HEAD_DIM)
        q = q_ref[:, sl].astype(F32)
        qn = q * lax.rsqrt(jnp.mean(q * q, axis=-1, keepdims=True) + EPS) * gq_ref[...]
        qo_ref[:, sl] = qn.astype(qo_ref.dtype)
        k = k_ref[:, sl].astype(F32)
        kn = k * lax.rsqrt(jnp.mean(k * k, axis=-1, keepdims=True) + EPS) * gk_ref[...]
        ko_ref[:, sl] = kn.astype(ko_ref.dtype)


def _qknorm(proj, gq, gk, layer, tm):
    n = proj.shape[0]
    w = BRANCH_WIDTH
    return pl.pallas_call(
        _qknorm_kernel,
        grid=(n // tm,),
        in_specs=[pl.BlockSpec((tm, w), lambda i: (i, COL_AQ)),
                  pl.BlockSpec((tm, w), lambda i: (i, COL_AK)),
                  _layer_spec((1, ATT_HEAD_DIM), layer),
                  _layer_spec((1, ATT_HEAD_DIM), layer)],
        out_specs=[pl.BlockSpec((tm, w), lambda i: (i, 0)),
                   pl.BlockSpec((tm, w), lambda i: (i, 0))],
        out_shape=[jax.ShapeDtypeStruct((n, w), BF16),
                   jax.ShapeDtypeStruct((n, w), F32)],
        compiler_params=_params(("parallel",), 32),
        name="qknorm",
    )(proj, proj, gq, gk)


def _attn_bias(rel_bias, group):
    period = ATT_KEYS + group
    m = np.arange(period)
    m = np.where(m <= ATT_KEYS, m, m - period)
    idx = np.clip(ATT_WINDOW - m, -REL_CLIP, REL_CLIP) + REL_CLIP
    vec = rel_bias[:, idx].astype(F32)
    heads = rel_bias.shape[0]
    bias = jnp.tile(vec, (1, group))[:, :group * (period - 1)].reshape(heads, group, period - 1)[:, :, :ATT_KEYS]
    i = np.arange(group)[:, None]
    j = np.arange(ATT_KEYS)[None, :]
    lo = (i // CHUNK) * CHUNK
    band = (j >= lo) & (j < lo + ATT_WINDOW + CHUNK)
    return jnp.where(jnp.asarray(band)[None], bias, NEG_BIG)


def _attn_kernel(q_ref, kp_ref, kc_ref, vp_ref, vc_ref, az_ref, bias_ref, o_ref, kw_ref, vw_ref,
                 *, tq, group, first_prev_invalid, prev_by_head):
    rows = kw_ref.shape[0]
    if prev_by_head:
        for h in range(ATT_HEADS):
            sl = slice(h * ATT_HEAD_DIM, (h + 1) * ATT_HEAD_DIM)
            kw_ref[0:ATT_WINDOW, sl] = kp_ref[pl.ds(h, ATT_WINDOW, stride=ATT_HEADS), :].astype(BF16)
            vw_ref[0:ATT_WINDOW, sl] = vp_ref[pl.ds(h, ATT_WINDOW, stride=ATT_HEADS), :].astype(BF16)
    else:
        kw_ref[0:ATT_WINDOW, :] = kp_ref[0].astype(BF16)
        vw_ref[0:ATT_WINDOW, :] = vp_ref[0].astype(BF16)
    kw_ref[ATT_WINDOW:ATT_WINDOW + tq, :] = kc_ref[0].astype(BF16)
    vw_ref[ATT_WINDOW:ATT_WINDOW + tq, :] = vc_ref[0].astype(BF16)
    if rows > ATT_WINDOW + tq:
        pad = jnp.zeros((rows - ATT_WINDOW - tq, kw_ref.shape[1]), BF16)
        kw_ref[ATT_WINDOW + tq:rows, :] = pad
        vw_ref[ATT_WINDOW + tq:rows, :] = pad

    first = pl.program_id(1) == 0
    col = lax.broadcasted_iota(jnp.int32, (group, ATT_KEYS), 1)
    for g in range(tq // group):
        r0 = g * group
        if first_prev_invalid:
            dead = jnp.logical_and(first, col + r0 < ATT_WINDOW)
        for h in range(ATT_HEADS):
            sl = slice(h * ATT_HEAD_DIM, (h + 1) * ATT_HEAD_DIM)
            q = q_ref[0, r0:r0 + group, sl]
            k = kw_ref[r0:r0 + ATT_KEYS, sl]
            v = vw_ref[r0:r0 + ATT_KEYS, sl]
            s = _nt_dot(q, k) + bias_ref[h]
            if first_prev_invalid:
                s = jnp.where(dead, NEG_BIG, s)
            m = jnp.max(s, axis=-1, keepdims=True)
            p = jnp.exp(s - m)
            denom = jnp.sum(p, axis=-1, keepdims=True)
            o = jnp.dot(p.astype(BF16), v, preferred_element_type=F32) / denom
            z = az_ref[0, r0:r0 + group, sl].astype(F32)
            o_ref[0, r0:r0 + group, sl] = (o * _silu(z)).astype(o_ref.dtype)


def _attention(qn3, kprev, kn3, vprev, proj3, bias, *, tq, group, cache_layer):
    b, t, w = qn3.shape
    nt = t // tq
    prompt = cache_layer is None
    if prompt:
        assert tq == ATT_WINDOW
        kprev_spec = pl.BlockSpec((1, ATT_WINDOW, w), lambda bi, i: (bi, jnp.maximum(i - 1, 0), 0))
        vprev_spec = pl.BlockSpec((1, ATT_WINDOW, w), lambda bi, i: (bi, jnp.maximum(i - 1, 0), COL_AV))
    else:
        assert nt == 1 and kprev.shape[2:] == (ATT_WINDOW, ATT_HEADS, ATT_HEAD_DIM)
        depth = kprev.shape[0]
        kprev = kprev.reshape(depth, b, ATT_WINDOW * ATT_HEADS, ATT_HEAD_DIM)
        vprev = vprev.reshape(depth, b, ATT_WINDOW * ATT_HEADS, ATT_HEAD_DIM)
        kprev_spec = pl.BlockSpec((None, None, ATT_WINDOW * ATT_HEADS, ATT_HEAD_DIM),
                                  lambda bi, i: (cache_layer, bi, 0, 0))
        vprev_spec = kprev_spec
    rows = tq + ATT_KEYS - group
    kernel = functools.partial(_attn_kernel, tq=tq, group=group, first_prev_invalid=prompt, prev_by_head=not prompt)
    return pl.pallas_call(
        kernel,
        grid=(b, nt),
        in_specs=[pl.BlockSpec((1, tq, w), lambda bi, i: (bi, i, 0)),
                  kprev_spec,
                  pl.BlockSpec((1, tq, w), lambda bi, i: (bi, i, 0)),
                  vprev_spec,
                  pl.BlockSpec((1, tq, w), lambda bi, i: (bi, i, COL_AV)),
                  pl.BlockSpec((1, tq, w), lambda bi, i: (bi, i, COL_AZ)),
                  pl.BlockSpec((ATT_HEADS, group, ATT_KEYS), lambda bi, i: (0, 0, 0))],
        out_specs=pl.BlockSpec((1, tq, w), lambda bi, i: (bi, i, 0)),
        out_shape=jax.ShapeDtypeStruct((b, t, w), BF16),
        scratch_shapes=[pltpu.VMEM((rows, w), BF16), pltpu.VMEM((rows, w), BF16)],
        compiler_params=_params(("parallel", "arbitrary"), 48),
        name="attention",
    )(qn3, kprev, kn3, vprev, proj3, proj3, bias)


def _mlstm_kernel(q_ref, k_ref, v_ref, o_ref, z_ref, g_ref, gb_ref, hn_ref, c0_ref, n0_ref, m0_ref,
                  y_ref, c_ref, n_ref, m_ref, *, chunk):
    L = chunk

    @pl.when(pl.program_id(1) == 0)
    def _():
        c_ref[...] = c0_ref[...]
        n_ref[...] = n0_ref[...]
        m_ref[...] = m0_ref[...]

    gates = g_ref[0] + gb_ref[...]
    logf = jnp.minimum(gates, 0.0) - jnp.log1p(jnp.exp(-jnp.abs(gates)))
    row = lax.broadcasted_iota(jnp.int32, (L, L), 0)
    col = lax.broadcasted_iota(jnp.int32, (L, L), 1)
    eye = row == col
    tril = col <= row
    triu = row <= col
    k_scale = ML_HEAD_DIM ** -0.5

    for hd in range(ML_HEADS):
        sl = slice(hd * ML_HEAD_DIM, (hd + 1) * ML_HEAD_DIM)
        ig_col = gates[:, hd:hd + 1]
        lf_col = logf[:, ML_HEADS + hd:ML_HEADS + hd + 1]
        lf_row = jnp.sum(jnp.where(eye, lf_col, 0.0), axis=0, keepdims=True)
        ig_row = jnp.sum(jnp.where(eye, ig_col, 0.0), axis=0, keepdims=True)
        a_col = jnp.sum(jnp.where(tril, lf_row, 0.0), axis=1, keepdims=True)
        a_row = jnp.sum(jnp.where(triu, lf_col, 0.0), axis=0, keepdims=True)
        b = a_col[L - 1:L, :]
        m_prev = m_ref[0, hd:hd + 1, 0:1]
        c_prev = c_ref[0, hd]
        n_prev = n_ref[0, hd:hd + 1, :]

        logd = jnp.where(tril, a_col - a_row + ig_row, NEG_BIG)
        inter = a_col + m_prev
        m_row = jnp.maximum(inter, jnp.max(logd, axis=1, keepdims=True))
        dmat = jnp.exp(logd - m_row)
        w_inter = jnp.exp(inter - m_row)

        q = q_ref[0, :, sl].astype(F32)
        qb = q.astype(BF16)
        kf = k_ref[0, :, sl].astype(F32) * k_scale
        kb = kf.astype(BF16)
        vb = v_ref[0, :, sl].astype(BF16)

        s = _nt_dot(qb, kb) * dmat
        num = jnp.dot(s.astype(BF16), vb, preferred_element_type=F32)
        num = num + w_inter * _nt_dot(qb, c_prev.astype(BF16))
        den = jnp.sum(s, axis=1, keepdims=True) + w_inter * jnp.sum(q * n_prev, axis=1, keepdims=True)
        h = num / jnp.maximum(jnp.abs(den), jnp.exp(-m_row))

        g_col = b - a_col + ig_col
        m_new = jnp.maximum(b + m_prev, jnp.max(g_col, axis=0, keepdims=True))
        wk = jnp.exp(g_col - m_new)
        decay = jnp.exp(b + m_prev - m_new)
        kw = kf * wk
        c_ref[0, hd] = decay * c_prev + _tn_dot(vb, kw.astype(BF16))
        n_ref[0, hd:hd + 1, :] = decay * n_prev + jnp.sum(kw, axis=0, keepdims=True)
        m_ref[0, hd:hd + 1, :] = jnp.broadcast_to(m_new, (1, LANES))

        hm = h * _sigmoid(o_ref[0, :, sl].astype(F32))
        hm = hm * lax.rsqrt(jnp.mean(hm * hm, axis=-1, keepdims=True) + EPS) * hn_ref[:, sl]
        y_ref[0, :, sl] = (hm * _silu(z_ref[0, :, sl].astype(F32))).astype(y_ref.dtype)


def _mlstm(proj3, gates3, gate_bias, head_norm, c0, n0, m0, layer, state_layer, chunk):
    b, t, _ = proj3.shape
    w = BRANCH_WIDTH
    h, d = ML_HEADS, ML_HEAD_DIM
    col = lambda c: pl.BlockSpec((1, chunk, w), lambda bi, i, c=c: (bi, i, c))
    state_in = lambda shape: pl.BlockSpec((None, 1) + shape, lambda bi, i: (state_layer, bi) + (0,) * len(shape))
    state_out = lambda shape: pl.BlockSpec((1,) + shape, lambda bi, i: (bi,) + (0,) * len(shape))
    kernel = functools.partial(_mlstm_kernel, chunk=chunk)
    return pl.pallas_call(
        kernel,
        grid=(b, t // chunk),
        in_specs=[col(COL_MQ), col(COL_MK), col(COL_MV), col(COL_MO), col(COL_MZ),
                  pl.BlockSpec((1, chunk, LANES), lambda bi, i: (bi, i, 0)),
                  _layer_spec((1, LANES), layer),
                  _layer_spec((1, w), layer),
                  state_in((h, d, d)), state_in((h, d)), state_in((h, LANES))],
        out_specs=[pl.BlockSpec((1, chunk, w), lambda bi, i: (bi, i, 0)),
                   state_out((h, d, d)), state_out((h, d)), state_out((h, LANES))],
        out_shape=[jax.ShapeDtypeStruct((b, t, w), BF16),
                   jax.ShapeDtypeStruct((b, h, d, d), F32),
                   jax.ShapeDtypeStruct((b, h, d), F32),
                   jax.ShapeDtypeStruct((b, h, LANES), F32)],
        compiler_params=_params(("parallel", "arbitrary"), 48),
        name="mlstm",
    )(proj3, proj3, proj3, proj3, proj3, gates3, gate_bias, head_norm, c0, n0, m0)


def _pool_kernel(u_ref, halo_ref, hist_ref, z_ref, w_ref, sc_ref, y_ref, *, tp, pos0):
    i = pl.program_id(1)
    width = u_ref.shape[2]
    hist = jnp.concatenate([jnp.zeros((POOL_HALO - POOL_BUF, width), F32), hist_ref[0].astype(F32)], axis=0)
    prev = jnp.where(i == 0, hist, halo_ref[0].astype(F32))
    u = u_ref[0].astype(F32)
    pos = pos0 + i * tp + lax.broadcasted_iota(jnp.int32, (tp, 1), 0)
    for gi, win in enumerate(POOL_WINDOWS):
        sl = slice(gi * POOL_GROUP_DIM, (gi + 1) * POOL_GROUP_DIM)
        ext = jnp.concatenate([prev[:, sl], u[:, sl]], axis=0)
        acc = ext
        span = 1
        while span < win:
            acc = acc + pltpu.roll(acc, span, 0)
            span *= 2
        cnt = jnp.minimum(pos + 1, win).astype(F32)
        mean = acc[POOL_HALO:, :] / cnt
        m = (mean - u[:, sl]).astype(BF16)
        y = jnp.dot(m, w_ref[gi], preferred_element_type=F32) * sc_ref[:, sl]
        y_ref[0, :, sl] = (y * _silu(z_ref[0, :, sl].astype(F32))).astype(y_ref.dtype)


def _pool(proj3, hist, w_group, scale, layer, hist_layer, *, tp, pos0):
    b, t, _ = proj3.shape
    w = BRANCH_WIDTH
    halo_blocks = tp // POOL_HALO
    kernel = functools.partial(_pool_kernel, tp=tp, pos0=pos0)
    return pl.pallas_call(
        kernel,
        grid=(b, t // tp),
        in_specs=[pl.BlockSpec((1, tp, w), lambda bi, i: (bi, i, COL_PU)),
                  pl.BlockSpec((1, POOL_HALO, w), lambda bi, i: (bi, jnp.maximum(i * halo_blocks - 1, 0), COL_PU)),
                  pl.BlockSpec((None, 1, POOL_BUF, w), lambda bi, i: (hist_layer, bi, 0, 0)),
                  pl.BlockSpec((1, tp, w), lambda bi, i: (bi, i, COL_PZ)),
                  _layer_spec((len(POOL_WINDOWS), POOL_GROUP_DIM, POOL_GROUP_DIM), layer),
                  _layer_spec((1, w), layer)],
        out_specs=pl.BlockSpec((1, tp, w), lambda bi, i: (bi, i, 0)),
        out_shape=jax.ShapeDtypeStruct((b, t, w), BF16),
        compiler_params=_params(("parallel", "arbitrary"), 32),
        name="pool",
    )(proj3, proj3, hist, proj3, w_group, scale)


def _merge_kernel(*refs):
    y_refs = refs[:N_BRANCH]
    gate_refs = refs[N_BRANCH:-4]
    x_ref, wb_ref, wo_ref, o_ref = refs[-4:]
    per_branch = len(gate_refs) // N_BRANCH
    merged = None
    for b in range(N_BRANCH):
        gate = jnp.concatenate([_sigmoid(r[...].astype(F32)) for r in gate_refs[b * per_branch:(b + 1) * per_branch]],
                               axis=1)
        term = gate * jnp.dot(y_refs[b][...], wb_ref[b], preferred_element_type=F32)
        merged = term if merged is None else merged + term
    o_ref[...] = x_ref[...] + jnp.dot(merged.astype(BF16), wo_ref[...], preferred_element_type=F32)


def _merge(y_pool, y_ml, y_att, proj, x2, w_branch, w_out, layer, tm):
    n, d = x2.shape
    w = BRANCH_WIDTH
    row = lambda width: pl.BlockSpec((tm, width), lambda i: (i, 0))
    n_gate_blocks = N_BRANCH * (d // w)
    gate_specs = [pl.BlockSpec((tm, w), lambda i, k=k: (i, COL_GATES + k)) for k in range(n_gate_blocks)]
    return pl.pallas_call(
        _merge_kernel,
        grid=(n // tm,),
        in_specs=[row(w)] * N_BRANCH + gate_specs + [row(d), _layer_spec((N_BRANCH, w, d), layer, True),
                                                    _layer_spec((d, d), layer, True)],
        out_specs=row(d),
        out_shape=jax.ShapeDtypeStruct((n, d), F32),
        compiler_params=_params(("parallel",), 56),
        name="merge",
    )(y_pool, y_ml, y_att, *([proj] * n_gate_blocks), x2, w_branch, w_out)


def _ple_kernel(x_ref, p_ref, g_ref, wg_ref, wp_ref, o_ref):
    x = x_ref[...]
    h = (x * lax.rsqrt(jnp.mean(x * x, axis=-1, keepdims=True) + EPS) * g_ref[...]).astype(BF16)
    pg = _sigmoid(jnp.dot(h, wg_ref[...], preferred_element_type=F32))
    pp = jnp.dot(p_ref[...].astype(BF16), wp_ref[...], preferred_element_type=F32)
    o_ref[...] = x + pg * pp


def _ple(x2, p_all, gain, w_gate, w_proj, layer, tm):
    n, d = x2.shape
    dp = p_all.shape[2]
    return pl.pallas_call(
        _ple_kernel,
        grid=(n // tm,),
        in_specs=[pl.BlockSpec((tm, d), lambda i: (i, 0)),
                  pl.BlockSpec((None, tm, dp), lambda i: (layer, i, 0)),
                  _layer_spec((1, d), layer, True), _layer_spec((d, d), layer, True),
                  _layer_spec((dp, d), layer, True)],
        out_specs=pl.BlockSpec((tm, d), lambda i: (i, 0)),
        out_shape=jax.ShapeDtypeStruct((n, d), F32),
        compiler_params=_params(("parallel",), 48),
        name="ple",
    )(x2, p_all, gain, w_gate, w_proj)


def _tile(n, preferred):
    t = min(n, preferred)
    assert n % t == 0, (n, t)
    return t


def _prepare_weights(norm_mix, w_in, w_pool_group, pool_scale, b_ig, b_fg, ml_head_norm, att_q_norm, att_k_norm,
                     att_rel_bias, w_branch, w_out, ple_norm, w_ple_gate, w_ple_proj):
    depth, d, width = w_in.shape
    assert width == W_IN_GATES_END + (COL_GATES - COL_AQ) * BRANCH_WIDTH + N_BRANCH * d
    pad = LANES - W_IN_GATE_COLS
    w_in_t = jnp.swapaxes(w_in, 1, 2)
    w_gate = jnp.pad(w_in_t[:, W_IN_GATES_START:W_IN_GATES_END, :], ((0, 0), (0, pad), (0, 0))).astype(BF16)
    gate_bias = jnp.pad(jnp.concatenate([b_ig, b_fg], axis=1), ((0, 0), (0, pad)))[:, None, :].astype(F32)
    gq = att_q_norm * (ATT_HEAD_DIM ** -0.5)
    return dict(
        norm_g=norm_mix[:, None, :], w_main=_repack_w_in_t(w_in_t), w_gate=w_gate, gate_bias=gate_bias,
        w_pool=w_pool_group.astype(BF16), pool_scale=pool_scale[:, None, :], head_norm=ml_head_norm[:, None, :],
        gq=gq[:, None, :], gk=att_k_norm[:, None, :], rel_bias=att_rel_bias,
        w_branch=w_branch.astype(BF16), w_out=w_out.astype(BF16), ple_norm=ple_norm[:, None, :],
        w_ple_gate=w_ple_gate.astype(BF16), w_ple_proj=w_ple_proj.astype(BF16))


def _layer(x3, p_all, hist, c0, n0, m0, kv_cache, state_layer, pos0, layer, lw):
    b, t, d = x3.shape
    n = b * t
    w = BRANCH_WIDTH
    prompt = kv_cache is None
    assert t % CHUNK == 0 and t >= POOL_BUF

    proj, gates = _inproj(x3.reshape(n, d), lw["norm_g"], lw["w_main"], lw["w_gate"], layer, _tile(n, 1024))
    qn, kn = _qknorm(proj, lw["gq"], lw["gk"], layer, _tile(n, 512))
    proj3 = proj.reshape(b, t, -1)
    qn3, kn3 = qn.reshape(b, t, w), kn.reshape(b, t, w)

    if prompt:
        tq, group = _tile(t, ATT_WINDOW), 2 * CHUNK
        kprev, vprev, cache_layer = kn3, proj3, None
    else:
        tq, group = t, CHUNK
        assert t == CHUNK
        kprev, vprev = kv_cache
        cache_layer = state_layer
    bias = _attn_bias(lw["rel_bias"][layer], group)
    y_att = _attention(qn3, kprev, kn3, vprev, proj3, bias, tq=tq, group=group, cache_layer=cache_layer)

    m0b = jnp.broadcast_to(m0[..., None], m0.shape + (LANES,))
    y_ml, c1, n1, m1 = _mlstm(proj3, gates.reshape(b, t, LANES), lw["gate_bias"], lw["head_norm"],
                              c0, n0, m0b, layer, state_layer, _tile(t, 256))

    y_pool = _pool(proj3, hist, lw["w_pool"], lw["pool_scale"], layer, state_layer, tp=_tile(t, 512), pos0=pos0)

    x1 = _merge(y_pool.reshape(n, w), y_ml.reshape(n, w), y_att.reshape(n, w), proj, x3.reshape(n, d),
                lw["w_branch"], lw["w_out"], layer, _tile(n, 256))
    x2 = _ple(x1, p_all.reshape(p_all.shape[0], n, -1), lw["ple_norm"], lw["w_ple_gate"], lw["w_ple_proj"],
              layer, _tile(n, 512))

    keep = min(ATT_WINDOW, t) if prompt else t
    new_pool = proj3[:, t - POOL_BUF:, COL_PU * w:(COL_PU + 1) * w].astype(F32)
    new_k = kn3[:, t - keep:].reshape(b, keep, ATT_HEADS, ATT_HEAD_DIM)
    new_v = proj3[:, t - keep:, COL_AV * w:(COL_AV + 1) * w].astype(F32).reshape(b, keep, ATT_HEADS, ATT_HEAD_DIM)
    return x2.reshape(b, t, d), (new_pool, c1, n1, m1[:, :, 0], new_k, new_v)


def kernel(x_prompt, x_sample, cache_att_k, cache_att_v, state_pool, state_mlstm_c, state_mlstm_n, state_mlstm_m, p_prompt, p_sample, norm_mix, w_in, w_pool_group, pool_scale, b_ig, b_fg, ml_head_norm, att_q_norm, att_k_norm, att_rel_bias, w_branch, w_out, ple_norm, w_ple_gate, w_ple_proj):
    xp, xs = x_prompt, x_sample
    bp = x_prompt.shape[0]
    depth = w_in.shape[0]
    lw = _prepare_weights(norm_mix, w_in, w_pool_group, pool_scale, b_ig, b_fg, ml_head_norm, att_q_norm,
                          att_k_norm, att_rel_bias, w_branch, w_out, ple_norm, w_ple_gate, w_ple_proj)
    hist0 = jnp.zeros((1, bp, POOL_BUF, BRANCH_WIDTH), F32)
    c0 = jnp.zeros((1, bp, ML_HEADS, ML_HEAD_DIM, ML_HEAD_DIM), F32)
    n0 = jnp.zeros((1, bp, ML_HEADS, ML_HEAD_DIM), F32)
    m0 = jnp.zeros((1, bp, ML_HEADS), F32)
    sp = [[] for _ in range(6)]
    ss = [[] for _ in range(6)]
    for i in range(depth):
        xp, st_p = _layer(xp, p_prompt, hist0, c0, n0, m0, None, 0, 0, i, lw)
        xs, st_s = _layer(xs, p_sample, state_pool, state_mlstm_c, state_mlstm_n, state_mlstm_m,
                          (cache_att_k, cache_att_v), i, PAST_LEN, i, lw)
        for j in range(6):
            sp[j].append(st_p[j])
            ss[j].append(st_s[j])
    pool_p, c_p, n_p, m_p, k_p, v_p = [jnp.stack(a) for a in sp]
    pool_s, c_s, n_s, m_s, k_s, v_s = [jnp.stack(a) for a in ss]
    return (xp, xs, pool_p, pool_s, c_p, c_s, n_p, n_s, m_p, m_s, k_p, k_s, v_p, v_s)
```

```python
import functools

import numpy as np
import jax
import jax.numpy as jnp
from jax import lax
from jax.experimental import pallas as pl
from jax.experimental.pallas import tpu as pltpu

F32 = jnp.float32
BF16 = jnp.bfloat16

EPS = 1e-6
CHUNK = 64
PAST_LEN = 1024

POOL_WINDOWS = (2, 4, 8, 16)
POOL_GROUP_DIM = 256
POOL_BUF = 15
POOL_HALO = 16

ML_HEADS = 4
ML_HEAD_DIM = 256

ATT_HEADS = 8
ATT_HEAD_DIM = 128
ATT_WINDOW = 512
REL_CLIP = 256
ATT_KEYS = 640

BRANCH_WIDTH = 1024
LANES = 128

N_BRANCH = 3
COL_PU, COL_PZ, COL_MQ, COL_MK, COL_MV, COL_MO, COL_MZ, COL_AQ, COL_AK, COL_AV, COL_AZ = range(11)
COL_GATES = 11
W_IN_GATES_START = 7168
W_IN_GATES_END = 7176
W_IN_GATE_COLS = W_IN_GATES_END - W_IN_GATES_START

PROJ_DTYPE = jnp.float32
NEG_BIG = -1e30
LOG2_E = 1.4426950408889634
MIB = 1024 * 1024


def _params(semantics, vmem_mib):
    return pltpu.CompilerParams(dimension_semantics=semantics, vmem_limit_bytes=vmem_mib * MIB)


def _layer_spec(tail, layer, single_buffer=False):
    index_map = lambda *_: (layer,) + (0,) * len(tail)
    if single_buffer:
        return pl.BlockSpec((None,) + tuple(tail), index_map, pipeline_mode=pl.Buffered(1))
    return pl.BlockSpec((None,) + tuple(tail), index_map)


def _sigmoid(x):
    return 1.0 / (1.0 + jnp.exp(-x))


def _silu(x):
    return x * _sigmoid(x)


def _nt_dot(a, b):
    return lax.dot_general(a, b, (((1,), (1,)), ((), ())), preferred_element_type=F32)


def _tn_dot(a, b):
    return lax.dot_general(a, b, (((0,), (0,)), ((), ())), preferred_element_type=F32)


def _repack_kernel(a_ref, b_ref, o_ref, *, blocks_before):
    j = pl.program_id(1)

    @pl.when(j < blocks_before)
    def _():
        o_ref[...] = a_ref[...].astype(BF16)

    @pl.when(j >= blocks_before)
    def _():
        shifted = jnp.concatenate([a_ref[W_IN_GATE_COLS:, :], b_ref[...]], axis=0)
        o_ref[...] = shifted.astype(BF16)


def _repack_w_in_t(w_in_t):
    depth, width, d = w_in_t.shape
    tn = BRANCH_WIDTH
    out_width = width - W_IN_GATE_COLS
    assert out_width % tn == 0 and W_IN_GATES_START % tn == 0
    kernel = functools.partial(_repack_kernel, blocks_before=W_IN_GATES_START // tn)
    return pl.pallas_call(
        kernel,
        grid=(depth, out_width // tn),
        in_specs=[pl.BlockSpec((None, tn, d), lambda l, j: (l, j, 0)),
                  pl.BlockSpec((None, W_IN_GATE_COLS, d), lambda l, j: (l, (j + 1) * (tn // W_IN_GATE_COLS), 0))],
        out_specs=pl.BlockSpec((None, tn, d), lambda l, j: (l, j, 0)),
        out_shape=jax.ShapeDtypeStruct((depth, out_width, d), BF16),
        compiler_params=_params(("parallel", "arbitrary"), 40),
        name="repack_w_in",
    )(w_in_t, w_in_t)


def _inproj_kernel(x_ref, g_ref, w_ref, wg_ref, gq_ref, gk_ref, o_ref, og_ref, h_ref):
    j = pl.program_id(1)

    @pl.when(j == 0)
    def _():
        x = x_ref[...]
        ms = jnp.mean(x * x, axis=-1, keepdims=True)
        h = (x * lax.rsqrt(ms + EPS) * g_ref[...]).astype(BF16)
        h_ref[...] = h
        og_ref[...] = _nt_dot(h, wg_ref[...])

    is_q = j == COL_AQ
    head_normed = jnp.logical_or(is_q, j == COL_AK)

    @pl.when(jnp.logical_not(head_normed))
    def _():
        o_ref[...] = _nt_dot(h_ref[...], w_ref[...]).astype(o_ref.dtype)

    @pl.when(head_normed)
    def _():
        acc = _nt_dot(h_ref[...], w_ref[...])
        gain = jnp.where(is_q, gq_ref[...], gk_ref[...])
        for h in range(ATT_HEADS):
            sl = slice(h * ATT_HEAD_DIM, (h + 1) * ATT_HEAD_DIM)
            a = acc[:, sl]
            r = lax.rsqrt(jnp.mean(a * a, axis=-1, keepdims=True) + EPS)
            o_ref[:, sl] = (a * r * gain).astype(o_ref.dtype)


def _inproj(x2, gain, w_main_t, w_gate_t, gq, gk, layer, tm):
    n, d = x2.shape
    nw = w_main_t.shape[1]
    tn = BRANCH_WIDTH
    assert tn == ATT_HEADS * ATT_HEAD_DIM
    return pl.pallas_call(
        _inproj_kernel,
        grid=(n // tm, nw // tn),
        in_specs=[pl.BlockSpec((tm, d), lambda i, j: (i, 0)),
                  _layer_spec((1, d), layer),
                  pl.BlockSpec((None, tn, d), lambda i, j: (layer, j, 0)),
                  _layer_spec((LANES, d), layer),
                  _layer_spec((1, ATT_HEAD_DIM), layer),
                  _layer_spec((1, ATT_HEAD_DIM), layer)],
        out_specs=[pl.BlockSpec((tm, tn), lambda i, j: (i, j)),
                   pl.BlockSpec((tm, LANES), lambda i, j: (i, 0))],
        out_shape=[jax.ShapeDtypeStruct((n, nw), PROJ_DTYPE),
                   jax.ShapeDtypeStruct((n, LANES), F32)],
        scratch_shapes=[pltpu.VMEM((tm, d), BF16)],
        compiler_params=_params(("parallel", "arbitrary"), 48),
        name="inproj",
    )(x2, gain, w_main_t, w_gate_t, gq, gk)


def _attn_bias(rel_bias, group):
    period = ATT_KEYS + group
    m = np.arange(period)
    m = np.where(m <= ATT_KEYS, m, m - period)
    idx = np.clip(ATT_WINDOW - m, -REL_CLIP, REL_CLIP) + REL_CLIP
    vec = rel_bias[:, idx].astype(F32)
    heads = rel_bias.shape[0]
    bias = jnp.tile(vec, (1, group))[:, :group * (period - 1)].reshape(heads, group, period - 1)[:, :, :ATT_KEYS]
    i = np.arange(group)[:, None]
    j = np.arange(ATT_KEYS)[None, :]
    lo = (i // CHUNK) * CHUNK
    band = (j >= lo) & (j < lo + ATT_WINDOW + CHUNK)
    return jnp.where(jnp.asarray(band)[None], bias * LOG2_E, NEG_BIG)


def _attn_kernel(q_ref, kp_ref, kc_ref, vp_ref, vc_ref, az_ref, bias_ref, o_ref, kw_ref, vw_ref, s_ref, m_ref,
                 *, tq, group, first_prev_invalid, prev_by_head):
    rows = kw_ref.shape[0]
    if prev_by_head:
        for h in range(ATT_HEADS):
            sl = slice(h * ATT_HEAD_DIM, (h + 1) * ATT_HEAD_DIM)
            kw_ref[0:ATT_WINDOW, sl] = kp_ref[pl.ds(h, ATT_WINDOW, stride=ATT_HEADS), :].astype(BF16)
            vw_ref[0:ATT_WINDOW, sl] = vp_ref[pl.ds(h, ATT_WINDOW, stride=ATT_HEADS), :].astype(BF16)
    else:
        kw_ref[0:ATT_WINDOW, :] = kp_ref[0].astype(BF16)
        vw_ref[0:ATT_WINDOW, :] = vp_ref[0].astype(BF16)
    kw_ref[ATT_WINDOW:ATT_WINDOW + tq, :] = kc_ref[0].astype(BF16)
    vw_ref[ATT_WINDOW:ATT_WINDOW + tq, :] = vc_ref[0].astype(BF16)
    if rows > ATT_WINDOW + tq:
        pad = jnp.zeros((rows - ATT_WINDOW - tq, kw_ref.shape[1]), BF16)
        kw_ref[ATT_WINDOW + tq:rows, :] = pad
        vw_ref[ATT_WINDOW + tq:rows, :] = pad

    def attend(mask_prev):
        col = lax.broadcasted_iota(jnp.int32, (group, ATT_KEYS), 1)
        ones = jnp.ones((ATT_KEYS, ATT_HEAD_DIM), BF16)
        for g in range(tq // group):
            r0 = g * group
            for h in range(ATT_HEADS):
                sl = slice(h * ATT_HEAD_DIM, (h + 1) * ATT_HEAD_DIM)
                q = q_ref[0, r0:r0 + group, sl].astype(BF16)
                k = kw_ref[r0:r0 + ATT_KEYS, sl]
                s = _nt_dot(q, k) + bias_ref[h]
                if mask_prev:
                    s = jnp.where(col + r0 < ATT_WINDOW, NEG_BIG, s)
                s_ref[h] = s
                m_ref[h] = jnp.max(s, axis=-1, keepdims=True)
            for h in range(ATT_HEADS):
                sl = slice(h * ATT_HEAD_DIM, (h + 1) * ATT_HEAD_DIM)
                v1 = jnp.concatenate([vw_ref[r0:r0 + ATT_KEYS, sl], ones], axis=1)
                p = jnp.exp2(s_ref[h] - m_ref[h])
                o = jnp.dot(p.astype(BF16), v1, preferred_element_type=F32)
                o = o[:, :ATT_HEAD_DIM] / o[:, ATT_HEAD_DIM:]
                z = az_ref[0, r0:r0 + group, sl].astype(F32)
                o_ref[0, r0:r0 + group, sl] = (o * _silu(z)).astype(o_ref.dtype)

    if first_prev_invalid:
        first = pl.program_id(1) == 0
        pl.when(first)(functools.partial(attend, True))
        pl.when(jnp.logical_not(first))(functools.partial(attend, False))
    else:
        attend(False)


def _attention(proj3, kprev, vprev, bias, *, tq, group, cache_layer):
    b, t, _ = proj3.shape
    w = BRANCH_WIDTH
    nt = t // tq
    prompt = cache_layer is None
    if prompt:
        assert tq == ATT_WINDOW
        kprev = vprev = proj3
        kprev_spec = pl.BlockSpec((1, ATT_WINDOW, w), lambda bi, i: (bi, jnp.maximum(i - 1, 0), COL_AK))
        vprev_spec = pl.BlockSpec((1, ATT_WINDOW, w), lambda bi, i: (bi, jnp.maximum(i - 1, 0), COL_AV))
    else:
        assert nt == 1 and kprev.shape[2:] == (ATT_WINDOW, ATT_HEADS, ATT_HEAD_DIM)
        depth = kprev.shape[0]
        kprev = kprev.reshape(depth, b, ATT_WINDOW * ATT_HEADS, ATT_HEAD_DIM)
        vprev = vprev.reshape(depth, b, ATT_WINDOW * ATT_HEADS, ATT_HEAD_DIM)
        kprev_spec = pl.BlockSpec((None, None, ATT_WINDOW * ATT_HEADS, ATT_HEAD_DIM),
                                  lambda bi, i: (cache_layer, bi, 0, 0))
        vprev_spec = kprev_spec
    rows = tq + ATT_KEYS - group
    kernel = functools.partial(_attn_kernel, tq=tq, group=group, first_prev_invalid=prompt, prev_by_head=not prompt)
    return pl.pallas_call(
        kernel,
        grid=(b, nt),
        in_specs=[pl.BlockSpec((1, tq, w), lambda bi, i: (bi, i, COL_AQ)),
                  kprev_spec,
                  pl.BlockSpec((1, tq, w), lambda bi, i: (bi, i, COL_AK)),
                  vprev_spec,
                  pl.BlockSpec((1, tq, w), lambda bi, i: (bi, i, COL_AV)),
                  pl.BlockSpec((1, tq, w), lambda bi, i: (bi, i, COL_AZ)),
                  pl.BlockSpec((ATT_HEADS, group, ATT_KEYS), lambda bi, i: (0, 0, 0))],
        out_specs=pl.BlockSpec((1, tq, w), lambda bi, i: (bi, i, 0)),
        out_shape=jax.ShapeDtypeStruct((b, t, w), BF16),
        scratch_shapes=[pltpu.VMEM((rows, w), BF16), pltpu.VMEM((rows, w), BF16),
                        pltpu.VMEM((ATT_HEADS, group, ATT_KEYS), F32), pltpu.VMEM((ATT_HEADS, group, 1), F32)],
        compiler_params=_params(("parallel", "arbitrary"), 48),
        name="attention",
    )(proj3, kprev, proj3, vprev, proj3, proj3, bias)


def _mlstm_kernel(q_ref, k_ref, v_ref, o_ref, z_ref, g_ref, gb_ref, hn_ref, c0_ref, n0_ref, m0_ref,
                  y_ref, c_ref, n_ref, m_ref, *, chunk):
    L = chunk

    @pl.when(pl.program_id(1) == 0)
    def _():
        c_ref[...] = c0_ref[...]
        n_ref[...] = n0_ref[...]
        m_ref[...] = m0_ref[...]

    gates = g_ref[0] + gb_ref[...]
    logf = jnp.minimum(gates, 0.0) - jnp.log1p(jnp.exp(-jnp.abs(gates)))
    row = lax.broadcasted_iota(jnp.int32, (L, L), 0)
    col = lax.broadcasted_iota(jnp.int32, (L, L), 1)
    eye = row == col
    tril = col <= row
    triu = row <= col
    k_scale = ML_HEAD_DIM ** -0.5

    for hd in range(ML_HEADS):
        sl = slice(hd * ML_HEAD_DIM, (hd + 1) * ML_HEAD_DIM)
        ig_col = gates[:, hd:hd + 1]
        lf_col = logf[:, ML_HEADS + hd:ML_HEADS + hd + 1]
        lf_row = jnp.sum(jnp.where(eye, lf_col, 0.0), axis=0, keepdims=True)
        ig_row = jnp.sum(jnp.where(eye, ig_col, 0.0), axis=0, keepdims=True)
        a_col = jnp.sum(jnp.where(tril, lf_row, 0.0), axis=1, keepdims=True)
        a_row = jnp.sum(jnp.where(triu, lf_col, 0.0), axis=0, keepdims=True)
        b = a_col[L - 1:L, :]
        m_prev = m_ref[0, hd:hd + 1, 0:1]
        c_prev = c_ref[0, hd]
        n_prev = n_ref[0, hd:hd + 1, :]

        logd = jnp.where(tril, a_col - a_row + ig_row, NEG_BIG)
        inter = a_col + m_prev
        m_row = jnp.maximum(inter, jnp.max(logd, axis=1, keepdims=True))
        dmat = jnp.exp(logd - m_row)
        w_inter = jnp.exp(inter - m_row)

        q = q_ref[0, :, sl].astype(F32)
        qb = q.astype(BF16)
        kf = k_ref[0, :, sl].astype(F32) * k_scale
        kb = kf.astype(BF16)
        vb = v_ref[0, :, sl].astype(BF16)

        s = _nt_dot(qb, kb) * dmat
        num = jnp.dot(s.astype(BF16), vb, preferred_element_type=F32)
        num = num + w_inter * _nt_dot(qb, c_prev.astype(BF16))
        den = jnp.sum(s, axis=1, keepdims=True) + w_inter * jnp.sum(q * n_prev, axis=1, keepdims=True)
        h = num / jnp.maximum(jnp.abs(den), jnp.exp(-m_row))

        g_col = b - a_col + ig_col
        m_new = jnp.maximum(b + m_prev, jnp.max(g_col, axis=0, keepdims=True))
        wk = jnp.exp(g_col - m_new)
        decay = jnp.exp(b + m_prev - m_new)
        kw = kf * wk
        c_ref[0, hd] = decay * c_prev + _tn_dot(vb, kw.astype(BF16))
        n_ref[0, hd:hd + 1, :] = decay * n_prev + jnp.sum(kw, axis=0, keepdims=True)
        m_ref[0, hd:hd + 1, :] = jnp.broadcast_to(m_new, (1, LANES))

        hm = h * _sigmoid(o_ref[0, :, sl].astype(F32))
        hm = hm * lax.rsqrt(jnp.mean(hm * hm, axis=-1, keepdims=True) + EPS) * hn_ref[:, sl]
        y_ref[0, :, sl] = (hm * _silu(z_ref[0, :, sl].astype(F32))).astype(y_ref.dtype)


def _mlstm(proj3, gates3, gate_bias, head_norm, c0, n0, m0, layer, state_layer, chunk):
    b, t, _ = proj3.shape
    w = BRANCH_WIDTH
    h, d = ML_HEADS, ML_HEAD_DIM
    col = lambda c: pl.BlockSpec((1, chunk, w), lambda bi, i, c=c: (bi, i, c))
    state_in = lambda shape: pl.BlockSpec((None, 1) + shape, lambda bi, i: (state_layer, bi) + (0,) * len(shape))
    state_out = lambda shape: pl.BlockSpec((1,) + shape, lambda bi, i: (bi,) + (0,) * len(shape))
    kernel = functools.partial(_mlstm_kernel, chunk=chunk)
    return pl.pallas_call(
        kernel,
        grid=(b, t // chunk),
        in_specs=[col(COL_MQ), col(COL_MK), col(COL_MV), col(COL_MO), col(COL_MZ),
                  pl.BlockSpec((1, chunk, LANES), lambda bi, i: (bi, i, 0)),
                  _layer_spec((1, LANES), layer),
                  _layer_spec((1, w), layer),
                  state_in((h, d, d)), state_in((h, d)), state_in((h, LANES))],
        out_specs=[pl.BlockSpec((1, chunk, w), lambda bi, i: (bi, i, 0)),
                   state_out((h, d, d)), state_out((h, d)), state_out((h, LANES))],
        out_shape=[jax.ShapeDtypeStruct((b, t, w), BF16),
                   jax.ShapeDtypeStruct((b, h, d, d), F32),
                   jax.ShapeDtypeStruct((b, h, d), F32),
                   jax.ShapeDtypeStruct((b, h, LANES), F32)],
        compiler_params=_params(("parallel", "arbitrary"), 48),
        name="mlstm",
    )(proj3, proj3, proj3, proj3, proj3, gates3, gate_bias, head_norm, c0, n0, m0)


def _pool_kernel(u_ref, halo_ref, hist_ref, z_ref, w_ref, sc_ref, y_ref, *, tp, pos0):
    i = pl.program_id(1)
    width = u_ref.shape[2]
    hist = jnp.concatenate([jnp.zeros((POOL_HALO - POOL_BUF, width), F32), hist_ref[0].astype(F32)], axis=0)
    prev = jnp.where(i == 0, hist, halo_ref[0].astype(F32))
    u = u_ref[0].astype(F32)
    pos = pos0 + i * tp + lax.broadcasted_iota(jnp.int32, (tp, 1), 0)
    for gi, win in enumerate(POOL_WINDOWS):
        sl = slice(gi * POOL_GROUP_DIM, (gi + 1) * POOL_GROUP_DIM)
        ext = jnp.concatenate([prev[:, sl], u[:, sl]], axis=0)
        acc = ext
        span = 1
        while span < win:
            acc = acc + pltpu.roll(acc, span, 0)
            span *= 2
        cnt = jnp.minimum(pos + 1, win).astype(F32)
        mean = acc[POOL_HALO:, :] / cnt
        m = (mean - u[:, sl]).astype(BF16)
        y = jnp.dot(m, w_ref[gi], preferred_element_type=F32) * sc_ref[:, sl]
        y_ref[0, :, sl] = (y * _silu(z_ref[0, :, sl].astype(F32))).astype(y_ref.dtype)


def _pool(proj3, hist, w_group, scale, layer, hist_layer, *, tp, pos0):
    b, t, _ = proj3.shape
    w = BRANCH_WIDTH
    halo_blocks = tp // POOL_HALO
    kernel = functools.partial(_pool_kernel, tp=tp, pos0=pos0)
    return pl.pallas_call(
        kernel,
        grid=(b, t // tp),
        in_specs=[pl.BlockSpec((1, tp, w), lambda bi, i: (bi, i, COL_PU)),
                  pl.BlockSpec((1, POOL_HALO, w), lambda bi, i: (bi, jnp.maximum(i * halo_blocks - 1, 0), COL_PU)),
                  pl.BlockSpec((None, 1, POOL_BUF, w), lambda bi, i: (hist_layer, bi, 0, 0)),
                  pl.BlockSpec((1, tp, w), lambda bi, i: (bi, i, COL_PZ)),
                  _layer_spec((len(POOL_WINDOWS), POOL_GROUP_DIM, POOL_GROUP_DIM), layer),
                  _layer_spec((1, w), layer)],
        out_specs=pl.BlockSpec((1, tp, w), lambda bi, i: (bi, i, 0)),
        out_shape=jax.ShapeDtypeStruct((b, t, w), BF16),
        compiler_params=_params(("parallel", "arbitrary"), 32),
        name="pool",
    )(proj3, proj3, hist, proj3, w_group, scale)


def _merge_kernel(*refs):
    y_refs = refs[:N_BRANCH]
    gate_refs = refs[N_BRANCH:-4]
    x_ref, wb_ref, wo_ref, o_ref = refs[-4:]
    per_branch = len(gate_refs) // N_BRANCH
    merged = None
    for b in range(N_BRANCH):
        gate = jnp.concatenate([_sigmoid(r[...].astype(F32)) for r in gate_refs[b * per_branch:(b + 1) * per_branch]],
                               axis=1)
        term = gate * jnp.dot(y_refs[b][...], wb_ref[b], preferred_element_type=F32)
        merged = term if merged is None else merged + term
    o_ref[...] = x_ref[...] + jnp.dot(merged.astype(BF16), wo_ref[...], preferred_element_type=F32)


def _merge(y_pool, y_ml, y_att, proj, x2, w_branch, w_out, layer, tm):
    n, d = x2.shape
    w = BRANCH_WIDTH
    row = lambda width: pl.BlockSpec((tm, width), lambda i: (i, 0))
    n_gate_blocks = N_BRANCH * (d // w)
    gate_specs = [pl.BlockSpec((tm, w), lambda i, k=k: (i, COL_GATES + k)) for k in range(n_gate_blocks)]
    return pl.pallas_call(
        _merge_kernel,
        grid=(n // tm,),
        in_specs=[row(w)] * N_BRANCH + gate_specs + [row(d), _layer_spec((N_BRANCH, w, d), layer, True),
                                                    _layer_spec((d, d), layer, True)],
        out_specs=row(d),
        out_shape=jax.ShapeDtypeStruct((n, d), F32),
        compiler_params=_params(("parallel",), 56),
        name="merge",
    )(y_pool, y_ml, y_att, *([proj] * n_gate_blocks), x2, w_branch, w_out)


def _ple_kernel(x_ref, p_ref, g_ref, wg_ref, wp_ref, o_ref):
    x = x_ref[...]
    h = (x * lax.rsqrt(jnp.mean(x * x, axis=-1, keepdims=True) + EPS) * g_ref[...]).astype(BF16)
    pg = _sigmoid(jnp.dot(h, wg_ref[...], preferred_element_type=F32))
    pp = jnp.dot(p_ref[...].astype(BF16), wp_ref[...], preferred_element_type=F32)
    o_ref[...] = x + pg * pp


def _ple(x2, p_all, gain, w_gate, w_proj, layer, tm):
    n, d = x2.shape
    dp = p_all.shape[2]
    return pl.pallas_call(
        _ple_kernel,
        grid=(n // tm,),
        in_specs=[pl.BlockSpec((tm, d), lambda i: (i, 0)),
                  pl.BlockSpec((None, tm, dp), lambda i: (layer, i, 0)),
                  _layer_spec((1, d), layer, True), _layer_spec((d, d), layer, True),
                  _layer_spec((dp, d), layer, True)],
        out_specs=pl.BlockSpec((tm, d), lambda i: (i, 0)),
        out_shape=jax.ShapeDtypeStruct((n, d), F32),
        compiler_params=_params(("parallel",), 48),
        name="ple",
    )(x2, p_all, gain, w_gate, w_proj)


def _tile(n, preferred):
    t = min(n, preferred)
    assert n % t == 0, (n, t)
    return t


def _prepare_weights(norm_mix, w_in, w_pool_group, pool_scale, b_ig, b_fg, ml_head_norm, att_q_norm, att_k_norm,
                     att_rel_bias, w_branch, w_out, ple_norm, w_ple_gate, w_ple_proj):
    depth, d, width = w_in.shape
    assert width == W_IN_GATES_END + (COL_GATES - COL_AQ) * BRANCH_WIDTH + N_BRANCH * d
    pad = LANES - W_IN_GATE_COLS
    w_in_t = jnp.swapaxes(w_in, 1, 2)
    w_gate = jnp.pad(w_in_t[:, W_IN_GATES_START:W_IN_GATES_END, :], ((0, 0), (0, pad), (0, 0))).astype(BF16)
    gate_bias = jnp.pad(jnp.concatenate([b_ig, b_fg], axis=1), ((0, 0), (0, pad)))[:, None, :].astype(F32)
    gq = att_q_norm * (ATT_HEAD_DIM ** -0.5 * LOG2_E)
    return dict(
        norm_g=norm_mix[:, None, :], w_main=_repack_w_in_t(w_in_t), w_gate=w_gate, gate_bias=gate_bias,
        w_pool=w_pool_group.astype(BF16), pool_scale=pool_scale[:, None, :], head_norm=ml_head_norm[:, None, :],
        gq=gq[:, None, :], gk=att_k_norm[:, None, :], rel_bias=att_rel_bias,
        w_branch=w_branch.astype(BF16), w_out=w_out.astype(BF16), ple_norm=ple_norm[:, None, :],
        w_ple_gate=w_ple_gate.astype(BF16), w_ple_proj=w_ple_proj.astype(BF16))


def _layer(x3, p_all, hist, c0, n0, m0, kv_cache, state_layer, pos0, layer, lw):
    b, t, d = x3.shape
    n = b * t
    w = BRANCH_WIDTH
    prompt = kv_cache is None
    assert t % CHUNK == 0 and t >= POOL_BUF

    proj, gates = _inproj(x3.reshape(n, d), lw["norm_g"], lw["w_main"], lw["w_gate"], lw["gq"], lw["gk"],
                          layer, _tile(n, 1024))
    proj3 = proj.reshape(b, t, -1)

    if prompt:
        tq, group = _tile(t, ATT_WINDOW), 2 * CHUNK
        kprev, vprev, cache_layer = None, None, None
    else:
        tq, group = t, CHUNK
        assert t == CHUNK
        kprev, vprev = kv_cache
        cache_layer = state_layer
    bias = _attn_bias(lw["rel_bias"][layer], group)
    y_att = _attention(proj3, kprev, vprev, bias, tq=tq, group=group, cache_layer=cache_layer)

    m0b = jnp.broadcast_to(m0[..., None], m0.shape + (LANES,))
    y_ml, c1, n1, m1 = _mlstm(proj3, gates.reshape(b, t, LANES), lw["gate_bias"], lw["head_norm"],
                              c0, n0, m0b, layer, state_layer, _tile(t, 256))

    y_pool = _pool(proj3, hist, lw["w_pool"], lw["pool_scale"], layer, state_layer, tp=_tile(t, 512), pos0=pos0)

    x1 = _merge(y_pool.reshape(n, w), y_ml.reshape(n, w), y_att.reshape(n, w), proj, x3.reshape(n, d),
                lw["w_branch"], lw["w_out"], layer, _tile(n, 256))
    x2 = _ple(x1, p_all.reshape(p_all.shape[0], n, -1), lw["ple_norm"], lw["w_ple_gate"], lw["w_ple_proj"],
              layer, _tile(n, 512))

    keep = min(ATT_WINDOW, t) if prompt else t
    new_pool = proj3[:, t - POOL_BUF:, COL_PU * w:(COL_PU + 1) * w].astype(F32)
    new_k = proj3[:, t - keep:, COL_AK * w:(COL_AK + 1) * w].astype(F32).reshape(b, keep, ATT_HEADS, ATT_HEAD_DIM)
    new_v = proj3[:, t - keep:, COL_AV * w:(COL_AV + 1) * w].astype(F32).reshape(b, keep, ATT_HEADS, ATT_HEAD_DIM)
    return x2.reshape(b, t, d), (new_pool, c1, n1, m1[:, :, 0], new_k, new_v)


def kernel(x_prompt, x_sample, cache_att_k, cache_att_v, state_pool, state_mlstm_c, state_mlstm_n, state_mlstm_m, p_prompt, p_sample, norm_mix, w_in, w_pool_group, pool_scale, b_ig, b_fg, ml_head_norm, att_q_norm, att_k_norm, att_rel_bias, w_branch, w_out, ple_norm, w_ple_gate, w_ple_proj):
    xp, xs = x_prompt, x_sample
    bp = x_prompt.shape[0]
    depth = w_in.shape[0]
    lw = _prepare_weights(norm_mix, w_in, w_pool_group, pool_scale, b_ig, b_fg, ml_head_norm, att_q_norm,
                          att_k_norm, att_rel_bias, w_branch, w_out, ple_norm, w_ple_gate, w_ple_proj)
    hist0 = jnp.zeros((1, bp, POOL_BUF, BRANCH_WIDTH), F32)
    c0 = jnp.zeros((1, bp, ML_HEADS, ML_HEAD_DIM, ML_HEAD_DIM), F32)
    n0 = jnp.zeros((1, bp, ML_HEADS, ML_HEAD_DIM), F32)
    m0 = jnp.zeros((1, bp, ML_HEADS), F32)
    sp = [[] for _ in range(6)]
    ss = [[] for _ in range(6)]
    for i in range(depth):
        xp, st_p = _layer(xp, p_prompt, hist0, c0, n0, m0, None, 0, 0, i, lw)
        xs, st_s = _layer(xs, p_sample, state_pool, state_mlstm_c, state_mlstm_n, state_mlstm_m,
                          (cache_att_k, cache_att_v), i, PAST_LEN, i, lw)
        for j in range(6):
            sp[j].append(st_p[j])
            ss[j].append(st_s[j])
    pool_p, c_p, n_p, m_p, k_p, v_p = [jnp.stack(a) for a in sp]
    pool_s, c_s, n_s, m_s, k_s, v_s = [jnp.stack(a) for a in ss]
    return (xp, xs, pool_p, pool_s, c_p, c_s, n_p, n_s, m_p, m_s, k_p, k_s, v_p, v_s)
```

```python
import functools

import numpy as np
import jax
import jax.numpy as jnp
from jax import lax
from jax.experimental import pallas as pl
from jax.experimental.pallas import tpu as pltpu

F32 = jnp.float32
BF16 = jnp.bfloat16

EPS = 1e-6
CHUNK = 64
PAST_LEN = 1024

POOL_WINDOWS = (2, 4, 8, 16)
POOL_GROUP_DIM = 256
POOL_BUF = 15
POOL_HALO = 16

ML_HEADS = 4
ML_HEAD_DIM = 256

ATT_HEADS = 8
ATT_HEAD_DIM = 128
ATT_WINDOW = 512
REL_CLIP = 256
ATT_KEYS = 640

BRANCH_WIDTH = 1024
LANES = 128

N_BRANCH = 3
COL_PU, COL_PZ, COL_MQ, COL_MK, COL_MV, COL_MO, COL_MZ, COL_AQ, COL_AK, COL_AV, COL_AZ = range(11)
COL_GATES = 11
W_IN_GATES_START = 7168
W_IN_GATES_END = 7176
W_IN_GATE_COLS = W_IN_GATES_END - W_IN_GATES_START

PROJ_DTYPE = jnp.bfloat16
NEG_BIG = -1e30
LOG2_E = 1.4426950408889634
MIB = 1024 * 1024


def _params(semantics, vmem_mib):
    return pltpu.CompilerParams(dimension_semantics=semantics, vmem_limit_bytes=vmem_mib * MIB)


def _layer_spec(tail, layer, single_buffer=False):
    index_map = lambda *_: (layer,) + (0,) * len(tail)
    if single_buffer:
        return pl.BlockSpec((None,) + tuple(tail), index_map, pipeline_mode=pl.Buffered(1))
    return pl.BlockSpec((None,) + tuple(tail), index_map)


def _sigmoid(x):
    return 1.0 / (1.0 + jnp.exp(-x))


def _silu(x):
    return x * _sigmoid(x)


def _nt_dot(a, b):
    return lax.dot_general(a, b, (((1,), (1,)), ((), ())), preferred_element_type=F32)


def _tn_dot(a, b):
    return lax.dot_general(a, b, (((0,), (0,)), ((), ())), preferred_element_type=F32)


def _repack_kernel(a_ref, b_ref, o_ref, *, blocks_before):
    j = pl.program_id(1)

    @pl.when(j < blocks_before)
    def _():
        o_ref[...] = a_ref[...].astype(BF16)

    @pl.when(j >= blocks_before)
    def _():
        shifted = jnp.concatenate([a_ref[W_IN_GATE_COLS:, :], b_ref[...]], axis=0)
        o_ref[...] = shifted.astype(BF16)


def _repack_w_in_t(w_in_t):
    depth, width, d = w_in_t.shape
    tn = BRANCH_WIDTH
    out_width = width - W_IN_GATE_COLS
    assert out_width % tn == 0 and W_IN_GATES_START % tn == 0
    kernel = functools.partial(_repack_kernel, blocks_before=W_IN_GATES_START // tn)
    return pl.pallas_call(
        kernel,
        grid=(depth, out_width // tn),
        in_specs=[pl.BlockSpec((None, tn, d), lambda l, j: (l, j, 0)),
                  pl.BlockSpec((None, W_IN_GATE_COLS, d), lambda l, j: (l, (j + 1) * (tn // W_IN_GATE_COLS), 0))],
        out_specs=pl.BlockSpec((None, tn, d), lambda l, j: (l, j, 0)),
        out_shape=jax.ShapeDtypeStruct((depth, out_width, d), BF16),
        compiler_params=_params(("parallel", "arbitrary"), 40),
        name="repack_w_in",
    )(w_in_t, w_in_t)


def _inproj_kernel(x_ref, g_ref, w_ref, wg_ref, gq_ref, gk_ref, o_ref, og_ref, h_ref):
    j = pl.program_id(1)

    @pl.when(j == 0)
    def _():
        x = x_ref[...]
        ms = jnp.mean(x * x, axis=-1, keepdims=True)
        h = (x * lax.rsqrt(ms + EPS) * g_ref[...]).astype(BF16)
        h_ref[...] = h
        og_ref[...] = _nt_dot(h, wg_ref[...])

    is_q = j == COL_AQ
    head_normed = jnp.logical_or(is_q, j == COL_AK)

    @pl.when(jnp.logical_not(head_normed))
    def _():
        o_ref[...] = _nt_dot(h_ref[...], w_ref[...]).astype(o_ref.dtype)

    @pl.when(head_normed)
    def _():
        acc = _nt_dot(h_ref[...], w_ref[...])
        gain = jnp.where(is_q, gq_ref[...], gk_ref[...])
        for h in range(ATT_HEADS):
            sl = slice(h * ATT_HEAD_DIM, (h + 1) * ATT_HEAD_DIM)
            a = acc[:, sl]
            r = lax.rsqrt(jnp.mean(a * a, axis=-1, keepdims=True) + EPS)
            o_ref[:, sl] = (a * r * gain).astype(o_ref.dtype)


def _inproj(x2, gain, w_main_t, w_gate_t, gq, gk, layer, tm):
    n, d = x2.shape
    nw = w_main_t.shape[1]
    tn = BRANCH_WIDTH
    assert tn == ATT_HEADS * ATT_HEAD_DIM
    return pl.pallas_call(
        _inproj_kernel,
        grid=(n // tm, nw // tn),
        in_specs=[pl.BlockSpec((tm, d), lambda i, j: (i, 0)),
                  _layer_spec((1, d), layer),
                  pl.BlockSpec((None, tn, d), lambda i, j: (layer, j, 0)),
                  _layer_spec((LANES, d), layer),
                  _layer_spec((1, ATT_HEAD_DIM), layer),
                  _layer_spec((1, ATT_HEAD_DIM), layer)],
        out_specs=[pl.BlockSpec((tm, tn), lambda i, j: (i, j)),
                   pl.BlockSpec((tm, LANES), lambda i, j: (i, 0))],
        out_shape=[jax.ShapeDtypeStruct((n, nw), PROJ_DTYPE),
                   jax.ShapeDtypeStruct((n, LANES), F32)],
        scratch_shapes=[pltpu.VMEM((tm, d), BF16)],
        compiler_params=_params(("parallel", "arbitrary"), 48),
        name="inproj",
    )(x2, gain, w_main_t, w_gate_t, gq, gk)


def _attn_bias(rel_bias, group):
    period = ATT_KEYS + group
    m = np.arange(period)
    m = np.where(m <= ATT_KEYS, m, m - period)
    idx = np.clip(ATT_WINDOW - m, -REL_CLIP, REL_CLIP) + REL_CLIP
    vec = rel_bias[:, idx].astype(F32)
    heads = rel_bias.shape[0]
    bias = jnp.tile(vec, (1, group))[:, :group * (period - 1)].reshape(heads, group, period - 1)[:, :, :ATT_KEYS]
    i = np.arange(group)[:, None]
    j = np.arange(ATT_KEYS)[None, :]
    lo = (i // CHUNK) * CHUNK
    band = (j >= lo) & (j < lo + ATT_WINDOW + CHUNK)
    return jnp.where(jnp.asarray(band)[None], bias * LOG2_E, NEG_BIG)


def _attn_kernel(q_ref, kp_ref, kc_ref, vp_ref, vc_ref, az_ref, bias_ref, o_ref, kw_ref, vw_ref, s_ref, m_ref,
                 *, tq, group, first_prev_invalid, prev_by_head):
    rows = kw_ref.shape[0]
    if prev_by_head:
        for h in range(ATT_HEADS):
            sl = slice(h * ATT_HEAD_DIM, (h + 1) * ATT_HEAD_DIM)
            kw_ref[0:ATT_WINDOW, sl] = kp_ref[pl.ds(h, ATT_WINDOW, stride=ATT_HEADS), :].astype(BF16)
            vw_ref[0:ATT_WINDOW, sl] = vp_ref[pl.ds(h, ATT_WINDOW, stride=ATT_HEADS), :].astype(BF16)
    else:
        kw_ref[0:ATT_WINDOW, :] = kp_ref[0].astype(BF16)
        vw_ref[0:ATT_WINDOW, :] = vp_ref[0].astype(BF16)
    kw_ref[ATT_WINDOW:ATT_WINDOW + tq, :] = kc_ref[0].astype(BF16)
    vw_ref[ATT_WINDOW:ATT_WINDOW + tq, :] = vc_ref[0].astype(BF16)
    if rows > ATT_WINDOW + tq:
        pad = jnp.zeros((rows - ATT_WINDOW - tq, kw_ref.shape[1]), BF16)
        kw_ref[ATT_WINDOW + tq:rows, :] = pad
        vw_ref[ATT_WINDOW + tq:rows, :] = pad

    def attend(mask_prev):
        col = lax.broadcasted_iota(jnp.int32, (group, ATT_KEYS), 1)
        ones = jnp.ones((ATT_KEYS, ATT_HEAD_DIM), BF16)
        for g in range(tq // group):
            r0 = g * group
            for h in range(ATT_HEADS):
                sl = slice(h * ATT_HEAD_DIM, (h + 1) * ATT_HEAD_DIM)
                q = q_ref[0, r0:r0 + group, sl].astype(BF16)
                k = kw_ref[r0:r0 + ATT_KEYS, sl]
                s = _nt_dot(q, k) + bias_ref[h]
                if mask_prev:
                    s = jnp.where(col + r0 < ATT_WINDOW, NEG_BIG, s)
                s_ref[h] = s
                m_ref[h] = jnp.max(s, axis=-1, keepdims=True)
            for h in range(ATT_HEADS):
                sl = slice(h * ATT_HEAD_DIM, (h + 1) * ATT_HEAD_DIM)
                v1 = jnp.concatenate([vw_ref[r0:r0 + ATT_KEYS, sl], ones], axis=1)
                p = jnp.exp2(s_ref[h] - m_ref[h])
                o = jnp.dot(p.astype(BF16), v1, preferred_element_type=F32)
                o = o[:, :ATT_HEAD_DIM] / o[:, ATT_HEAD_DIM:]
                z = az_ref[0, r0:r0 + group, sl].astype(F32)
                o_ref[0, r0:r0 + group, sl] = (o * _silu(z)).astype(o_ref.dtype)

    if first_prev_invalid:
        first = pl.program_id(1) == 0
        pl.when(first)(functools.partial(attend, True))
        pl.when(jnp.logical_not(first))(functools.partial(attend, False))
    else:
        attend(False)


def _attention(proj3, kprev, vprev, bias, *, tq, group, cache_layer):
    b, t, _ = proj3.shape
    w = BRANCH_WIDTH
    nt = t // tq
    prompt = cache_layer is None
    if prompt:
        assert tq == ATT_WINDOW
        kprev = vprev = proj3
        kprev_spec = pl.BlockSpec((1, ATT_WINDOW, w), lambda bi, i: (bi, jnp.maximum(i - 1, 0), COL_AK))
        vprev_spec = pl.BlockSpec((1, ATT_WINDOW, w), lambda bi, i: (bi, jnp.maximum(i - 1, 0), COL_AV))
    else:
        assert nt == 1 and kprev.shape[2:] == (ATT_WINDOW, ATT_HEADS, ATT_HEAD_DIM)
        depth = kprev.shape[0]
        kprev = kprev.reshape(depth, b, ATT_WINDOW * ATT_HEADS, ATT_HEAD_DIM)
        vprev = vprev.reshape(depth, b, ATT_WINDOW * ATT_HEADS, ATT_HEAD_DIM)
        kprev_spec = pl.BlockSpec((None, None, ATT_WINDOW * ATT_HEADS, ATT_HEAD_DIM),
                                  lambda bi, i: (cache_layer, bi, 0, 0))
        vprev_spec = kprev_spec
    rows = tq + ATT_KEYS - group
    kernel = functools.partial(_attn_kernel, tq=tq, group=group, first_prev_invalid=prompt, prev_by_head=not prompt)
    return pl.pallas_call(
        kernel,
        grid=(b, nt),
        in_specs=[pl.BlockSpec((1, tq, w), lambda bi, i: (bi, i, COL_AQ)),
                  kprev_spec,
                  pl.BlockSpec((1, tq, w), lambda bi, i: (bi, i, COL_AK)),
                  vprev_spec,
                  pl.BlockSpec((1, tq, w), lambda bi, i: (bi, i, COL_AV)),
                  pl.BlockSpec((1, tq, w), lambda bi, i: (bi, i, COL_AZ)),
                  pl.BlockSpec((ATT_HEADS, group, ATT_KEYS), lambda bi, i: (0, 0, 0))],
        out_specs=pl.BlockSpec((1, tq, w), lambda bi, i: (bi, i, 0)),
        out_shape=jax.ShapeDtypeStruct((b, t, w), BF16),
        scratch_shapes=[pltpu.VMEM((rows, w), BF16), pltpu.VMEM((rows, w), BF16),
                        pltpu.VMEM((ATT_HEADS, group, ATT_KEYS), F32), pltpu.VMEM((ATT_HEADS, group, 1), F32)],
        compiler_params=_params(("parallel", "arbitrary"), 48),
        name="attention",
    )(proj3, kprev, proj3, vprev, proj3, proj3, bias)


def _mlstm_kernel(q_ref, k_ref, v_ref, o_ref, z_ref, g_ref, gb_ref, hn_ref, c0_ref, n0_ref, m0_ref,
                  y_ref, c_ref, n_ref, m_ref, *, chunk):
    L = chunk

    @pl.when(pl.program_id(1) == 0)
    def _():
        c_ref[...] = c0_ref[...]
        n_ref[...] = n0_ref[...]
        m_ref[...] = m0_ref[...]

    gates = g_ref[0] + gb_ref[...]
    logf = jnp.minimum(gates, 0.0) - jnp.log1p(jnp.exp(-jnp.abs(gates)))
    row = lax.broadcasted_iota(jnp.int32, (L, L), 0)
    col = lax.broadcasted_iota(jnp.int32, (L, L), 1)
    eye = row == col
    tril = col <= row
    triu = row <= col
    k_scale = ML_HEAD_DIM ** -0.5

    for hd in range(ML_HEADS):
        sl = slice(hd * ML_HEAD_DIM, (hd + 1) * ML_HEAD_DIM)
        ig_col = gates[:, hd:hd + 1]
        lf_col = logf[:, ML_HEADS + hd:ML_HEADS + hd + 1]
        lf_row = jnp.sum(jnp.where(eye, lf_col, 0.0), axis=0, keepdims=True)
        ig_row = jnp.sum(jnp.where(eye, ig_col, 0.0), axis=0, keepdims=True)
        a_col = jnp.sum(jnp.where(tril, lf_row, 0.0), axis=1, keepdims=True)
        a_row = jnp.sum(jnp.where(triu, lf_col, 0.0), axis=0, keepdims=True)
        b = a_col[L - 1:L, :]
        m_prev = m_ref[0, hd:hd + 1, 0:1]
        c_prev = c_ref[0, hd]
        n_prev = n_ref[0, hd:hd + 1, :]

        logd = jnp.where(tril, a_col - a_row + ig_row, NEG_BIG)
        inter = a_col + m_prev
        m_row = jnp.maximum(inter, jnp.max(logd, axis=1, keepdims=True))
        dmat = jnp.exp(logd - m_row)
        w_inter = jnp.exp(inter - m_row)

        q = q_ref[0, :, sl].astype(F32)
        qb = q.astype(BF16)
        kf = k_ref[0, :, sl].astype(F32) * k_scale
        kb = kf.astype(BF16)
        vb = v_ref[0, :, sl].astype(BF16)

        s = _nt_dot(qb, kb) * dmat
        num = jnp.dot(s.astype(BF16), vb, preferred_element_type=F32)
        num = num + w_inter * _nt_dot(qb, c_prev.astype(BF16))
        den = jnp.sum(s, axis=1, keepdims=True) + w_inter * jnp.sum(q * n_prev, axis=1, keepdims=True)
        h = num / jnp.maximum(jnp.abs(den), jnp.exp(-m_row))

        g_col = b - a_col + ig_col
        m_new = jnp.maximum(b + m_prev, jnp.max(g_col, axis=0, keepdims=True))
        wk = jnp.exp(g_col - m_new)
        decay = jnp.exp(b + m_prev - m_new)
        kw = kf * wk
        c_ref[0, hd] = decay * c_prev + _tn_dot(vb, kw.astype(BF16))
        n_ref[0, hd:hd + 1, :] = decay * n_prev + jnp.sum(kw, axis=0, keepdims=True)
        m_ref[0, hd:hd + 1, :] = jnp.broadcast_to(m_new, (1, LANES))

        hm = h * _sigmoid(o_ref[0, :, sl].astype(F32))
        hm = hm * lax.rsqrt(jnp.mean(hm * hm, axis=-1, keepdims=True) + EPS) * hn_ref[:, sl]
        y_ref[0, :, sl] = (hm * _silu(z_ref[0, :, sl].astype(F32))).astype(y_ref.dtype)


def _mlstm(proj3, gates3, gate_bias, head_norm, c0, n0, m0, layer, state_layer, chunk):
    b, t, _ = proj3.shape
    w = BRANCH_WIDTH
    h, d = ML_HEADS, ML_HEAD_DIM
    col = lambda c: pl.BlockSpec((1, chunk, w), lambda bi, i, c=c: (bi, i, c))
    state_in = lambda shape: pl.BlockSpec((None, 1) + shape, lambda bi, i: (state_layer, bi) + (0,) * len(shape))
    state_out = lambda shape: pl.BlockSpec((1,) + shape, lambda bi, i: (bi,) + (0,) * len(shape))
    kernel = functools.partial(_mlstm_kernel, chunk=chunk)
    return pl.pallas_call(
        kernel,
        grid=(b, t // chunk),
        in_specs=[col(COL_MQ), col(COL_MK), col(COL_MV), col(COL_MO), col(COL_MZ),
                  pl.BlockSpec((1, chunk, LANES), lambda bi, i: (bi, i, 0)),
                  _layer_spec((1, LANES), layer),
                  _layer_spec((1, w), layer),
                  state_in((h, d, d)), state_in((h, d)), state_in((h, LANES))],
        out_specs=[pl.BlockSpec((1, chunk, w), lambda bi, i: (bi, i, 0)),
                   state_out((h, d, d)), state_out((h, d)), state_out((h, LANES))],
        out_shape=[jax.ShapeDtypeStruct((b, t, w), BF16),
                   jax.ShapeDtypeStruct((b, h, d, d), F32),
                   jax.ShapeDtypeStruct((b, h, d), F32),
                   jax.ShapeDtypeStruct((b, h, LANES), F32)],
        compiler_params=_params(("parallel", "arbitrary"), 48),
        name="mlstm",
    )(proj3, proj3, proj3, proj3, proj3, gates3, gate_bias, head_norm, c0, n0, m0)


def _pool_kernel(u_ref, halo_ref, hist_ref, z_ref, w_ref, sc_ref, y_ref, *, tp, pos0):
    i = pl.program_id(1)
    width = u_ref.shape[2]
    hist = jnp.concatenate([jnp.zeros((POOL_HALO - POOL_BUF, width), F32), hist_ref[0].astype(F32)], axis=0)
    prev = jnp.where(i == 0, hist, halo_ref[0].astype(F32))
    u = u_ref[0].astype(F32)
    pos = pos0 + i * tp + lax.broadcasted_iota(jnp.int32, (tp, 1), 0)
    for gi, win in enumerate(POOL_WINDOWS):
        sl = slice(gi * POOL_GROUP_DIM, (gi + 1) * POOL_GROUP_DIM)
        ext = jnp.concatenate([prev[:, sl], u[:, sl]], axis=0)
        acc = ext
        span = 1
        while span < win:
            acc = acc + pltpu.roll(acc, span, 0)
            span *= 2
        cnt = jnp.minimum(pos + 1, win).astype(F32)
        mean = acc[POOL_HALO:, :] / cnt
        m = (mean - u[:, sl]).astype(BF16)
        y = jnp.dot(m, w_ref[gi], preferred_element_type=F32) * sc_ref[:, sl]
        y_ref[0, :, sl] = (y * _silu(z_ref[0, :, sl].astype(F32))).astype(y_ref.dtype)


def _pool(proj3, hist, w_group, scale, layer, hist_layer, *, tp, pos0):
    b, t, _ = proj3.shape
    w = BRANCH_WIDTH
    halo_blocks = tp // POOL_HALO
    kernel = functools.partial(_pool_kernel, tp=tp, pos0=pos0)
    return pl.pallas_call(
        kernel,
        grid=(b, t // tp),
        in_specs=[pl.BlockSpec((1, tp, w), lambda bi, i: (bi, i, COL_PU)),
                  pl.BlockSpec((1, POOL_HALO, w), lambda bi, i: (bi, jnp.maximum(i * halo_blocks - 1, 0), COL_PU)),
                  pl.BlockSpec((None, 1, POOL_BUF, w), lambda bi, i: (hist_layer, bi, 0, 0)),
                  pl.BlockSpec((1, tp, w), lambda bi, i: (bi, i, COL_PZ)),
                  _layer_spec((len(POOL_WINDOWS), POOL_GROUP_DIM, POOL_GROUP_DIM), layer),
                  _layer_spec((1, w), layer)],
        out_specs=pl.BlockSpec((1, tp, w), lambda bi, i: (bi, i, 0)),
        out_shape=jax.ShapeDtypeStruct((b, t, w), BF16),
        compiler_params=_params(("parallel", "arbitrary"), 32),
        name="pool",
    )(proj3, proj3, hist, proj3, w_group, scale)


def _merge_kernel(*refs):
    y_refs = refs[:N_BRANCH]
    gate_refs = refs[N_BRANCH:-4]
    x_ref, wb_ref, wo_ref, o_ref = refs[-4:]
    per_branch = len(gate_refs) // N_BRANCH
    merged = None
    for b in range(N_BRANCH):
        gate = jnp.concatenate([_sigmoid(r[...].astype(F32)) for r in gate_refs[b * per_branch:(b + 1) * per_branch]],
                               axis=1)
        term = gate * jnp.dot(y_refs[b][...], wb_ref[b], preferred_element_type=F32)
        merged = term if merged is None else merged + term
    o_ref[...] = x_ref[...] + jnp.dot(merged.astype(BF16), wo_ref[...], preferred_element_type=F32)


def _merge(y_pool, y_ml, y_att, proj, x2, w_branch, w_out, layer, tm):
    n, d = x2.shape
    w = BRANCH_WIDTH
    row = lambda width: pl.BlockSpec((tm, width), lambda i: (i, 0))
    n_gate_blocks = N_BRANCH * (d // w)
    gate_specs = [pl.BlockSpec((tm, w), lambda i, k=k: (i, COL_GATES + k)) for k in range(n_gate_blocks)]
    return pl.pallas_call(
        _merge_kernel,
        grid=(n // tm,),
        in_specs=[row(w)] * N_BRANCH + gate_specs + [row(d), _layer_spec((N_BRANCH, w, d), layer, True),
                                                    _layer_spec((d, d), layer, True)],
        out_specs=row(d),
        out_shape=jax.ShapeDtypeStruct((n, d), F32),
        compiler_params=_params(("parallel",), 56),
        name="merge",
    )(y_pool, y_ml, y_att, *([proj] * n_gate_blocks), x2, w_branch, w_out)


def _ple_kernel(x_ref, p_ref, g_ref, wg_ref, wp_ref, o_ref):
    x = x_ref[...]
    h = (x * lax.rsqrt(jnp.mean(x * x, axis=-1, keepdims=True) + EPS) * g_ref[...]).astype(BF16)
    pg = _sigmoid(jnp.dot(h, wg_ref[...], preferred_element_type=F32))
    pp = jnp.dot(p_ref[...].astype(BF16), wp_ref[...], preferred_element_type=F32)
    o_ref[...] = x + pg * pp


def _ple(x2, p_all, gain, w_gate, w_proj, layer, tm):
    n, d = x2.shape
    dp = p_all.shape[2]
    return pl.pallas_call(
        _ple_kernel,
        grid=(n // tm,),
        in_specs=[pl.BlockSpec((tm, d), lambda i: (i, 0)),
                  pl.BlockSpec((None, tm, dp), lambda i: (layer, i, 0)),
                  _layer_spec((1, d), layer, True), _layer_spec((d, d), layer, True),
                  _layer_spec((dp, d), layer, True)],
        out_specs=pl.BlockSpec((tm, d), lambda i: (i, 0)),
        out_shape=jax.ShapeDtypeStruct((n, d), F32),
        compiler_params=_params(("parallel",), 48),
        name="ple",
    )(x2, p_all, gain, w_gate, w_proj)


def _tile(n, preferred):
    t = min(n, preferred)
    assert n % t == 0, (n, t)
    return t


def _prepare_weights(norm_mix, w_in, w_pool_group, pool_scale, b_ig, b_fg, ml_head_norm, att_q_norm, att_k_norm,
                     att_rel_bias, w_branch, w_out, ple_norm, w_ple_gate, w_ple_proj):
    depth, d, width = w_in.shape
    assert width == W_IN_GATES_END + (COL_GATES - COL_AQ) * BRANCH_WIDTH + N_BRANCH * d
    pad = LANES - W_IN_GATE_COLS
    w_in_t = jnp.swapaxes(w_in, 1, 2)
    w_gate = jnp.pad(w_in_t[:, W_IN_GATES_START:W_IN_GATES_END, :], ((0, 0), (0, pad), (0, 0))).astype(BF16)
    gate_bias = jnp.pad(jnp.concatenate([b_ig, b_fg], axis=1), ((0, 0), (0, pad)))[:, None, :].astype(F32)
    gq = att_q_norm * (ATT_HEAD_DIM ** -0.5 * LOG2_E)
    return dict(
        norm_g=norm_mix[:, None, :], w_main=_repack_w_in_t(w_in_t), w_gate=w_gate, gate_bias=gate_bias,
        w_pool=w_pool_group.astype(BF16), pool_scale=pool_scale[:, None, :], head_norm=ml_head_norm[:, None, :],
        gq=gq[:, None, :], gk=att_k_norm[:, None, :], rel_bias=att_rel_bias,
        w_branch=w_branch.astype(BF16), w_out=w_out.astype(BF16), ple_norm=ple_norm[:, None, :],
        w_ple_gate=w_ple_gate.astype(BF16), w_ple_proj=w_ple_proj.astype(BF16))


def _layer(x3, p_all, hist, c0, n0, m0, kv_cache, state_layer, pos0, layer, lw):
    b, t, d = x3.shape
    n = b * t
    w = BRANCH_WIDTH
    prompt = kv_cache is None
    assert t % CHUNK == 0 and t >= POOL_BUF

    proj, gates = _inproj(x3.reshape(n, d), lw["norm_g"], lw["w_main"], lw["w_gate"], lw["gq"], lw["gk"],
                          layer, _tile(n, 1024))
    proj3 = proj.reshape(b, t, -1)

    if prompt:
        tq, group = _tile(t, ATT_WINDOW), 2 * CHUNK
        kprev, vprev, cache_layer = None, None, None
    else:
        tq, group = t, CHUNK
        assert t == CHUNK
        kprev, vprev = kv_cache
        cache_layer = state_layer
    bias = _attn_bias(lw["rel_bias"][layer], group)
    y_att = _attention(proj3, kprev, vprev, bias, tq=tq, group=group, cache_layer=cache_layer)

    m0b = jnp.broadcast_to(m0[..., None], m0.shape + (LANES,))
    y_ml, c1, n1, m1 = _mlstm(proj3, gates.reshape(b, t, LANES), lw["gate_bias"], lw["head_norm"],
                              c0, n0, m0b, layer, state_layer, _tile(t, 256))

    y_pool = _pool(proj3, hist, lw["w_pool"], lw["pool_scale"], layer, state_layer, tp=_tile(t, 512), pos0=pos0)

    x1 = _merge(y_pool.reshape(n, w), y_ml.reshape(n, w), y_att.reshape(n, w), proj, x3.reshape(n, d),
                lw["w_branch"], lw["w_out"], layer, _tile(n, 256))
    x2 = _ple(x1, p_all.reshape(p_all.shape[0], n, -1), lw["ple_norm"], lw["w_ple_gate"], lw["w_ple_proj"],
              layer, _tile(n, 512))

    keep = min(ATT_WINDOW, t) if prompt else t
    new_pool = proj3[:, t - POOL_BUF:, COL_PU * w:(COL_PU + 1) * w].astype(F32)
    new_k = proj3[:, t - keep:, COL_AK * w:(COL_AK + 1) * w].astype(F32).reshape(b, keep, ATT_HEADS, ATT_HEAD_DIM)
    new_v = proj3[:, t - keep:, COL_AV * w:(COL_AV + 1) * w].astype(F32).reshape(b, keep, ATT_HEADS, ATT_HEAD_DIM)
    return x2.reshape(b, t, d), (new_pool, c1, n1, m1[:, :, 0], new_k, new_v)


def kernel(x_prompt, x_sample, cache_att_k, cache_att_v, state_pool, state_mlstm_c, state_mlstm_n, state_mlstm_m, p_prompt, p_sample, norm_mix, w_in, w_pool_group, pool_scale, b_ig, b_fg, ml_head_norm, att_q_norm, att_k_norm, att_rel_bias, w_branch, w_out, ple_norm, w_ple_gate, w_ple_proj):
    xp, xs = x_prompt, x_sample
    bp = x_prompt.shape[0]
    depth = w_in.shape[0]
    lw = _prepare_weights(norm_mix, w_in, w_pool_group, pool_scale, b_ig, b_fg, ml_head_norm, att_q_norm,
                          att_k_norm, att_rel_bias, w_branch, w_out, ple_norm, w_ple_gate, w_ple_proj)
    hist0 = jnp.zeros((1, bp, POOL_BUF, BRANCH_WIDTH), F32)
    c0 = jnp.zeros((1, bp, ML_HEADS, ML_HEAD_DIM, ML_HEAD_DIM), F32)
    n0 = jnp.zeros((1, bp, ML_HEADS, ML_HEAD_DIM), F32)
    m0 = jnp.zeros((1, bp, ML_HEADS), F32)
    sp = [[] for _ in range(6)]
    ss = [[] for _ in range(6)]
    for i in range(depth):
        xp, st_p = _layer(xp, p_prompt, hist0, c0, n0, m0, None, 0, 0, i, lw)
        xs, st_s = _layer(xs, p_sample, state_pool, state_mlstm_c, state_mlstm_n, state_mlstm_m,
                          (cache_att_k, cache_att_v), i, PAST_LEN, i, lw)
        for j in range(6):
            sp[j].append(st_p[j])
            ss[j].append(st_s[j])
    pool_p, c_p, n_p, m_p, k_p, v_p = [jnp.stack(a) for a in sp]
    pool_s, c_s, n_s, m_s, k_s, v_s = [jnp.stack(a) for a in ss]
    return (xp, xs, pool_p, pool_s, c_p, c_s, n_p, n_s, m_p, m_s, k_p, k_s, v_p, v_s)
```

```python
import functools

import numpy as np
import jax
import jax.numpy as jnp
from jax import lax
from jax.experimental import pallas as pl
from jax.experimental.pallas import tpu as pltpu

F32 = jnp.float32
BF16 = jnp.bfloat16

EPS = 1e-6
CHUNK = 64
PAST_LEN = 1024

POOL_WINDOWS = (2, 4, 8, 16)
POOL_GROUP_DIM = 256
POOL_BUF = 15
POOL_HALO = 16

ML_HEADS = 4
ML_HEAD_DIM = 256

ATT_HEADS = 8
ATT_HEAD_DIM = 128
ATT_WINDOW = 512
REL_CLIP = 256
ATT_KEYS = 640

BRANCH_WIDTH = 1024
LANES = 128

N_BRANCH = 3
COL_PU, COL_PZ, COL_MQ, COL_MK, COL_MV, COL_MO, COL_MZ, COL_AQ, COL_AK, COL_AV, COL_AZ = range(11)
COL_GATES = 11
W_IN_GATES_START = 7168
W_IN_GATES_END = 7176
W_IN_GATE_COLS = W_IN_GATES_END - W_IN_GATES_START

PROJ_DTYPE = jnp.bfloat16
NEG_BIG = -1e30
LOG2_E = 1.4426950408889634
LN_2 = 0.6931471805599453
MIB = 1024 * 1024


def _params(semantics, vmem_mib):
    return pltpu.CompilerParams(dimension_semantics=semantics, vmem_limit_bytes=vmem_mib * MIB)


def _layer_spec(tail, layer, single_buffer=False):
    index_map = lambda *_: (layer,) + (0,) * len(tail)
    if single_buffer:
        return pl.BlockSpec((None,) + tuple(tail), index_map, pipeline_mode=pl.Buffered(1))
    return pl.BlockSpec((None,) + tuple(tail), index_map)


def _sigmoid(x):
    return 1.0 / (1.0 + jnp.exp(-x))


def _silu(x):
    return x * _sigmoid(x)


def _nt_dot(a, b):
    return lax.dot_general(a, b, (((1,), (1,)), ((), ())), preferred_element_type=F32)


def _tn_dot(a, b):
    return lax.dot_general(a, b, (((0,), (0,)), ((), ())), preferred_element_type=F32)


def _repack_kernel(a_ref, b_ref, o_ref, *, blocks_before):
    j = pl.program_id(1)

    @pl.when(j < blocks_before)
    def _():
        o_ref[...] = a_ref[...].astype(BF16)

    @pl.when(j >= blocks_before)
    def _():
        shifted = jnp.concatenate([a_ref[W_IN_GATE_COLS:, :], b_ref[...]], axis=0)
        o_ref[...] = shifted.astype(BF16)


def _repack_w_in_t(w_in_t):
    depth, width, d = w_in_t.shape
    tn = BRANCH_WIDTH
    out_width = width - W_IN_GATE_COLS
    assert out_width % tn == 0 and W_IN_GATES_START % tn == 0
    kernel = functools.partial(_repack_kernel, blocks_before=W_IN_GATES_START // tn)
    return pl.pallas_call(
        kernel,
        grid=(depth, out_width // tn),
        in_specs=[pl.BlockSpec((None, tn, d), lambda l, j: (l, j, 0)),
                  pl.BlockSpec((None, W_IN_GATE_COLS, d), lambda l, j: (l, (j + 1) * (tn // W_IN_GATE_COLS), 0))],
        out_specs=pl.BlockSpec((None, tn, d), lambda l, j: (l, j, 0)),
        out_shape=jax.ShapeDtypeStruct((depth, out_width, d), BF16),
        compiler_params=_params(("parallel", "arbitrary"), 40),
        name="repack_w_in",
    )(w_in_t, w_in_t)


def _inproj_kernel(x_ref, g_ref, w_ref, wg_ref, gq_ref, gk_ref, o_ref, og_ref, ogt_ref, h_ref):
    j = pl.program_id(1)

    @pl.when(j == 0)
    def _():
        x = x_ref[...]
        ms = jnp.mean(x * x, axis=-1, keepdims=True)
        h = (x * lax.rsqrt(ms + EPS) * g_ref[...]).astype(BF16)
        h_ref[...] = h
        og_ref[...] = _nt_dot(h, wg_ref[...])
        ogt_ref[...] = _nt_dot(wg_ref[...], h)

    is_q = j == COL_AQ
    head_normed = jnp.logical_or(is_q, j == COL_AK)

    @pl.when(jnp.logical_not(head_normed))
    def _():
        o_ref[...] = _nt_dot(h_ref[...], w_ref[...]).astype(o_ref.dtype)

    @pl.when(head_normed)
    def _():
        acc = _nt_dot(h_ref[...], w_ref[...])
        gain = jnp.where(is_q, gq_ref[...], gk_ref[...])
        for h in range(ATT_HEADS):
            sl = slice(h * ATT_HEAD_DIM, (h + 1) * ATT_HEAD_DIM)
            a = acc[:, sl]
            r = lax.rsqrt(jnp.mean(a * a, axis=-1, keepdims=True) + EPS)
            o_ref[:, sl] = (a * r * gain).astype(o_ref.dtype)


def _inproj(x2, gain, w_main_t, w_gate_t, gq, gk, layer, tm):
    n, d = x2.shape
    nw = w_main_t.shape[1]
    tn = BRANCH_WIDTH
    assert tn == ATT_HEADS * ATT_HEAD_DIM
    return pl.pallas_call(
        _inproj_kernel,
        grid=(n // tm, nw // tn),
        in_specs=[pl.BlockSpec((tm, d), lambda i, j: (i, 0)),
                  _layer_spec((1, d), layer),
                  pl.BlockSpec((None, tn, d), lambda i, j: (layer, j, 0)),
                  _layer_spec((LANES, d), layer),
                  _layer_spec((1, ATT_HEAD_DIM), layer),
                  _layer_spec((1, ATT_HEAD_DIM), layer)],
        out_specs=[pl.BlockSpec((tm, tn), lambda i, j: (i, j)),
                   pl.BlockSpec((tm, LANES), lambda i, j: (i, 0)),
                   pl.BlockSpec((LANES, tm), lambda i, j: (0, i))],
        out_shape=[jax.ShapeDtypeStruct((n, nw), PROJ_DTYPE),
                   jax.ShapeDtypeStruct((n, LANES), F32),
                   jax.ShapeDtypeStruct((LANES, n), F32)],
        scratch_shapes=[pltpu.VMEM((tm, d), BF16)],
        compiler_params=_params(("parallel", "arbitrary"), 48),
        name="inproj",
    )(x2, gain, w_main_t, w_gate_t, gq, gk)


def _attn_bias(rel_bias, group):
    period = ATT_KEYS + group
    m = np.arange(period)
    m = np.where(m <= ATT_KEYS, m, m - period)
    idx = np.clip(ATT_WINDOW - m, -REL_CLIP, REL_CLIP) + REL_CLIP
    vec = rel_bias[:, idx].astype(F32)
    heads = rel_bias.shape[0]
    bias = jnp.tile(vec, (1, group))[:, :group * (period - 1)].reshape(heads, group, period - 1)[:, :, :ATT_KEYS]
    i = np.arange(group)[:, None]
    j = np.arange(ATT_KEYS)[None, :]
    lo = (i // CHUNK) * CHUNK
    band = (j >= lo) & (j < lo + ATT_WINDOW + CHUNK)
    return jnp.where(jnp.asarray(band)[None], bias * LOG2_E, NEG_BIG)


def _attn_kernel(q_ref, kp_ref, kc_ref, vp_ref, vc_ref, az_ref, bias_ref, o_ref, kw_ref, vw_ref, s_ref, m_ref,
                 *, tq, group, first_prev_invalid, prev_by_head):
    rows = kw_ref.shape[0]
    if prev_by_head:
        for h in range(ATT_HEADS):
            sl = slice(h * ATT_HEAD_DIM, (h + 1) * ATT_HEAD_DIM)
            kw_ref[0:ATT_WINDOW, sl] = kp_ref[pl.ds(h, ATT_WINDOW, stride=ATT_HEADS), :].astype(BF16)
            vw_ref[0:ATT_WINDOW, sl] = vp_ref[pl.ds(h, ATT_WINDOW, stride=ATT_HEADS), :].astype(BF16)
    else:
        kw_ref[0:ATT_WINDOW, :] = kp_ref[0].astype(BF16)
        vw_ref[0:ATT_WINDOW, :] = vp_ref[0].astype(BF16)
    kw_ref[ATT_WINDOW:ATT_WINDOW + tq, :] = kc_ref[0].astype(BF16)
    vw_ref[ATT_WINDOW:ATT_WINDOW + tq, :] = vc_ref[0].astype(BF16)
    if rows > ATT_WINDOW + tq:
        pad = jnp.zeros((rows - ATT_WINDOW - tq, kw_ref.shape[1]), BF16)
        kw_ref[ATT_WINDOW + tq:rows, :] = pad
        vw_ref[ATT_WINDOW + tq:rows, :] = pad

    def attend(mask_prev):
        col = lax.broadcasted_iota(jnp.int32, (group, ATT_KEYS), 1)
        ones = jnp.ones((ATT_KEYS, ATT_HEAD_DIM), BF16)
        for g in range(tq // group):
            r0 = g * group
            for h in range(ATT_HEADS):
                sl = slice(h * ATT_HEAD_DIM, (h + 1) * ATT_HEAD_DIM)
                q = q_ref[0, r0:r0 + group, sl].astype(BF16)
                k = kw_ref[r0:r0 + ATT_KEYS, sl]
                s = _nt_dot(q, k) + bias_ref[h]
                if mask_prev:
                    s = jnp.where(col + r0 < ATT_WINDOW, NEG_BIG, s)
                s_ref[h] = s
                m_ref[h] = jnp.max(s, axis=-1, keepdims=True)
            for h in range(ATT_HEADS):
                sl = slice(h * ATT_HEAD_DIM, (h + 1) * ATT_HEAD_DIM)
                v1 = jnp.concatenate([vw_ref[r0:r0 + ATT_KEYS, sl], ones], axis=1)
                p = jnp.exp2(s_ref[h] - m_ref[h])
                o = jnp.dot(p.astype(BF16), v1, preferred_element_type=F32)
                o = o[:, :ATT_HEAD_DIM] / o[:, ATT_HEAD_DIM:]
                z = az_ref[0, r0:r0 + group, sl].astype(F32)
                o_ref[0, r0:r0 + group, sl] = (o * _silu(z)).astype(o_ref.dtype)

    if first_prev_invalid:
        first = pl.program_id(1) == 0
        pl.when(first)(functools.partial(attend, True))
        pl.when(jnp.logical_not(first))(functools.partial(attend, False))
    else:
        attend(False)


def _attention(proj3, kprev, vprev, bias, *, tq, group, cache_layer):
    b, t, _ = proj3.shape
    w = BRANCH_WIDTH
    nt = t // tq
    prompt = cache_layer is None
    if prompt:
        assert tq == ATT_WINDOW
        kprev = vprev = proj3
        kprev_spec = pl.BlockSpec((1, ATT_WINDOW, w), lambda bi, i: (bi, jnp.maximum(i - 1, 0), COL_AK))
        vprev_spec = pl.BlockSpec((1, ATT_WINDOW, w), lambda bi, i: (bi, jnp.maximum(i - 1, 0), COL_AV))
    else:
        assert nt == 1 and kprev.shape[2:] == (ATT_WINDOW, ATT_HEADS, ATT_HEAD_DIM)
        depth = kprev.shape[0]
        kprev = kprev.reshape(depth, b, ATT_WINDOW * ATT_HEADS, ATT_HEAD_DIM)
        vprev = vprev.reshape(depth, b, ATT_WINDOW * ATT_HEADS, ATT_HEAD_DIM)
        kprev_spec = pl.BlockSpec((None, None, ATT_WINDOW * ATT_HEADS, ATT_HEAD_DIM),
                                  lambda bi, i: (cache_layer, bi, 0, 0))
        vprev_spec = kprev_spec
    rows = tq + ATT_KEYS - group
    kernel = functools.partial(_attn_kernel, tq=tq, group=group, first_prev_invalid=prompt, prev_by_head=not prompt)
    return pl.pallas_call(
        kernel,
        grid=(b, nt),
        in_specs=[pl.BlockSpec((1, tq, w), lambda bi, i: (bi, i, COL_AQ)),
                  kprev_spec,
                  pl.BlockSpec((1, tq, w), lambda bi, i: (bi, i, COL_AK)),
                  vprev_spec,
                  pl.BlockSpec((1, tq, w), lambda bi, i: (bi, i, COL_AV)),
                  pl.BlockSpec((1, tq, w), lambda bi, i: (bi, i, COL_AZ)),
                  pl.BlockSpec((ATT_HEADS, group, ATT_KEYS), lambda bi, i: (0, 0, 0))],
        out_specs=pl.BlockSpec((1, tq, w), lambda bi, i: (bi, i, 0)),
        out_shape=jax.ShapeDtypeStruct((b, t, w), BF16),
        scratch_shapes=[pltpu.VMEM((rows, w), BF16), pltpu.VMEM((rows, w), BF16),
                        pltpu.VMEM((ATT_HEADS, group, ATT_KEYS), F32), pltpu.VMEM((ATT_HEADS, group, 1), F32)],
        compiler_params=_params(("parallel", "arbitrary"), 48),
        name="attention",
    )(proj3, kprev, proj3, vprev, proj3, proj3, bias)


def _log_sigmoid(x):
    return jnp.minimum(x, 0.0) - jnp.log1p(jnp.exp(-jnp.abs(x)))


def _split3(x):
    hi = x.astype(BF16)
    rest = x - hi.astype(F32)
    mid = rest.astype(BF16)
    lo = (rest - mid.astype(F32)).astype(BF16)
    return hi, mid, lo


def _mlstm_kernel(q_ref, k_ref, v_ref, o_ref, z_ref, g_ref, gt_ref, gb_ref, gbt_ref, hn_ref, tri_ref,
                  c0_ref, n0_ref, m0_ref, y_ref, c_ref, n_ref, m_ref, *, chunk, rows_from_input):
    L = chunk
    D = ML_HEAD_DIM

    @pl.when(pl.program_id(1) == 0)
    def _():
        c_ref[...] = c0_ref[...]
        n_ref[...] = n0_ref[...]
        m_ref[...] = m0_ref[...]

    row = lax.broadcasted_iota(jnp.int32, (L, L), 0)
    col = lax.broadcasted_iota(jnp.int32, (L, L), 1)
    causal = col <= row

    n_gates = 2 * ML_HEADS
    gates = (g_ref[0] + gb_ref[...]) * LOG2_E
    if rows_from_input:
        gates_t = gt_ref[0:n_gates, :] + gbt_ref[0:n_gates, :]
        fill = (jnp.zeros((n_gates, L), BF16),)
        a_rows = jnp.dot(jnp.concatenate(_split3(_log_sigmoid(gates_t)) + fill, axis=0), tri_ref[1],
                         preferred_element_type=F32)
        a_rows = (a_rows[:n_gates] + a_rows[n_gates:2 * n_gates] + a_rows[2 * n_gates:3 * n_gates]) * LOG2_E
        gates_t = gates_t * LOG2_E
        pick = (lax.broadcasted_iota(jnp.int32, (4 * n_gates, LANES), 0) % n_gates
                == lax.broadcasted_iota(jnp.int32, (4 * n_gates, LANES), 1)).astype(BF16)
        a_cols = _tn_dot(jnp.concatenate(_split3(a_rows) + fill, axis=0), pick)
    else:
        logf = _log_sigmoid(g_ref[0] + gb_ref[...])
        a_cols = jnp.dot(tri_ref[0], jnp.concatenate(_split3(logf), axis=1), preferred_element_type=F32)
        a_cols = (a_cols[:, :LANES] + a_cols[:, LANES:2 * LANES] + a_cols[:, 2 * LANES:]) * LOG2_E
        eye = row == col

    ones = jnp.ones((L, LANES), BF16)
    k_scale = jnp.asarray(D ** -0.5, BF16)

    for hd in range(ML_HEADS):
        sl = slice(hd * D, (hd + 1) * D)
        ig_col = gates[:, hd:hd + 1]
        a_col = a_cols[:, ML_HEADS + hd:ML_HEADS + hd + 1]
        if rows_from_input:
            ig_row = gates_t[hd:hd + 1, :]
            a_row = a_rows[ML_HEADS + hd:ML_HEADS + hd + 1, :]
        else:
            ig_row = jnp.sum(jnp.where(eye, ig_col, 0.0), axis=0, keepdims=True)
            a_row = jnp.sum(jnp.where(eye, a_col, 0.0), axis=0, keepdims=True)
        b = a_col[L - 1:L, :]
        m_prev = m_ref[0, hd:hd + 1, 0:1] * LOG2_E
        c_prev = c_ref[0, hd]
        n_prev = n_ref[0, hd:hd + 1, :]

        logd = jnp.where(causal, a_col - a_row + ig_row, NEG_BIG)
        inter = a_col + m_prev
        m_row = jnp.maximum(inter, jnp.max(logd, axis=1, keepdims=True))
        dmat = jnp.exp2(logd - m_row)
        w_inter = jnp.exp2(inter - m_row)

        qb = q_ref[0, :, sl].astype(BF16)
        kb = k_ref[0, :, sl].astype(BF16) * k_scale
        v1 = jnp.concatenate([v_ref[0, :, sl].astype(BF16), ones], axis=1)
        c1 = jnp.concatenate([c_prev.astype(BF16), jnp.broadcast_to(n_prev.astype(BF16), (LANES, D))], axis=0)

        s = _nt_dot(qb, kb) * dmat
        sv = jnp.dot(s.astype(BF16), v1, preferred_element_type=F32)
        qc = _nt_dot(qb, c1)
        num = sv[:, :D] + w_inter * qc[:, :D]
        den = sv[:, D:] + w_inter * qc[:, D:]
        r = 1.0 / jnp.maximum(jnp.abs(den), jnp.exp2(-m_row))
        h = num * jnp.concatenate([r] * (D // LANES), axis=1)

        g_col = b - a_col + ig_col
        m_new = jnp.maximum(b + m_prev, jnp.max(g_col, axis=0, keepdims=True))
        wk = jnp.exp2(g_col - m_new)
        decay = jnp.exp2(b + m_prev - m_new)
        kw = kb * wk.astype(BF16)
        upd = _tn_dot(v1, kw)
        c_ref[0, hd] = decay * c_prev + upd[:D]
        n_ref[0, hd:hd + 1, :] = decay * n_prev + upd[D:D + 1]
        m_ref[0, hd:hd + 1, :] = jnp.broadcast_to(m_new * LN_2, (1, LANES))

        hm = h * _sigmoid(o_ref[0, :, sl].astype(F32))
        hm = hm * lax.rsqrt(jnp.mean(hm * hm, axis=-1, keepdims=True) + EPS) * hn_ref[:, sl]
        y_ref[0, :, sl] = (hm * _silu(z_ref[0, :, sl].astype(F32))).astype(y_ref.dtype)


def _mlstm(proj3, gates3, gates_t, gate_bias, head_norm, c0, n0, m0, layer, state_layer, chunk):
    b, t, _ = proj3.shape
    w = BRANCH_WIDTH
    h, d = ML_HEADS, ML_HEAD_DIM
    nc = t // chunk
    rows_from_input = chunk % LANES == 0
    col = lambda c: pl.BlockSpec((1, chunk, w), lambda bi, i, c=c: (bi, i, c))
    state_in = lambda shape: pl.BlockSpec((None, 1) + shape, lambda bi, i: (state_layer, bi) + (0,) * len(shape))
    state_out = lambda shape: pl.BlockSpec((1,) + shape, lambda bi, i: (bi,) + (0,) * len(shape))
    if rows_from_input:
        gates_t_spec = pl.BlockSpec((LANES, chunk), lambda bi, i: (0, bi * nc + i))
    else:
        gates_t_spec = pl.BlockSpec((LANES, LANES), lambda bi, i: (0, 0))
    lower = jnp.tril(jnp.ones((chunk, chunk), BF16))
    triangles = jnp.stack([lower, lower.T])
    gate_bias_t = jnp.swapaxes(gate_bias, 1, 2)
    kernel = functools.partial(_mlstm_kernel, chunk=chunk, rows_from_input=rows_from_input)
    return pl.pallas_call(
        kernel,
        grid=(b, nc),
        in_specs=[col(COL_MQ), col(COL_MK), col(COL_MV), col(COL_MO), col(COL_MZ),
                  pl.BlockSpec((1, chunk, LANES), lambda bi, i: (bi, i, 0)),
                  gates_t_spec,
                  _layer_spec((1, LANES), layer),
                  _layer_spec((LANES, 1), layer),
                  _layer_spec((1, w), layer),
                  pl.BlockSpec((2, chunk, chunk), lambda bi, i: (0, 0, 0)),
                  state_in((h, d, d)), state_in((h, d)), state_in((h, LANES))],
        out_specs=[pl.BlockSpec((1, chunk, w), lambda bi, i: (bi, i, 0)),
                   state_out((h, d, d)), state_out((h, d)), state_out((h, LANES))],
        out_shape=[jax.ShapeDtypeStruct((b, t, w), BF16),
                   jax.ShapeDtypeStruct((b, h, d, d), F32),
                   jax.ShapeDtypeStruct((b, h, d), F32),
                   jax.ShapeDtypeStruct((b, h, LANES), F32)],
        compiler_params=_params(("parallel", "arbitrary"), 48),
        name="mlstm",
    )(proj3, proj3, proj3, proj3, proj3, gates3, gates_t, gate_bias, gate_bias_t, head_norm, triangles,
      c0, n0, m0)


def _pool_prev_rows(hist_ref, halo_ref, first):
    hist = hist_ref[0].astype(F32)
    hist = jnp.concatenate([jnp.zeros((POOL_HALO - POOL_BUF, hist.shape[1]), F32), hist], axis=0)
    return jnp.where(first, hist, halo_ref[...].astype(F32))


def _pool_mix_tile(u, prev, z_ref, w_ref, sc_ref, pos):
    outs = []
    for gi, win in enumerate(POOL_WINDOWS):
        sl = slice(gi * POOL_GROUP_DIM, (gi + 1) * POOL_GROUP_DIM)
        ext = jnp.concatenate([prev[:, sl], u[:, sl]], axis=0)
        acc = ext
        span = 1
        while span < win:
            acc = acc + pltpu.roll(acc, span, 0)
            span *= 2
        cnt = jnp.minimum(pos + 1, win).astype(F32)
        mean = acc[POOL_HALO:, :] / cnt
        m = (mean - u[:, sl]).astype(BF16)
        y = jnp.dot(m, w_ref[gi], preferred_element_type=F32) * sc_ref[:, sl]
        outs.append((y * _silu(z_ref[:, sl].astype(F32))).astype(BF16))
    return jnp.concatenate(outs, axis=1)


def _pool_kernel(u_ref, halo_ref, hist_ref, z_ref, w_ref, sc_ref, y_ref, *, tp, pos0):
    i = pl.program_id(1)
    prev = _pool_prev_rows(hist_ref, halo_ref.at[0], i == 0)
    pos = pos0 + i * tp + lax.broadcasted_iota(jnp.int32, (tp, 1), 0)
    y_ref[0] = _pool_mix_tile(u_ref[0].astype(F32), prev, z_ref.at[0], w_ref, sc_ref, pos)


def _pool(proj3, hist, w_group, scale, layer, hist_layer, *, tp, pos0):
    b, t, _ = proj3.shape
    w = BRANCH_WIDTH
    halo_blocks = tp // POOL_HALO
    kernel = functools.partial(_pool_kernel, tp=tp, pos0=pos0)
    return pl.pallas_call(
        kernel,
        grid=(b, t // tp),
        in_specs=[pl.BlockSpec((1, tp, w), lambda bi, i: (bi, i, COL_PU)),
                  pl.BlockSpec((1, POOL_HALO, w), lambda bi, i: (bi, jnp.maximum(i * halo_blocks - 1, 0), COL_PU)),
                  pl.BlockSpec((None, 1, POOL_BUF, w), lambda bi, i: (hist_layer, bi, 0, 0)),
                  pl.BlockSpec((1, tp, w), lambda bi, i: (bi, i, COL_PZ)),
                  _layer_spec((len(POOL_WINDOWS), POOL_GROUP_DIM, POOL_GROUP_DIM), layer),
                  _layer_spec((1, w), layer)],
        out_specs=pl.BlockSpec((1, tp, w), lambda bi, i: (bi, i, 0)),
        out_shape=jax.ShapeDtypeStruct((b, t, w), BF16),
        compiler_params=_params(("parallel", "arbitrary"), 32),
        name="pool",
    )(proj3, proj3, hist, proj3, w_group, scale)


def _merge_kernel(*refs):
    y_refs = refs[:N_BRANCH]
    gate_refs = refs[N_BRANCH:-4]
    x_ref, wb_ref, wo_ref, o_ref = refs[-4:]
    per_branch = len(gate_refs) // N_BRANCH
    merged = None
    for b in range(N_BRANCH):
        gate = jnp.concatenate([_sigmoid(r[...].astype(F32)) for r in gate_refs[b * per_branch:(b + 1) * per_branch]],
                               axis=1)
        term = gate * jnp.dot(y_refs[b][...], wb_ref[b], preferred_element_type=F32)
        merged = term if merged is None else merged + term
    o_ref[...] = x_ref[...] + jnp.dot(merged.astype(BF16), wo_ref[...], preferred_element_type=F32)


def _merge(y_pool, y_ml, y_att, proj, x2, w_branch, w_out, layer, tm):
    n, d = x2.shape
    w = BRANCH_WIDTH
    row = lambda width: pl.BlockSpec((tm, width), lambda i: (i, 0))
    n_gate_blocks = N_BRANCH * (d // w)
    gate_specs = [pl.BlockSpec((tm, w), lambda i, k=k: (i, COL_GATES + k)) for k in range(n_gate_blocks)]
    return pl.pallas_call(
        _merge_kernel,
        grid=(n // tm,),
        in_specs=[row(w)] * N_BRANCH + gate_specs + [row(d), _layer_spec((N_BRANCH, w, d), layer, True),
                                                    _layer_spec((d, d), layer, True)],
        out_specs=row(d),
        out_shape=jax.ShapeDtypeStruct((n, d), F32),
        compiler_params=_params(("parallel",), 56),
        name="merge",
    )(y_pool, y_ml, y_att, *([proj] * n_gate_blocks), x2, w_branch, w_out)


def _ple_kernel(x_ref, p_ref, g_ref, wg_ref, wp_ref, o_ref):
    x = x_ref[...]
    h = (x * lax.rsqrt(jnp.mean(x * x, axis=-1, keepdims=True) + EPS) * g_ref[...]).astype(BF16)
    pg = _sigmoid(jnp.dot(h, wg_ref[...], preferred_element_type=F32))
    pp = jnp.dot(p_ref[...].astype(BF16), wp_ref[...], preferred_element_type=F32)
    o_ref[...] = x + pg * pp


def _ple(x2, p_all, gain, w_gate, w_proj, layer, tm):
    n, d = x2.shape
    dp = p_all.shape[2]
    return pl.pallas_call(
        _ple_kernel,
        grid=(n // tm,),
        in_specs=[pl.BlockSpec((tm, d), lambda i: (i, 0)),
                  pl.BlockSpec((None, tm, dp), lambda i: (layer, i, 0)),
                  _layer_spec((1, d), layer, True), _layer_spec((d, d), layer, True),
                  _layer_spec((dp, d), layer, True)],
        out_specs=pl.BlockSpec((tm, d), lambda i: (i, 0)),
        out_shape=jax.ShapeDtypeStruct((n, d), F32),
        compiler_params=_params(("parallel",), 48),
        name="ple",
    )(x2, p_all, gain, w_gate, w_proj)


def _tile(n, preferred):
    t = min(n, preferred)
    assert n % t == 0, (n, t)
    return t


def _prepare_weights(norm_mix, w_in, w_pool_group, pool_scale, b_ig, b_fg, ml_head_norm, att_q_norm, att_k_norm,
                     att_rel_bias, w_branch, w_out, ple_norm, w_ple_gate, w_ple_proj):
    depth, d, width = w_in.shape
    assert width == W_IN_GATES_END + (COL_GATES - COL_AQ) * BRANCH_WIDTH + N_BRANCH * d
    pad = LANES - W_IN_GATE_COLS
    w_in_t = jnp.swapaxes(w_in, 1, 2)
    w_gate = jnp.pad(w_in_t[:, W_IN_GATES_START:W_IN_GATES_END, :], ((0, 0), (0, pad), (0, 0))).astype(BF16)
    gate_bias = jnp.pad(jnp.concatenate([b_ig, b_fg], axis=1), ((0, 0), (0, pad)))[:, None, :].astype(F32)
    gq = att_q_norm * (ATT_HEAD_DIM ** -0.5 * LOG2_E)
    return dict(
        norm_g=norm_mix[:, None, :], w_main=_repack_w_in_t(w_in_t), w_gate=w_gate, gate_bias=gate_bias,
        w_pool=w_pool_group.astype(BF16), pool_scale=pool_scale[:, None, :], head_norm=ml_head_norm[:, None, :],
        gq=gq[:, None, :], gk=att_k_norm[:, None, :], rel_bias=att_rel_bias,
        w_branch=w_branch.astype(BF16), w_out=w_out.astype(BF16), ple_norm=ple_norm[:, None, :],
        w_ple_gate=w_ple_gate.astype(BF16), w_ple_proj=w_ple_proj.astype(BF16))


def _layer(x3, p_all, hist, c0, n0, m0, kv_cache, state_layer, pos0, layer, lw):
    b, t, d = x3.shape
    n = b * t
    w = BRANCH_WIDTH
    prompt = kv_cache is None
    assert t % CHUNK == 0 and t >= POOL_BUF

    proj, gates, gates_t = _inproj(x3.reshape(n, d), lw["norm_g"], lw["w_main"], lw["w_gate"], lw["gq"], lw["gk"],
                                   layer, _tile(n, 1024))
    proj3 = proj.reshape(b, t, -1)

    if prompt:
        tq, group = _tile(t, ATT_WINDOW), 2 * CHUNK
        kprev, vprev, cache_layer = None, None, None
    else:
        tq, group = t, CHUNK
        assert t == CHUNK
        kprev, vprev = kv_cache
        cache_layer = state_layer
    bias = _attn_bias(lw["rel_bias"][layer], group)
    y_att = _attention(proj3, kprev, vprev, bias, tq=tq, group=group, cache_layer=cache_layer)

    m0b = jnp.broadcast_to(m0[..., None], m0.shape + (LANES,))
    y_ml, c1, n1, m1 = _mlstm(proj3, gates.reshape(b, t, LANES), gates_t, lw["gate_bias"], lw["head_norm"],
                              c0, n0, m0b, layer, state_layer, _tile(t, 256))

    y_pool = _pool(proj3, hist, lw["w_pool"], lw["pool_scale"], layer, state_layer, tp=_tile(t, 512), pos0=pos0)

    x1 = _merge(y_pool.reshape(n, w), y_ml.reshape(n, w), y_att.reshape(n, w), proj, x3.reshape(n, d),
                lw["w_branch"], lw["w_out"], layer, _tile(n, 256))
    x2 = _ple(x1, p_all.reshape(p_all.shape[0], n, -1), lw["ple_norm"], lw["w_ple_gate"], lw["w_ple_proj"],
              layer, _tile(n, 512))

    keep = min(ATT_WINDOW, t) if prompt else t
    new_pool = proj3[:, t - POOL_BUF:, COL_PU * w:(COL_PU + 1) * w].astype(F32)
    new_k = proj3[:, t - keep:, COL_AK * w:(COL_AK + 1) * w].astype(F32).reshape(b, keep, ATT_HEADS, ATT_HEAD_DIM)
    new_v = proj3[:, t - keep:, COL_AV * w:(COL_AV + 1) * w].astype(F32).reshape(b, keep, ATT_HEADS, ATT_HEAD_DIM)
    return x2.reshape(b, t, d), (new_pool, c1, n1, m1[:, :, 0], new_k, new_v)


def kernel(x_prompt, x_sample, cache_att_k, cache_att_v, state_pool, state_mlstm_c, state_mlstm_n, state_mlstm_m, p_prompt, p_sample, norm_mix, w_in, w_pool_group, pool_scale, b_ig, b_fg, ml_head_norm, att_q_norm, att_k_norm, att_rel_bias, w_branch, w_out, ple_norm, w_ple_gate, w_ple_proj):
    xp, xs = x_prompt, x_sample
    bp = x_prompt.shape[0]
    depth = w_in.shape[0]
    lw = _prepare_weights(norm_mix, w_in, w_pool_group, pool_scale, b_ig, b_fg, ml_head_norm, att_q_norm,
                          att_k_norm, att_rel_bias, w_branch, w_out, ple_norm, w_ple_gate, w_ple_proj)
    hist0 = jnp.zeros((1, bp, POOL_BUF, BRANCH_WIDTH), F32)
    c0 = jnp.zeros((1, bp, ML_HEADS, ML_HEAD_DIM, ML_HEAD_DIM), F32)
    n0 = jnp.zeros((1, bp, ML_HEADS, ML_HEAD_DIM), F32)
    m0 = jnp.zeros((1, bp, ML_HEADS), F32)
    sp = [[] for _ in range(6)]
    ss = [[] for _ in range(6)]
    for i in range(depth):
        xp, st_p = _layer(xp, p_prompt, hist0, c0, n0, m0, None, 0, 0, i, lw)
        xs, st_s = _layer(xs, p_sample, state_pool, state_mlstm_c, state_mlstm_n, state_mlstm_m,
                          (cache_att_k, cache_att_v), i, PAST_LEN, i, lw)
        for j in range(6):
            sp[j].append(st_p[j])
            ss[j].append(st_s[j])
    pool_p, c_p, n_p, m_p, k_p, v_p = [jnp.stack(a) for a in sp]
    pool_s, c_s, n_s, m_s, k_s, v_s = [jnp.stack(a) for a in ss]
    return (xp, xs, pool_p, pool_s, c_p, c_s, n_p, n_s, m_p, m_s, k_p, k_s, v_p, v_s)
```

```python
import functools

import numpy as np
import jax
import jax.numpy as jnp
from jax import lax
from jax.experimental import pallas as pl
from jax.experimental.pallas import tpu as pltpu

F32 = jnp.float32
BF16 = jnp.bfloat16

EPS = 1e-6
CHUNK = 64
PAST_LEN = 1024

POOL_WINDOWS = (2, 4, 8, 16)
POOL_GROUP_DIM = 256
POOL_BUF = 15
POOL_HALO = 16

ML_HEADS = 4
ML_HEAD_DIM = 256

ATT_HEADS = 8
ATT_HEAD_DIM = 128
ATT_WINDOW = 512
REL_CLIP = 256

BRANCH_WIDTH = 1024
LANES = 128

N_BRANCH = 3
COL_PU, COL_PZ, COL_MQ, COL_MK, COL_MV, COL_MO, COL_MZ, COL_AQ, COL_AK, COL_AV, COL_AZ = range(11)
COL_GATES = 11
W_IN_GATES_START = 7168
W_IN_GATES_END = 7176
W_IN_GATE_COLS = W_IN_GATES_END - W_IN_GATES_START

PROJ_DTYPE = jnp.bfloat16
NEG_BIG = -1e30
LOG2_E = 1.4426950408889634
LN_2 = 0.6931471805599453
MIB = 1024 * 1024


def _params(semantics, vmem_mib):
    return pltpu.CompilerParams(dimension_semantics=semantics, vmem_limit_bytes=vmem_mib * MIB)


def _layer_spec(tail, layer, single_buffer=False):
    index_map = lambda *_: (layer,) + (0,) * len(tail)
    if single_buffer:
        return pl.BlockSpec((None,) + tuple(tail), index_map, pipeline_mode=pl.Buffered(1))
    return pl.BlockSpec((None,) + tuple(tail), index_map)


def _sigmoid(x):
    return 1.0 / (1.0 + jnp.exp(-x))


def _silu(x):
    return x * _sigmoid(x)


def _nt_dot(a, b):
    return lax.dot_general(a, b, (((1,), (1,)), ((), ())), preferred_element_type=F32)


def _tn_dot(a, b):
    return lax.dot_general(a, b, (((0,), (0,)), ((), ())), preferred_element_type=F32)


def _repack_kernel(a_ref, b_ref, o_ref, *, blocks_before):
    j = pl.program_id(1)

    @pl.when(j < blocks_before)
    def _():
        o_ref[...] = a_ref[...].astype(BF16)

    @pl.when(j >= blocks_before)
    def _():
        shifted = jnp.concatenate([a_ref[W_IN_GATE_COLS:, :], b_ref[...]], axis=0)
        o_ref[...] = shifted.astype(BF16)


def _repack_w_in_t(w_in_t):
    depth, width, d = w_in_t.shape
    tn = BRANCH_WIDTH
    out_width = width - W_IN_GATE_COLS
    assert out_width % tn == 0 and W_IN_GATES_START % tn == 0
    kernel = functools.partial(_repack_kernel, blocks_before=W_IN_GATES_START // tn)
    return pl.pallas_call(
        kernel,
        grid=(depth, out_width // tn),
        in_specs=[pl.BlockSpec((None, tn, d), lambda l, j: (l, j, 0)),
                  pl.BlockSpec((None, W_IN_GATE_COLS, d), lambda l, j: (l, (j + 1) * (tn // W_IN_GATE_COLS), 0))],
        out_specs=pl.BlockSpec((None, tn, d), lambda l, j: (l, j, 0)),
        out_shape=jax.ShapeDtypeStruct((depth, out_width, d), BF16),
        compiler_params=_params(("parallel", "arbitrary"), 40),
        name="repack_w_in",
    )(w_in_t, w_in_t)


def _inproj_kernel(x_ref, g_ref, w_ref, wg_ref, gq_ref, gk_ref, o_ref, og_ref, ogt_ref, h_ref):
    j = pl.program_id(1)

    @pl.when(j == 0)
    def _():
        x = x_ref[...]
        ms = jnp.mean(x * x, axis=-1, keepdims=True)
        h = (x * lax.rsqrt(ms + EPS) * g_ref[...]).astype(BF16)
        h_ref[...] = h
        og_ref[...] = _nt_dot(h, wg_ref[...])
        ogt_ref[...] = _nt_dot(wg_ref[...], h)

    is_q = j == COL_AQ
    head_normed = jnp.logical_or(is_q, j == COL_AK)

    @pl.when(jnp.logical_not(head_normed))
    def _():
        o_ref[...] = _nt_dot(h_ref[...], w_ref[...]).astype(o_ref.dtype)

    @pl.when(head_normed)
    def _():
        acc = _nt_dot(h_ref[...], w_ref[...])
        gain = jnp.where(is_q, gq_ref[...], gk_ref[...])
        for h in range(ATT_HEADS):
            sl = slice(h * ATT_HEAD_DIM, (h + 1) * ATT_HEAD_DIM)
            a = acc[:, sl]
            r = lax.rsqrt(jnp.mean(a * a, axis=-1, keepdims=True) + EPS)
            o_ref[:, sl] = (a * r * gain).astype(o_ref.dtype)


def _inproj(x2, gain, w_main_t, w_gate_t, gq, gk, layer, tm):
    n, d = x2.shape
    nw = w_main_t.shape[1]
    tn = BRANCH_WIDTH
    assert tn == ATT_HEADS * ATT_HEAD_DIM
    return pl.pallas_call(
        _inproj_kernel,
        grid=(n // tm, nw // tn),
        in_specs=[pl.BlockSpec((tm, d), lambda i, j: (i, 0)),
                  _layer_spec((1, d), layer),
                  pl.BlockSpec((None, tn, d), lambda i, j: (layer, j, 0)),
                  _layer_spec((LANES, d), layer),
                  _layer_spec((1, ATT_HEAD_DIM), layer),
                  _layer_spec((1, ATT_HEAD_DIM), layer)],
        out_specs=[pl.BlockSpec((tm, tn), lambda i, j: (i, j)),
                   pl.BlockSpec((tm, LANES), lambda i, j: (i, 0)),
                   pl.BlockSpec((LANES, tm), lambda i, j: (0, i))],
        out_shape=[jax.ShapeDtypeStruct((n, nw), PROJ_DTYPE),
                   jax.ShapeDtypeStruct((n, LANES), F32),
                   jax.ShapeDtypeStruct((LANES, n), F32)],
        scratch_shapes=[pltpu.VMEM((tm, d), BF16)],
        compiler_params=_params(("parallel", "arbitrary"), 48),
        name="inproj",
    )(x2, gain, w_main_t, w_gate_t, gq, gk)


def _att_keys(group):
    return -(-(ATT_WINDOW + group) // LANES) * LANES


def _attn_bias(rel_bias, group):
    keys = _att_keys(group)
    period = keys + group
    m = np.arange(period)
    m = np.where(m <= keys, m, m - period)
    idx = np.clip(ATT_WINDOW - m, -REL_CLIP, REL_CLIP) + REL_CLIP
    vec = rel_bias[:, idx].astype(F32)
    heads = rel_bias.shape[0]
    bias = jnp.tile(vec, (1, group))[:, :group * (period - 1)].reshape(heads, group, period - 1)[:, :, :keys]
    i = np.arange(group)[:, None]
    j = np.arange(keys)[None, :]
    lo = (i // CHUNK) * CHUNK
    band = (j >= lo) & (j < lo + ATT_WINDOW + CHUNK)
    return jnp.where(jnp.asarray(band)[None], bias * LOG2_E, NEG_BIG)


def _attn_kernel(q_ref, kp_ref, kc_ref, vp_ref, vc_ref, az_ref, bias_ref, o_ref, kw_ref, vw_ref, s_ref, m_ref,
                 *, tq, group, first_prev_invalid, prev_by_head):
    rows = kw_ref.shape[0]
    keys = bias_ref.shape[2]
    if prev_by_head:
        for h in range(ATT_HEADS):
            sl = slice(h * ATT_HEAD_DIM, (h + 1) * ATT_HEAD_DIM)
            kw_ref[0:ATT_WINDOW, sl] = kp_ref[pl.ds(h, ATT_WINDOW, stride=ATT_HEADS), :].astype(BF16)
            vw_ref[0:ATT_WINDOW, sl] = vp_ref[pl.ds(h, ATT_WINDOW, stride=ATT_HEADS), :].astype(BF16)
    else:
        kw_ref[0:ATT_WINDOW, :] = kp_ref[0].astype(BF16)
        vw_ref[0:ATT_WINDOW, :] = vp_ref[0].astype(BF16)
    kw_ref[ATT_WINDOW:ATT_WINDOW + tq, :] = kc_ref[0].astype(BF16)
    vw_ref[ATT_WINDOW:ATT_WINDOW + tq, :] = vc_ref[0].astype(BF16)
    if rows > ATT_WINDOW + tq:
        pad = jnp.zeros((rows - ATT_WINDOW - tq, kw_ref.shape[1]), BF16)
        kw_ref[ATT_WINDOW + tq:rows, :] = pad
        vw_ref[ATT_WINDOW + tq:rows, :] = pad

    def attend(mask_prev):
        col = lax.broadcasted_iota(jnp.int32, (group, keys), 1)
        ones = jnp.ones((keys, ATT_HEAD_DIM), BF16)
        def one_group(g, carry):
            r0 = pl.multiple_of(g * group, group)
            for h in range(ATT_HEADS):
                sl = slice(h * ATT_HEAD_DIM, (h + 1) * ATT_HEAD_DIM)
                q = q_ref[0, pl.ds(r0, group), sl].astype(BF16)
                k = kw_ref[pl.ds(r0, keys), sl]
                s = _nt_dot(q, k) + bias_ref[h]
                if mask_prev:
                    s = jnp.where(col + r0 < ATT_WINDOW, NEG_BIG, s)
                s_ref[h] = s
                m_ref[h] = jnp.max(s, axis=-1, keepdims=True)
            for h in range(ATT_HEADS):
                sl = slice(h * ATT_HEAD_DIM, (h + 1) * ATT_HEAD_DIM)
                v1 = jnp.concatenate([vw_ref[pl.ds(r0, keys), sl], ones], axis=1)
                p = jnp.exp2(s_ref[h] - m_ref[h])
                o = jnp.dot(p.astype(BF16), v1, preferred_element_type=F32)
                o = o[:, :ATT_HEAD_DIM] / o[:, ATT_HEAD_DIM:]
                z = az_ref[0, pl.ds(r0, group), sl].astype(F32)
                o_ref[0, pl.ds(r0, group), sl] = (o * _silu(z)).astype(o_ref.dtype)
            return carry

        lax.fori_loop(0, tq // group, one_group, 0)

    if first_prev_invalid:
        first = pl.program_id(1) == 0
        pl.when(first)(functools.partial(attend, True))
        pl.when(jnp.logical_not(first))(functools.partial(attend, False))
    else:
        attend(False)


def _attention(proj3, kprev, vprev, bias, *, tq, group, cache_layer):
    b, t, _ = proj3.shape
    w = BRANCH_WIDTH
    nt = t // tq
    prompt = cache_layer is None
    if prompt:
        assert tq == ATT_WINDOW
        kprev = vprev = proj3
        kprev_spec = pl.BlockSpec((1, ATT_WINDOW, w), lambda bi, i: (bi, jnp.maximum(i - 1, 0), COL_AK))
        vprev_spec = pl.BlockSpec((1, ATT_WINDOW, w), lambda bi, i: (bi, jnp.maximum(i - 1, 0), COL_AV))
    else:
        assert nt == 1 and kprev.shape[2:] == (ATT_WINDOW, ATT_HEADS, ATT_HEAD_DIM)
        depth = kprev.shape[0]
        kprev = kprev.reshape(depth, b, ATT_WINDOW * ATT_HEADS, ATT_HEAD_DIM)
        vprev = vprev.reshape(depth, b, ATT_WINDOW * ATT_HEADS, ATT_HEAD_DIM)
        kprev_spec = pl.BlockSpec((None, None, ATT_WINDOW * ATT_HEADS, ATT_HEAD_DIM),
                                  lambda bi, i: (cache_layer, bi, 0, 0))
        vprev_spec = kprev_spec
    keys = bias.shape[2]
    rows = tq + keys - group
    kernel = functools.partial(_attn_kernel, tq=tq, group=group, first_prev_invalid=prompt, prev_by_head=not prompt)
    return pl.pallas_call(
        kernel,
        grid=(b, nt),
        in_specs=[pl.BlockSpec((1, tq, w), lambda bi, i: (bi, i, COL_AQ)),
                  kprev_spec,
                  pl.BlockSpec((1, tq, w), lambda bi, i: (bi, i, COL_AK)),
                  vprev_spec,
                  pl.BlockSpec((1, tq, w), lambda bi, i: (bi, i, COL_AV)),
                  pl.BlockSpec((1, tq, w), lambda bi, i: (bi, i, COL_AZ)),
                  pl.BlockSpec((ATT_HEADS, group, keys), lambda bi, i: (0, 0, 0))],
        out_specs=pl.BlockSpec((1, tq, w), lambda bi, i: (bi, i, 0)),
        out_shape=jax.ShapeDtypeStruct((b, t, w), BF16),
        scratch_shapes=[pltpu.VMEM((rows, w), BF16), pltpu.VMEM((rows, w), BF16),
                        pltpu.VMEM((ATT_HEADS, group, keys), F32), pltpu.VMEM((ATT_HEADS, group, 1), F32)],
        compiler_params=_params(("parallel", "arbitrary"), 48),
        name="attention",
    )(proj3, kprev, proj3, vprev, proj3, proj3, bias)


def _log_sigmoid(x):
    return jnp.minimum(x, 0.0) - jnp.log1p(jnp.exp(-jnp.abs(x)))


def _split3(x):
    hi = x.astype(BF16)
    rest = x - hi.astype(F32)
    mid = rest.astype(BF16)
    lo = (rest - mid.astype(F32)).astype(BF16)
    return hi, mid, lo


def _mlstm_kernel(q_ref, k_ref, v_ref, o_ref, z_ref, g_ref, gt_ref, gb_ref, gbt_ref, hn_ref, tri_ref,
                  c0_ref, n0_ref, m0_ref, *rest, chunk, rows_from_input):
    y_ref, c_ref, n_ref, m_ref = rest[-4:]
    L = chunk
    D = ML_HEAD_DIM

    @pl.when(pl.program_id(1) == 0)
    def _():
        c_ref[...] = c0_ref[...]
        n_ref[...] = n0_ref[...]
        m_ref[...] = m0_ref[...]

    row = lax.broadcasted_iota(jnp.int32, (L, L), 0)
    col = lax.broadcasted_iota(jnp.int32, (L, L), 1)
    causal = col <= row

    n_gates = 2 * ML_HEADS
    gates = (g_ref[0] + gb_ref[...]) * LOG2_E
    if rows_from_input:
        gates_t = gt_ref[0:n_gates, :] + gbt_ref[0:n_gates, :]
        fill = (jnp.zeros((n_gates, L), BF16),)
        a_rows = jnp.dot(jnp.concatenate(_split3(_log_sigmoid(gates_t)) + fill, axis=0), tri_ref[1],
                         preferred_element_type=F32)
        a_rows = (a_rows[:n_gates] + a_rows[n_gates:2 * n_gates] + a_rows[2 * n_gates:3 * n_gates]) * LOG2_E
        gates_t = gates_t * LOG2_E
        pick = (lax.broadcasted_iota(jnp.int32, (4 * n_gates, LANES), 0) % n_gates
                == lax.broadcasted_iota(jnp.int32, (4 * n_gates, LANES), 1)).astype(BF16)
        a_cols = _tn_dot(jnp.concatenate(_split3(a_rows) + fill, axis=0), pick)
    else:
        logf = _log_sigmoid(g_ref[0] + gb_ref[...])
        a_cols = jnp.dot(tri_ref[0], jnp.concatenate(_split3(logf), axis=1), preferred_element_type=F32)
        a_cols = (a_cols[:, :LANES] + a_cols[:, LANES:2 * LANES] + a_cols[:, 2 * LANES:]) * LOG2_E
        eye = row == col

    ones = jnp.ones((L, LANES), BF16)
    k_scale = jnp.asarray(D ** -0.5, BF16)

    for hd in range(ML_HEADS):
        sl = slice(hd * D, (hd + 1) * D)
        ig_col = gates[:, hd:hd + 1]
        a_col = a_cols[:, ML_HEADS + hd:ML_HEADS + hd + 1]
        if rows_from_input:
            ig_row = gates_t[hd:hd + 1, :]
            a_row = a_rows[ML_HEADS + hd:ML_HEADS + hd + 1, :]
        else:
            ig_row = jnp.sum(jnp.where(eye, ig_col, 0.0), axis=0, keepdims=True)
            a_row = jnp.sum(jnp.where(eye, a_col, 0.0), axis=0, keepdims=True)
        b = a_col[L - 1:L, :]
        m_prev = m_ref[0, hd:hd + 1, 0:1] * LOG2_E
        c_prev = c_ref[0, hd]
        n_prev = n_ref[0, hd:hd + 1, :]

        logd = jnp.where(causal, a_col - a_row + ig_row, NEG_BIG)
        inter = a_col + m_prev
        m_row = jnp.maximum(inter, jnp.max(logd, axis=1, keepdims=True))
        dmat = jnp.exp2(logd - m_row)
        w_inter = jnp.exp2(inter - m_row)

        qb = q_ref[0, :, sl].astype(BF16)
        kb = k_ref[0, :, sl].astype(BF16) * k_scale
        v1 = jnp.concatenate([v_ref[0, :, sl].astype(BF16), ones], axis=1)
        c1 = jnp.concatenate([c_prev.astype(BF16), jnp.broadcast_to(n_prev.astype(BF16), (LANES, D))], axis=0)

        s = _nt_dot(qb, kb) * dmat
        sv = jnp.dot(s.astype(BF16), v1, preferred_element_type=F32)
        qc = _nt_dot(qb, c1)
        num = sv[:, :D] + w_inter * qc[:, :D]
        den = sv[:, D:] + w_inter * qc[:, D:]
        r = 1.0 / jnp.maximum(jnp.abs(den), jnp.exp2(-m_row))
        h = num * jnp.concatenate([r] * (D // LANES), axis=1)

        g_col = b - a_col + ig_col
        m_new = jnp.maximum(b + m_prev, jnp.max(g_col, axis=0, keepdims=True))
        wk = jnp.exp2(g_col - m_new)
        decay = jnp.exp2(b + m_prev - m_new)
        kw = kb * wk.astype(BF16)
        upd = _tn_dot(v1, kw)
        c_ref[0, hd] = decay * c_prev + upd[:D]
        n_ref[0, hd:hd + 1, :] = decay * n_prev + upd[D:D + 1]
        m_ref[0, hd:hd + 1, :] = jnp.broadcast_to(m_new * LN_2, (1, LANES))

        hm = h * _sigmoid(o_ref[0, :, sl].astype(F32))
        hm = hm * lax.rsqrt(jnp.mean(hm * hm, axis=-1, keepdims=True) + EPS) * hn_ref[:, sl]
        y_ref[0, :, sl] = (hm * _silu(z_ref[0, :, sl].astype(F32))).astype(y_ref.dtype)


def _mlstm(proj3, gates3, gates_t, gate_bias, head_norm, c0, n0, m0, layer, state_layer, chunk, c_stack):
    depth = gate_bias.shape[0]
    b, t, _ = proj3.shape
    w = BRANCH_WIDTH
    h, d = ML_HEADS, ML_HEAD_DIM
    nc = t // chunk
    rows_from_input = chunk % LANES == 0
    col = lambda c: pl.BlockSpec((1, chunk, w), lambda bi, i, c=c: (bi, i, c))
    state_in = lambda shape: pl.BlockSpec((None, 1) + shape, lambda bi, i: (state_layer, bi) + (0,) * len(shape))
    state_out = lambda shape: pl.BlockSpec((1,) + shape, lambda bi, i: (bi,) + (0,) * len(shape))
    if rows_from_input:
        gates_t_spec = pl.BlockSpec((LANES, chunk), lambda bi, i: (0, bi * nc + i))
    else:
        gates_t_spec = pl.BlockSpec((LANES, LANES), lambda bi, i: (0, 0))
    lower = jnp.tril(jnp.ones((chunk, chunk), BF16))
    triangles = jnp.stack([lower, lower.T])
    gate_bias_t = jnp.swapaxes(gate_bias, 1, 2)
    kernel = functools.partial(_mlstm_kernel, chunk=chunk, rows_from_input=rows_from_input)
    in_specs = [col(COL_MQ), col(COL_MK), col(COL_MV), col(COL_MO), col(COL_MZ),
                pl.BlockSpec((1, chunk, LANES), lambda bi, i: (bi, i, 0)),
                gates_t_spec,
                _layer_spec((1, LANES), layer),
                _layer_spec((LANES, 1), layer),
                _layer_spec((1, w), layer),
                pl.BlockSpec((2, chunk, chunk), lambda bi, i: (0, 0, 0)),
                state_in((h, d, d)), state_in((h, d)), state_in((h, LANES))]
    args = [proj3, proj3, proj3, proj3, proj3, gates3, gates_t, gate_bias, gate_bias_t, head_norm, triangles,
            c0, n0, m0]
    aliases = {}
    if c_stack is not None:
        aliases = {len(args): 1}
        in_specs.append(pl.BlockSpec(memory_space=pl.ANY))
        args.append(c_stack)
    return pl.pallas_call(
        kernel,
        grid=(b, nc),
        in_specs=in_specs,
        out_specs=[pl.BlockSpec((1, chunk, w), lambda bi, i: (bi, i, 0)),
                   pl.BlockSpec((None, 1, h, d, d), lambda bi, i: (layer, bi, 0, 0, 0)),
                   state_out((h, d)), state_out((h, LANES))],
        out_shape=[jax.ShapeDtypeStruct((b, t, w), BF16),
                   jax.ShapeDtypeStruct((depth, b, h, d, d), F32),
                   jax.ShapeDtypeStruct((b, h, d), F32),
                   jax.ShapeDtypeStruct((b, h, LANES), F32)],
        input_output_aliases=aliases,
        compiler_params=_params(("parallel", "arbitrary"), 48),
        name="mlstm",
    )(*args)


def _pool_prev_rows(hist_ref, halo_ref, first):
    hist = hist_ref[0].astype(F32)
    hist = jnp.concatenate([jnp.zeros((POOL_HALO - POOL_BUF, hist.shape[1]), F32), hist], axis=0)
    return jnp.where(first, hist, halo_ref[...].astype(F32))


def _pool_mix_tile(u, prev, z_ref, w_ref, sc_ref, pos):
    outs = []
    for gi, win in enumerate(POOL_WINDOWS):
        sl = slice(gi * POOL_GROUP_DIM, (gi + 1) * POOL_GROUP_DIM)
        ext = jnp.concatenate([prev[:, sl], u[:, sl]], axis=0)
        acc = ext
        span = 1
        while span < win:
            acc = acc + pltpu.roll(acc, span, 0)
            span *= 2
        cnt = jnp.minimum(pos + 1, win).astype(F32)
        mean = acc[POOL_HALO:, :] / cnt
        m = (mean - u[:, sl]).astype(BF16)
        y = jnp.dot(m, w_ref[gi], preferred_element_type=F32) * sc_ref[:, sl]
        outs.append((y * _silu(z_ref[:, sl].astype(F32))).astype(BF16))
    return jnp.concatenate(outs, axis=1)


def _pool_kernel(u_ref, halo_ref, hist_ref, z_ref, w_ref, sc_ref, y_ref, *, tp, pos0):
    i = pl.program_id(1)
    prev = _pool_prev_rows(hist_ref, halo_ref.at[0], i == 0)
    pos = pos0 + i * tp + lax.broadcasted_iota(jnp.int32, (tp, 1), 0)
    y_ref[0] = _pool_mix_tile(u_ref[0].astype(F32), prev, z_ref.at[0], w_ref, sc_ref, pos)


def _pool(proj3, hist, w_group, scale, layer, hist_layer, *, tp, pos0):
    b, t, _ = proj3.shape
    w = BRANCH_WIDTH
    halo_blocks = tp // POOL_HALO
    kernel = functools.partial(_pool_kernel, tp=tp, pos0=pos0)
    return pl.pallas_call(
        kernel,
        grid=(b, t // tp),
        in_specs=[pl.BlockSpec((1, tp, w), lambda bi, i: (bi, i, COL_PU)),
                  pl.BlockSpec((1, POOL_HALO, w), lambda bi, i: (bi, jnp.maximum(i * halo_blocks - 1, 0), COL_PU)),
                  pl.BlockSpec((None, 1, POOL_BUF, w), lambda bi, i: (hist_layer, bi, 0, 0)),
                  pl.BlockSpec((1, tp, w), lambda bi, i: (bi, i, COL_PZ)),
                  _layer_spec((len(POOL_WINDOWS), POOL_GROUP_DIM, POOL_GROUP_DIM), layer),
                  _layer_spec((1, w), layer)],
        out_specs=pl.BlockSpec((1, tp, w), lambda bi, i: (bi, i, 0)),
        out_shape=jax.ShapeDtypeStruct((b, t, w), BF16),
        compiler_params=_params(("parallel", "arbitrary"), 32),
        name="pool",
    )(proj3, proj3, hist, proj3, w_group, scale)


def _merge_kernel(*refs):
    y_refs = refs[:N_BRANCH]
    gate_refs = refs[N_BRANCH:-4]
    x_ref, wb_ref, wo_ref, o_ref = refs[-4:]
    per_branch = len(gate_refs) // N_BRANCH
    merged = None
    for b in range(N_BRANCH):
        gate = jnp.concatenate([_sigmoid(r[...].astype(F32)) for r in gate_refs[b * per_branch:(b + 1) * per_branch]],
                               axis=1)
        term = gate * jnp.dot(y_refs[b][...], wb_ref[b], preferred_element_type=F32)
        merged = term if merged is None else merged + term
    o_ref[...] = x_ref[...] + jnp.dot(merged.astype(BF16), wo_ref[...], preferred_element_type=F32)


def _merge(y_pool, y_ml, y_att, proj, x2, w_branch, w_out, layer, tm):
    n, d = x2.shape
    w = BRANCH_WIDTH
    row = lambda width: pl.BlockSpec((tm, width), lambda i: (i, 0))
    n_gate_blocks = N_BRANCH * (d // w)
    gate_specs = [pl.BlockSpec((tm, w), lambda i, k=k: (i, COL_GATES + k)) for k in range(n_gate_blocks)]
    return pl.pallas_call(
        _merge_kernel,
        grid=(n // tm,),
        in_specs=[row(w)] * N_BRANCH + gate_specs + [row(d), _layer_spec((N_BRANCH, w, d), layer, True),
                                                    _layer_spec((d, d), layer, True)],
        out_specs=row(d),
        out_shape=jax.ShapeDtypeStruct((n, d), F32),
        compiler_params=_params(("parallel",), 56),
        name="merge",
    )(y_pool, y_ml, y_att, *([proj] * n_gate_blocks), x2, w_branch, w_out)


def _ple_kernel(x_ref, p_ref, g_ref, wg_ref, wp_ref, o_ref):
    x = x_ref[...]
    h = (x * lax.rsqrt(jnp.mean(x * x, axis=-1, keepdims=True) + EPS) * g_ref[...]).astype(BF16)
    pg = _sigmoid(jnp.dot(h, wg_ref[...], preferred_element_type=F32))
    pp = jnp.dot(p_ref[...].astype(BF16), wp_ref[...], preferred_element_type=F32)
    o_ref[...] = x + pg * pp


def _ple(x2, p_all, gain, w_gate, w_proj, layer, tm):
    n, d = x2.shape
    dp = p_all.shape[2]
    return pl.pallas_call(
        _ple_kernel,
        grid=(n // tm,),
        in_specs=[pl.BlockSpec((tm, d), lambda i: (i, 0)),
                  pl.BlockSpec((None, tm, dp), lambda i: (layer, i, 0)),
                  _layer_spec((1, d), layer, True), _layer_spec((d, d), layer, True),
                  _layer_spec((dp, d), layer, True)],
        out_specs=pl.BlockSpec((tm, d), lambda i: (i, 0)),
        out_shape=jax.ShapeDtypeStruct((n, d), F32),
        compiler_params=_params(("parallel",), 48),
        name="ple",
    )(x2, p_all, gain, w_gate, w_proj)


def _tile(n, preferred):
    t = min(n, preferred)
    assert n % t == 0, (n, t)
    return t


def _prepare_weights(norm_mix, w_in, w_pool_group, pool_scale, b_ig, b_fg, ml_head_norm, att_q_norm, att_k_norm,
                     att_rel_bias, w_branch, w_out, ple_norm, w_ple_gate, w_ple_proj):
    depth, d, width = w_in.shape
    assert width == W_IN_GATES_END + (COL_GATES - COL_AQ) * BRANCH_WIDTH + N_BRANCH * d
    pad = LANES - W_IN_GATE_COLS
    w_in_t = jnp.swapaxes(w_in, 1, 2)
    w_gate = jnp.pad(w_in_t[:, W_IN_GATES_START:W_IN_GATES_END, :], ((0, 0), (0, pad), (0, 0))).astype(BF16)
    gate_bias = jnp.pad(jnp.concatenate([b_ig, b_fg], axis=1), ((0, 0), (0, pad)))[:, None, :].astype(F32)
    gq = att_q_norm * (ATT_HEAD_DIM ** -0.5 * LOG2_E)
    return dict(
        norm_g=norm_mix[:, None, :], w_main=_repack_w_in_t(w_in_t), w_gate=w_gate, gate_bias=gate_bias,
        w_pool=w_pool_group.astype(BF16), pool_scale=pool_scale[:, None, :], head_norm=ml_head_norm[:, None, :],
        gq=gq[:, None, :], gk=att_k_norm[:, None, :], rel_bias=att_rel_bias,
        w_branch=w_branch.astype(BF16), w_out=w_out.astype(BF16), ple_norm=ple_norm[:, None, :],
        w_ple_gate=w_ple_gate.astype(BF16), w_ple_proj=w_ple_proj.astype(BF16))


def _layer(x3, p_all, hist, c0, n0, m0, kv_cache, state_layer, pos0, layer, lw, c_stack):
    b, t, d = x3.shape
    n = b * t
    w = BRANCH_WIDTH
    prompt = kv_cache is None
    assert t % CHUNK == 0 and t >= POOL_BUF

    proj, gates, gates_t = _inproj(x3.reshape(n, d), lw["norm_g"], lw["w_main"], lw["w_gate"], lw["gq"], lw["gk"],
                                   layer, _tile(n, 1024))
    proj3 = proj.reshape(b, t, -1)

    if prompt:
        tq, group = _tile(t, ATT_WINDOW), 2 * CHUNK
        kprev, vprev, cache_layer = None, None, None
    else:
        tq, group = t, CHUNK
        assert t == CHUNK
        kprev, vprev = kv_cache
        cache_layer = state_layer
    bias = _attn_bias(lw["rel_bias"][layer], group)
    y_att = _attention(proj3, kprev, vprev, bias, tq=tq, group=group, cache_layer=cache_layer)

    m0b = jnp.broadcast_to(m0[..., None], m0.shape + (LANES,))
    y_ml, c1, n1, m1 = _mlstm(proj3, gates.reshape(b, t, LANES), gates_t, lw["gate_bias"], lw["head_norm"],
                              c0, n0, m0b, layer, state_layer, _tile(t, 256), c_stack)

    y_pool = _pool(proj3, hist, lw["w_pool"], lw["pool_scale"], layer, state_layer, tp=_tile(t, 512), pos0=pos0)

    x1 = _merge(y_pool.reshape(n, w), y_ml.reshape(n, w), y_att.reshape(n, w), proj, x3.reshape(n, d),
                lw["w_branch"], lw["w_out"], layer, _tile(n, 256))
    x2 = _ple(x1, p_all.reshape(p_all.shape[0], n, -1), lw["ple_norm"], lw["w_ple_gate"], lw["w_ple_proj"],
              layer, _tile(n, 512))

    keep = min(ATT_WINDOW, t) if prompt else t
    new_pool = proj3[:, t - POOL_BUF:, COL_PU * w:(COL_PU + 1) * w].astype(F32)
    new_k = proj3[:, t - keep:, COL_AK * w:(COL_AK + 1) * w].astype(F32).reshape(b, keep, ATT_HEADS, ATT_HEAD_DIM)
    new_v = proj3[:, t - keep:, COL_AV * w:(COL_AV + 1) * w].astype(F32).reshape(b, keep, ATT_HEADS, ATT_HEAD_DIM)
    return x2.reshape(b, t, d), (new_pool, c1, n1, m1[:, :, 0], new_k, new_v)


def kernel(x_prompt, x_sample, cache_att_k, cache_att_v, state_pool, state_mlstm_c, state_mlstm_n, state_mlstm_m, p_prompt, p_sample, norm_mix, w_in, w_pool_group, pool_scale, b_ig, b_fg, ml_head_norm, att_q_norm, att_k_norm, att_rel_bias, w_branch, w_out, ple_norm, w_ple_gate, w_ple_proj):
    xp, xs = x_prompt, x_sample
    bp = x_prompt.shape[0]
    depth = w_in.shape[0]
    lw = _prepare_weights(norm_mix, w_in, w_pool_group, pool_scale, b_ig, b_fg, ml_head_norm, att_q_norm,
                          att_k_norm, att_rel_bias, w_branch, w_out, ple_norm, w_ple_gate, w_ple_proj)
    hist0 = jnp.zeros((1, bp, POOL_BUF, BRANCH_WIDTH), F32)
    c0 = jnp.zeros((1, bp, ML_HEADS, ML_HEAD_DIM, ML_HEAD_DIM), F32)
    n0 = jnp.zeros((1, bp, ML_HEADS, ML_HEAD_DIM), F32)
    m0 = jnp.zeros((1, bp, ML_HEADS), F32)
    sp = [[] for _ in range(6)]
    ss = [[] for _ in range(6)]
    c_p = c_s = None
    for i in range(depth):
        xp, st_p = _layer(xp, p_prompt, hist0, c0, n0, m0, None, 0, 0, i, lw, c_p)
        xs, st_s = _layer(xs, p_sample, state_pool, state_mlstm_c, state_mlstm_n, state_mlstm_m,
                          (cache_att_k, cache_att_v), i, PAST_LEN, i, lw, c_s)
        c_p, c_s = st_p[1], st_s[1]
        for j in range(6):
            sp[j].append(st_p[j])
            ss[j].append(st_s[j])
    pool_p, _, n_p, m_p, k_p, v_p = [jnp.stack(a) if j != 1 else None for j, a in enumerate(sp)]
    pool_s, _, n_s, m_s, k_s, v_s = [jnp.stack(a) if j != 1 else None for j, a in enumerate(ss)]
    return (xp, xs, pool_p, pool_s, c_p, c_s, n_p, n_s, m_p, m_s, k_p, k_s, v_p, v_s)
```

```python
import functools

import numpy as np
import jax
import jax.numpy as jnp
from jax import lax
from jax.experimental import pallas as pl
from jax.experimental.pallas import tpu as pltpu

F32 = jnp.float32
BF16 = jnp.bfloat16

EPS = 1e-6
CHUNK = 64
PAST_LEN = 1024

POOL_WINDOWS = (2, 4, 8, 16)
POOL_GROUP_DIM = 256
POOL_BUF = 15
POOL_HALO = 16

ML_HEADS = 4
ML_HEAD_DIM = 256

ATT_HEADS = 8
ATT_HEAD_DIM = 128
ATT_WINDOW = 512
REL_CLIP = 256

BRANCH_WIDTH = 1024
LANES = 128

N_BRANCH = 3
COL_PU, COL_PZ, COL_MQ, COL_MK, COL_MV, COL_MO, COL_MZ, COL_AQ, COL_AK, COL_AV, COL_AZ = range(11)
COL_GATES = 11
W_IN_GATES_START = 7168
W_IN_GATES_END = 7176
W_IN_GATE_COLS = W_IN_GATES_END - W_IN_GATES_START

PROJ_DTYPE = jnp.bfloat16
NEG_BIG = -1e30
LOG2_E = 1.4426950408889634
LN_2 = 0.6931471805599453
MIB = 1024 * 1024


def _params(semantics, vmem_mib):
    return pltpu.CompilerParams(dimension_semantics=semantics, vmem_limit_bytes=vmem_mib * MIB)


def _layer_spec(tail, layer, single_buffer=False):
    index_map = lambda *_: (layer,) + (0,) * len(tail)
    if single_buffer:
        return pl.BlockSpec((None,) + tuple(tail), index_map, pipeline_mode=pl.Buffered(1))
    return pl.BlockSpec((None,) + tuple(tail), index_map)


def _sigmoid(x):
    return 1.0 / (1.0 + jnp.exp(-x))


def _silu(x):
    return x * _sigmoid(x)


def _nt_dot(a, b):
    return lax.dot_general(a, b, (((1,), (1,)), ((), ())), preferred_element_type=F32)


def _tn_dot(a, b):
    return lax.dot_general(a, b, (((0,), (0,)), ((), ())), preferred_element_type=F32)


def _repack_kernel(a_ref, b_ref, o_ref, *, blocks_before):
    j = pl.program_id(1)

    @pl.when(j < blocks_before)
    def _():
        o_ref[...] = a_ref[...].astype(BF16)

    @pl.when(j >= blocks_before)
    def _():
        shifted = jnp.concatenate([a_ref[W_IN_GATE_COLS:, :], b_ref[...]], axis=0)
        o_ref[...] = shifted.astype(BF16)


def _repack_w_in_t(w_in_t):
    depth, width, d = w_in_t.shape
    tn = BRANCH_WIDTH
    out_width = width - W_IN_GATE_COLS
    assert out_width % tn == 0 and W_IN_GATES_START % tn == 0
    kernel = functools.partial(_repack_kernel, blocks_before=W_IN_GATES_START // tn)
    return pl.pallas_call(
        kernel,
        grid=(depth, out_width // tn),
        in_specs=[pl.BlockSpec((None, tn, d), lambda l, j: (l, j, 0)),
                  pl.BlockSpec((None, W_IN_GATE_COLS, d), lambda l, j: (l, (j + 1) * (tn // W_IN_GATE_COLS), 0))],
        out_specs=pl.BlockSpec((None, tn, d), lambda l, j: (l, j, 0)),
        out_shape=jax.ShapeDtypeStruct((depth, out_width, d), BF16),
        compiler_params=_params(("parallel", "arbitrary"), 40),
        name="repack_w_in",
    )(w_in_t, w_in_t)


def _inproj_kernel(x_ref, g_ref, w_ref, wg_ref, gq_ref, gk_ref, o_ref, og_ref, ogt_ref, h_ref):
    j = pl.program_id(1)

    @pl.when(j == 0)
    def _():
        x = x_ref[...]
        ms = jnp.mean(x * x, axis=-1, keepdims=True)
        h = (x * lax.rsqrt(ms + EPS) * g_ref[...]).astype(BF16)
        h_ref[...] = h
        og_ref[...] = _nt_dot(h, wg_ref[...])
        ogt_ref[...] = _nt_dot(wg_ref[...], h)

    is_q = j == COL_AQ
    head_normed = jnp.logical_or(is_q, j == COL_AK)

    @pl.when(jnp.logical_not(head_normed))
    def _():
        o_ref[...] = _nt_dot(h_ref[...], w_ref[...]).astype(o_ref.dtype)

    @pl.when(head_normed)
    def _():
        acc = _nt_dot(h_ref[...], w_ref[...])
        gain = jnp.where(is_q, gq_ref[...], gk_ref[...])
        for h in range(ATT_HEADS):
            sl = slice(h * ATT_HEAD_DIM, (h + 1) * ATT_HEAD_DIM)
            a = acc[:, sl]
            r = lax.rsqrt(jnp.mean(a * a, axis=-1, keepdims=True) + EPS)
            o_ref[:, sl] = (a * r * gain).astype(o_ref.dtype)


def _inproj(x2, gain, w_main_t, w_gate_t, gq, gk, layer, tm):
    n, d = x2.shape
    nw = w_main_t.shape[1]
    tn = BRANCH_WIDTH
    assert tn == ATT_HEADS * ATT_HEAD_DIM
    return pl.pallas_call(
        _inproj_kernel,
        grid=(n // tm, nw // tn),
        in_specs=[pl.BlockSpec((tm, d), lambda i, j: (i, 0)),
                  _layer_spec((1, d), layer),
                  pl.BlockSpec((None, tn, d), lambda i, j: (layer, j, 0)),
                  _layer_spec((LANES, d), layer),
                  _layer_spec((1, ATT_HEAD_DIM), layer),
                  _layer_spec((1, ATT_HEAD_DIM), layer)],
        out_specs=[pl.BlockSpec((None, tm, tn), lambda i, j: (j, i, 0)),
                   pl.BlockSpec((tm, LANES), lambda i, j: (i, 0)),
                   pl.BlockSpec((LANES, tm), lambda i, j: (0, i))],
        out_shape=[jax.ShapeDtypeStruct((nw // tn, n, tn), PROJ_DTYPE),
                   jax.ShapeDtypeStruct((n, LANES), F32),
                   jax.ShapeDtypeStruct((LANES, n), F32)],
        scratch_shapes=[pltpu.VMEM((tm, d), BF16)],
        compiler_params=_params(("parallel", "arbitrary"), 48),
        name="inproj",
    )(x2, gain, w_main_t, w_gate_t, gq, gk)


def _att_keys(group):
    return -(-(ATT_WINDOW + group) // LANES) * LANES


def _attn_bias(rel_bias, group):
    keys = _att_keys(group)
    period = keys + group
    m = np.arange(period)
    m = np.where(m <= keys, m, m - period)
    idx = np.clip(ATT_WINDOW - m, -REL_CLIP, REL_CLIP) + REL_CLIP
    vec = rel_bias[:, idx].astype(F32)
    heads = rel_bias.shape[0]
    bias = jnp.tile(vec, (1, group))[:, :group * (period - 1)].reshape(heads, group, period - 1)[:, :, :keys]
    i = np.arange(group)[:, None]
    j = np.arange(keys)[None, :]
    lo = (i // CHUNK) * CHUNK
    band = (j >= lo) & (j < lo + ATT_WINDOW + CHUNK)
    return jnp.where(jnp.asarray(band)[None], bias * LOG2_E, NEG_BIG)


def _attn_kernel(q_ref, kp_ref, kc_ref, vp_ref, vc_ref, az_ref, bias_ref, o_ref, kw_ref, vw_ref, s_ref, m_ref,
                 *, tq, group, first_prev_invalid, prev_by_head):
    rows = kw_ref.shape[0]
    keys = bias_ref.shape[2]
    if prev_by_head:
        for h in range(ATT_HEADS):
            sl = slice(h * ATT_HEAD_DIM, (h + 1) * ATT_HEAD_DIM)
            kw_ref[0:ATT_WINDOW, sl] = kp_ref[pl.ds(h, ATT_WINDOW, stride=ATT_HEADS), :].astype(BF16)
            vw_ref[0:ATT_WINDOW, sl] = vp_ref[pl.ds(h, ATT_WINDOW, stride=ATT_HEADS), :].astype(BF16)
    else:
        kw_ref[0:ATT_WINDOW, :] = kp_ref[0].astype(BF16)
        vw_ref[0:ATT_WINDOW, :] = vp_ref[0].astype(BF16)
    kw_ref[ATT_WINDOW:ATT_WINDOW + tq, :] = kc_ref[0].astype(BF16)
    vw_ref[ATT_WINDOW:ATT_WINDOW + tq, :] = vc_ref[0].astype(BF16)
    if rows > ATT_WINDOW + tq:
        pad = jnp.zeros((rows - ATT_WINDOW - tq, kw_ref.shape[1]), BF16)
        kw_ref[ATT_WINDOW + tq:rows, :] = pad
        vw_ref[ATT_WINDOW + tq:rows, :] = pad

    def attend(mask_prev):
        col = lax.broadcasted_iota(jnp.int32, (group, keys), 1)
        ones = jnp.ones((keys, ATT_HEAD_DIM), BF16)
        for g in range(tq // group):
            r0 = g * group
            for h in range(ATT_HEADS):
                sl = slice(h * ATT_HEAD_DIM, (h + 1) * ATT_HEAD_DIM)
                q = q_ref[0, r0:r0 + group, sl].astype(BF16)
                k = kw_ref[r0:r0 + keys, sl]
                s = _nt_dot(q, k) + bias_ref[h]
                if mask_prev:
                    s = jnp.where(col + r0 < ATT_WINDOW, NEG_BIG, s)
                s_ref[h] = s
                m_ref[h] = jnp.max(s, axis=-1, keepdims=True)
            for h in range(ATT_HEADS):
                sl = slice(h * ATT_HEAD_DIM, (h + 1) * ATT_HEAD_DIM)
                v1 = jnp.concatenate([vw_ref[r0:r0 + keys, sl], ones], axis=1)
                p = jnp.exp2(s_ref[h] - m_ref[h])
                o = jnp.dot(p.astype(BF16), v1, preferred_element_type=F32)
                o = o[:, :ATT_HEAD_DIM] / o[:, ATT_HEAD_DIM:]
                z = az_ref[0, r0:r0 + group, sl].astype(F32)
                o_ref[0, r0:r0 + group, sl] = (o * _silu(z)).astype(o_ref.dtype)

    if first_prev_invalid:
        first = pl.program_id(1) == 0
        pl.when(first)(functools.partial(attend, True))
        pl.when(jnp.logical_not(first))(functools.partial(attend, False))
    else:
        attend(False)


def _attention(proj4, kprev, vprev, bias, *, tq, group, cache_layer):
    _, b, t, w = proj4.shape
    nt = t // tq
    prompt = cache_layer is None
    if prompt:
        assert tq == ATT_WINDOW
        kprev = vprev = proj4
        kprev_spec = pl.BlockSpec((None, 1, ATT_WINDOW, w), lambda bi, i: (COL_AK, bi, jnp.maximum(i - 1, 0), 0))
        vprev_spec = pl.BlockSpec((None, 1, ATT_WINDOW, w), lambda bi, i: (COL_AV, bi, jnp.maximum(i - 1, 0), 0))
    else:
        assert nt == 1 and kprev.shape[2:] == (ATT_WINDOW, ATT_HEADS, ATT_HEAD_DIM)
        depth = kprev.shape[0]
        kprev = kprev.reshape(depth, b, ATT_WINDOW * ATT_HEADS, ATT_HEAD_DIM)
        vprev = vprev.reshape(depth, b, ATT_WINDOW * ATT_HEADS, ATT_HEAD_DIM)
        kprev_spec = pl.BlockSpec((None, None, ATT_WINDOW * ATT_HEADS, ATT_HEAD_DIM),
                                  lambda bi, i: (cache_layer, bi, 0, 0))
        vprev_spec = kprev_spec
    keys = bias.shape[2]
    rows = tq + keys - group
    cur = lambda c: pl.BlockSpec((None, 1, tq, w), lambda bi, i: (c, bi, i, 0))
    kernel = functools.partial(_attn_kernel, tq=tq, group=group, first_prev_invalid=prompt, prev_by_head=not prompt)
    return pl.pallas_call(
        kernel,
        grid=(b, nt),
        in_specs=[cur(COL_AQ), kprev_spec, cur(COL_AK), vprev_spec, cur(COL_AV), cur(COL_AZ),
                  pl.BlockSpec((ATT_HEADS, group, keys), lambda bi, i: (0, 0, 0))],
        out_specs=pl.BlockSpec((1, tq, w), lambda bi, i: (bi, i, 0)),
        out_shape=jax.ShapeDtypeStruct((b, t, w), BF16),
        scratch_shapes=[pltpu.VMEM((rows, w), BF16), pltpu.VMEM((rows, w), BF16),
                        pltpu.VMEM((ATT_HEADS, group, keys), F32), pltpu.VMEM((ATT_HEADS, group, 1), F32)],
        compiler_params=_params(("parallel", "arbitrary"), 48),
        name="attention",
    )(proj4, kprev, proj4, vprev, proj4, proj4, bias)


def _log_sigmoid(x):
    return jnp.minimum(x, 0.0) - jnp.log1p(jnp.exp(-jnp.abs(x)))


def _split3(x):
    hi = x.astype(BF16)
    rest = x - hi.astype(F32)
    mid = rest.astype(BF16)
    lo = (rest - mid.astype(F32)).astype(BF16)
    return hi, mid, lo


def _mlstm_kernel(q_ref, k_ref, v_ref, o_ref, z_ref, g_ref, gt_ref, gb_ref, gbt_ref, hn_ref, tri_ref,
                  c0_ref, n0_ref, m0_ref, *rest, chunk, rows_from_input):
    y_ref, c_ref, n_ref, m_ref = rest[-4:]
    L = chunk
    D = ML_HEAD_DIM

    @pl.when(pl.program_id(1) == 0)
    def _():
        c_ref[...] = c0_ref[...]
        n_ref[...] = n0_ref[...]
        m_ref[...] = m0_ref[...]

    row = lax.broadcasted_iota(jnp.int32, (L, L), 0)
    col = lax.broadcasted_iota(jnp.int32, (L, L), 1)
    causal = col <= row

    n_gates = 2 * ML_HEADS
    gates = (g_ref[0] + gb_ref[...]) * LOG2_E
    if rows_from_input:
        gates_t = gt_ref[0:n_gates, :] + gbt_ref[0:n_gates, :]
        fill = (jnp.zeros((n_gates, L), BF16),)
        a_rows = jnp.dot(jnp.concatenate(_split3(_log_sigmoid(gates_t)) + fill, axis=0), tri_ref[1],
                         preferred_element_type=F32)
        a_rows = (a_rows[:n_gates] + a_rows[n_gates:2 * n_gates] + a_rows[2 * n_gates:3 * n_gates]) * LOG2_E
        gates_t = gates_t * LOG2_E
        pick = (lax.broadcasted_iota(jnp.int32, (4 * n_gates, LANES), 0) % n_gates
                == lax.broadcasted_iota(jnp.int32, (4 * n_gates, LANES), 1)).astype(BF16)
        a_cols = _tn_dot(jnp.concatenate(_split3(a_rows) + fill, axis=0), pick)
    else:
        logf = _log_sigmoid(g_ref[0] + gb_ref[...])
        a_cols = jnp.dot(tri_ref[0], jnp.concatenate(_split3(logf), axis=1), preferred_element_type=F32)
        a_cols = (a_cols[:, :LANES] + a_cols[:, LANES:2 * LANES] + a_cols[:, 2 * LANES:]) * LOG2_E
        eye = row == col

    ones = jnp.ones((L, LANES), BF16)
    k_scale = jnp.asarray(D ** -0.5, BF16)

    for hd in range(ML_HEADS):
        sl = slice(hd * D, (hd + 1) * D)
        ig_col = gates[:, hd:hd + 1]
        a_col = a_cols[:, ML_HEADS + hd:ML_HEADS + hd + 1]
        if rows_from_input:
            ig_row = gates_t[hd:hd + 1, :]
            a_row = a_rows[ML_HEADS + hd:ML_HEADS + hd + 1, :]
        else:
            ig_row = jnp.sum(jnp.where(eye, ig_col, 0.0), axis=0, keepdims=True)
            a_row = jnp.sum(jnp.where(eye, a_col, 0.0), axis=0, keepdims=True)
        b = a_col[L - 1:L, :]
        m_prev = m_ref[0, hd:hd + 1, 0:1] * LOG2_E
        c_prev = c_ref[0, hd]
        n_prev = n_ref[0, hd:hd + 1, :]

        logd = jnp.where(causal, a_col - a_row + ig_row, NEG_BIG)
        inter = a_col + m_prev
        m_row = jnp.maximum(inter, jnp.max(logd, axis=1, keepdims=True))
        dmat = jnp.exp2(logd - m_row)
        w_inter = jnp.exp2(inter - m_row)

        qb = q_ref[0, :, sl].astype(BF16)
        kb = k_ref[0, :, sl].astype(BF16) * k_scale
        v1 = jnp.concatenate([v_ref[0, :, sl].astype(BF16), ones], axis=1)
        c1 = jnp.concatenate([c_prev.astype(BF16), jnp.broadcast_to(n_prev.astype(BF16), (LANES, D))], axis=0)

        s = _nt_dot(qb, kb) * dmat
        sv = jnp.dot(s.astype(BF16), v1, preferred_element_type=F32)
        qc = _nt_dot(qb, c1)
        num = sv[:, :D] + w_inter * qc[:, :D]
        den = sv[:, D:] + w_inter * qc[:, D:]
        r = 1.0 / jnp.maximum(jnp.abs(den), jnp.exp2(-m_row))
        h = num * jnp.concatenate([r] * (D // LANES), axis=1)

        g_col = b - a_col + ig_col
        m_new = jnp.maximum(b + m_prev, jnp.max(g_col, axis=0, keepdims=True))
        wk = jnp.exp2(g_col - m_new)
        decay = jnp.exp2(b + m_prev - m_new)
        kw = kb * wk.astype(BF16)
        upd = _tn_dot(v1, kw)
        c_ref[0, hd] = decay * c_prev + upd[:D]
        n_ref[0, hd:hd + 1, :] = decay * n_prev + upd[D:D + 1]
        m_ref[0, hd:hd + 1, :] = jnp.broadcast_to(m_new * LN_2, (1, LANES))

        hm = h * _sigmoid(o_ref[0, :, sl].astype(F32))
        hm = hm * lax.rsqrt(jnp.mean(hm * hm, axis=-1, keepdims=True) + EPS) * hn_ref[:, sl]
        y_ref[0, :, sl] = (hm * _silu(z_ref[0, :, sl].astype(F32))).astype(y_ref.dtype)


def _mlstm(proj4, gates3, gates_t, gate_bias, head_norm, c0, n0, m0, layer, state_layer, chunk, c_stack):
    depth = gate_bias.shape[0]
    _, b, t, w = proj4.shape
    h, d = ML_HEADS, ML_HEAD_DIM
    nc = t // chunk
    rows_from_input = chunk % LANES == 0
    col = lambda c: pl.BlockSpec((None, 1, chunk, w), lambda bi, i: (c, bi, i, 0))
    state_in = lambda shape: pl.BlockSpec((None, 1) + shape, lambda bi, i: (state_layer, bi) + (0,) * len(shape))
    state_out = lambda shape: pl.BlockSpec((1,) + shape, lambda bi, i: (bi,) + (0,) * len(shape))
    if rows_from_input:
        gates_t_spec = pl.BlockSpec((LANES, chunk), lambda bi, i: (0, bi * nc + i))
    else:
        gates_t_spec = pl.BlockSpec((LANES, LANES), lambda bi, i: (0, 0))
    lower = jnp.tril(jnp.ones((chunk, chunk), BF16))
    triangles = jnp.stack([lower, lower.T])
    gate_bias_t = jnp.swapaxes(gate_bias, 1, 2)
    kernel = functools.partial(_mlstm_kernel, chunk=chunk, rows_from_input=rows_from_input)
    in_specs = [col(COL_MQ), col(COL_MK), col(COL_MV), col(COL_MO), col(COL_MZ),
                pl.BlockSpec((1, chunk, LANES), lambda bi, i: (bi, i, 0)),
                gates_t_spec,
                _layer_spec((1, LANES), layer),
                _layer_spec((LANES, 1), layer),
                _layer_spec((1, w), layer),
                pl.BlockSpec((2, chunk, chunk), lambda bi, i: (0, 0, 0)),
                state_in((h, d, d)), state_in((h, d)), state_in((h, LANES))]
    args = [proj4, proj4, proj4, proj4, proj4, gates3, gates_t, gate_bias, gate_bias_t, head_norm, triangles,
            c0, n0, m0]
    aliases = {}
    if c_stack is not None:
        aliases = {len(args): 1}
        in_specs.append(pl.BlockSpec(memory_space=pl.ANY))
        args.append(c_stack)
    return pl.pallas_call(
        kernel,
        grid=(b, nc),
        in_specs=in_specs,
        out_specs=[pl.BlockSpec((1, chunk, w), lambda bi, i: (bi, i, 0)),
                   pl.BlockSpec((None, 1, h, d, d), lambda bi, i: (layer, bi, 0, 0, 0)),
                   state_out((h, d)), state_out((h, LANES))],
        out_shape=[jax.ShapeDtypeStruct((b, t, w), BF16),
                   jax.ShapeDtypeStruct((depth, b, h, d, d), F32),
                   jax.ShapeDtypeStruct((b, h, d), F32),
                   jax.ShapeDtypeStruct((b, h, LANES), F32)],
        input_output_aliases=aliases,
        compiler_params=_params(("parallel", "arbitrary"), 48),
        name="mlstm",
    )(*args)


def _pool_prev_rows(hist_ref, halo_ref, first):
    hist = hist_ref[0].astype(F32)
    hist = jnp.concatenate([jnp.zeros((POOL_HALO - POOL_BUF, hist.shape[1]), F32), hist], axis=0)
    return jnp.where(first, hist, halo_ref[...].astype(F32))


def _pool_mix_tile(u, prev, z_ref, w_ref, sc_ref, pos):
    outs = []
    for gi, win in enumerate(POOL_WINDOWS):
        sl = slice(gi * POOL_GROUP_DIM, (gi + 1) * POOL_GROUP_DIM)
        ext = jnp.concatenate([prev[:, sl], u[:, sl]], axis=0)
        acc = ext
        span = 1
        while span < win:
            acc = acc + pltpu.roll(acc, span, 0)
            span *= 2
        cnt = jnp.minimum(pos + 1, win).astype(F32)
        mean = acc[POOL_HALO:, :] / cnt
        m = (mean - u[:, sl]).astype(BF16)
        y = jnp.dot(m, w_ref[gi], preferred_element_type=F32) * sc_ref[:, sl]
        outs.append((y * _silu(z_ref[:, sl].astype(F32))).astype(BF16))
    return jnp.concatenate(outs, axis=1)


def _pool_kernel(u_ref, halo_ref, hist_ref, z_ref, w_ref, sc_ref, y_ref, *, tp, pos0):
    i = pl.program_id(1)
    prev = _pool_prev_rows(hist_ref, halo_ref.at[0], i == 0)
    pos = pos0 + i * tp + lax.broadcasted_iota(jnp.int32, (tp, 1), 0)
    y_ref[0] = _pool_mix_tile(u_ref[0].astype(F32), prev, z_ref.at[0], w_ref, sc_ref, pos)


def _pool(proj4, hist, w_group, scale, layer, hist_layer, *, tp, pos0):
    _, b, t, w = proj4.shape
    halo_blocks = tp // POOL_HALO
    kernel = functools.partial(_pool_kernel, tp=tp, pos0=pos0)
    return pl.pallas_call(
        kernel,
        grid=(b, t // tp),
        in_specs=[pl.BlockSpec((None, 1, tp, w), lambda bi, i: (COL_PU, bi, i, 0)),
                  pl.BlockSpec((None, 1, POOL_HALO, w),
                               lambda bi, i: (COL_PU, bi, jnp.maximum(i * halo_blocks - 1, 0), 0)),
                  pl.BlockSpec((None, 1, POOL_BUF, w), lambda bi, i: (hist_layer, bi, 0, 0)),
                  pl.BlockSpec((None, 1, tp, w), lambda bi, i: (COL_PZ, bi, i, 0)),
                  _layer_spec((len(POOL_WINDOWS), POOL_GROUP_DIM, POOL_GROUP_DIM), layer),
                  _layer_spec((1, w), layer)],
        out_specs=pl.BlockSpec((1, tp, w), lambda bi, i: (bi, i, 0)),
        out_shape=jax.ShapeDtypeStruct((b, t, w), BF16),
        compiler_params=_params(("parallel", "arbitrary"), 32),
        name="pool",
    )(proj4, proj4, hist, proj4, w_group, scale)


def _merge_kernel(*refs):
    y_refs = refs[:N_BRANCH]
    gate_refs = refs[N_BRANCH:-4]
    x_ref, wb_ref, wo_ref, o_ref = refs[-4:]
    per_branch = len(gate_refs) // N_BRANCH
    merged = None
    for b in range(N_BRANCH):
        gate = jnp.concatenate([_sigmoid(r[...].astype(F32)) for r in gate_refs[b * per_branch:(b + 1) * per_branch]],
                               axis=1)
        term = gate * jnp.dot(y_refs[b][...], wb_ref[b], preferred_element_type=F32)
        merged = term if merged is None else merged + term
    o_ref[...] = x_ref[...] + jnp.dot(merged.astype(BF16), wo_ref[...], preferred_element_type=F32)


def _merge(y_pool, y_ml, y_att, proj, x2, w_branch, w_out, layer, tm):
    n, d = x2.shape
    w = BRANCH_WIDTH
    row = lambda width: pl.BlockSpec((tm, width), lambda i: (i, 0))
    n_gate_blocks = N_BRANCH * (d // w)
    gate_specs = [pl.BlockSpec((None, tm, w), lambda i, k=k: (COL_GATES + k, i, 0)) for k in range(n_gate_blocks)]
    return pl.pallas_call(
        _merge_kernel,
        grid=(n // tm,),
        in_specs=[row(w)] * N_BRANCH + gate_specs + [row(d), _layer_spec((N_BRANCH, w, d), layer, True),
                                                    _layer_spec((d, d), layer, True)],
        out_specs=row(d),
        out_shape=jax.ShapeDtypeStruct((n, d), F32),
        compiler_params=_params(("parallel",), 56),
        name="merge",
    )(y_pool, y_ml, y_att, *([proj] * n_gate_blocks), x2, w_branch, w_out)


def _ple_kernel(x_ref, p_ref, g_ref, wg_ref, wp_ref, o_ref):
    x = x_ref[...]
    h = (x * lax.rsqrt(jnp.mean(x * x, axis=-1, keepdims=True) + EPS) * g_ref[...]).astype(BF16)
    pg = _sigmoid(jnp.dot(h, wg_ref[...], preferred_element_type=F32))
    pp = jnp.dot(p_ref[...].astype(BF16), wp_ref[...], preferred_element_type=F32)
    o_ref[...] = x + pg * pp


def _ple(x2, p_all, gain, w_gate, w_proj, layer, tm):
    n, d = x2.shape
    dp = p_all.shape[2]
    return pl.pallas_call(
        _ple_kernel,
        grid=(n // tm,),
        in_specs=[pl.BlockSpec((tm, d), lambda i: (i, 0)),
                  pl.BlockSpec((None, tm, dp), lambda i: (layer, i, 0)),
                  _layer_spec((1, d), layer, True), _layer_spec((d, d), layer, True),
                  _layer_spec((dp, d), layer, True)],
        out_specs=pl.BlockSpec((tm, d), lambda i: (i, 0)),
        out_shape=jax.ShapeDtypeStruct((n, d), F32),
        compiler_params=_params(("parallel",), 48),
        name="ple",
    )(x2, p_all, gain, w_gate, w_proj)


def _tile(n, preferred):
    t = min(n, preferred)
    assert n % t == 0, (n, t)
    return t


def _prepare_weights(norm_mix, w_in, w_pool_group, pool_scale, b_ig, b_fg, ml_head_norm, att_q_norm, att_k_norm,
                     att_rel_bias, w_branch, w_out, ple_norm, w_ple_gate, w_ple_proj):
    depth, d, width = w_in.shape
    assert width == W_IN_GATES_END + (COL_GATES - COL_AQ) * BRANCH_WIDTH + N_BRANCH * d
    pad = LANES - W_IN_GATE_COLS
    w_in_t = jnp.swapaxes(w_in, 1, 2)
    w_gate = jnp.pad(w_in_t[:, W_IN_GATES_START:W_IN_GATES_END, :], ((0, 0), (0, pad), (0, 0))).astype(BF16)
    gate_bias = jnp.pad(jnp.concatenate([b_ig, b_fg], axis=1), ((0, 0), (0, pad)))[:, None, :].astype(F32)
    gq = att_q_norm * (ATT_HEAD_DIM ** -0.5 * LOG2_E)
    return dict(
        norm_g=norm_mix[:, None, :], w_main=_repack_w_in_t(w_in_t), w_gate=w_gate, gate_bias=gate_bias,
        w_pool=w_pool_group.astype(BF16), pool_scale=pool_scale[:, None, :], head_norm=ml_head_norm[:, None, :],
        gq=gq[:, None, :], gk=att_k_norm[:, None, :], rel_bias=att_rel_bias,
        w_branch=w_branch.astype(BF16), w_out=w_out.astype(BF16), ple_norm=ple_norm[:, None, :],
        w_ple_gate=w_ple_gate.astype(BF16), w_ple_proj=w_ple_proj.astype(BF16))


def _layer(x3, p_all, hist, c0, n0, m0, kv_cache, state_layer, pos0, layer, lw, c_stack):
    b, t, d = x3.shape
    n = b * t
    w = BRANCH_WIDTH
    prompt = kv_cache is None
    assert t % CHUNK == 0 and t >= POOL_BUF

    proj, gates, gates_t = _inproj(x3.reshape(n, d), lw["norm_g"], lw["w_main"], lw["w_gate"], lw["gq"], lw["gk"],
                                   layer, _tile(n, 1024))
    proj4 = proj.reshape(-1, b, t, w)

    if prompt:
        tq, group = _tile(t, ATT_WINDOW), 2 * CHUNK
        kprev, vprev, cache_layer = None, None, None
    else:
        tq, group = t, CHUNK
        assert t == CHUNK
        kprev, vprev = kv_cache
        cache_layer = state_layer
    bias = _attn_bias(lw["rel_bias"][layer], group)
    y_att = _attention(proj4, kprev, vprev, bias, tq=tq, group=group, cache_layer=cache_layer)

    m0b = jnp.broadcast_to(m0[..., None], m0.shape + (LANES,))
    y_ml, c1, n1, m1 = _mlstm(proj4, gates.reshape(b, t, LANES), gates_t, lw["gate_bias"], lw["head_norm"],
                              c0, n0, m0b, layer, state_layer, _tile(t, 256), c_stack)

    y_pool = _pool(proj4, hist, lw["w_pool"], lw["pool_scale"], layer, state_layer, tp=_tile(t, 512), pos0=pos0)

    x1 = _merge(y_pool.reshape(n, w), y_ml.reshape(n, w), y_att.reshape(n, w), proj, x3.reshape(n, d),
                lw["w_branch"], lw["w_out"], layer, _tile(n, 256))
    x2 = _ple(x1, p_all.reshape(p_all.shape[0], n, -1), lw["ple_norm"], lw["w_ple_gate"], lw["w_ple_proj"],
              layer, _tile(n, 512))

    keep = min(ATT_WINDOW, t) if prompt else t
    new_pool = proj4[COL_PU, :, t - POOL_BUF:].astype(F32)
    new_k = proj4[COL_AK, :, t - keep:].astype(F32).reshape(b, keep, ATT_HEADS, ATT_HEAD_DIM)
    new_v = proj4[COL_AV, :, t - keep:].astype(F32).reshape(b, keep, ATT_HEADS, ATT_HEAD_DIM)
    return x2.reshape(b, t, d), (new_pool, c1, n1, m1[:, :, 0], new_k, new_v)


def kernel(x_prompt, x_sample, cache_att_k, cache_att_v, state_pool, state_mlstm_c, state_mlstm_n, state_mlstm_m, p_prompt, p_sample, norm_mix, w_in, w_pool_group, pool_scale, b_ig, b_fg, ml_head_norm, att_q_norm, att_k_norm, att_rel_bias, w_branch, w_out, ple_norm, w_ple_gate, w_ple_proj):
    xp, xs = x_prompt, x_sample
    bp = x_prompt.shape[0]
    depth = w_in.shape[0]
    lw = _prepare_weights(norm_mix, w_in, w_pool_group, pool_scale, b_ig, b_fg, ml_head_norm, att_q_norm,
                          att_k_norm, att_rel_bias, w_branch, w_out, ple_norm, w_ple_gate, w_ple_proj)
    hist0 = jnp.zeros((1, bp, POOL_BUF, BRANCH_WIDTH), F32)
    c0 = jnp.zeros((1, bp, ML_HEADS, ML_HEAD_DIM, ML_HEAD_DIM), F32)
    n0 = jnp.zeros((1, bp, ML_HEADS, ML_HEAD_DIM), F32)
    m0 = jnp.zeros((1, bp, ML_HEADS), F32)
    sp = [[] for _ in range(6)]
    ss = [[] for _ in range(6)]
    c_p = c_s = None
    for i in range(depth):
        xp, st_p = _layer(xp, p_prompt, hist0, c0, n0, m0, None, 0, 0, i, lw, c_p)
        xs, st_s = _layer(xs, p_sample, state_pool, state_mlstm_c, state_mlstm_n, state_mlstm_m,
                          (cache_att_k, cache_att_v), i, PAST_LEN, i, lw, c_s)
        c_p, c_s = st_p[1], st_s[1]
        for j in range(6):
            sp[j].append(st_p[j])
            ss[j].append(st_s[j])
    pool_p, _, n_p, m_p, k_p, v_p = [jnp.stack(a) if j != 1 else None for j, a in enumerate(sp)]
    pool_s, _, n_s, m_s, k_s, v_s = [jnp.stack(a) if j != 1 else None for j, a in enumerate(ss)]
    return (xp, xs, pool_p, pool_s, c_p, c_s, n_p, n_s, m_p, m_s, k_p, k_s, v_p, v_s)
```

```python
import functools

import numpy as np
import jax
import jax.numpy as jnp
from jax import lax
from jax.experimental import pallas as pl
from jax.experimental.pallas import tpu as pltpu

F32 = jnp.float32
BF16 = jnp.bfloat16

EPS = 1e-6
CHUNK = 64
PAST_LEN = 1024

POOL_WINDOWS = (2, 4, 8, 16)
POOL_GROUP_DIM = 256
POOL_BUF = 15
POOL_HALO = 16

ML_HEADS = 4
ML_HEAD_DIM = 256

ATT_HEADS = 8
ATT_HEAD_DIM = 128
ATT_WINDOW = 512
REL_CLIP = 256

BRANCH_WIDTH = 1024
LANES = 128

N_BRANCH = 3
COL_PU, COL_PZ, COL_MQ, COL_MK, COL_MV, COL_MO, COL_MZ, COL_AQ, COL_AK, COL_AV, COL_AZ = range(11)
COL_GATES = 11
W_IN_GATES_START = 7168
W_IN_GATES_END = 7176
W_IN_GATE_COLS = W_IN_GATES_END - W_IN_GATES_START

PROJ_DTYPE = jnp.bfloat16
NEG_BIG = -1e30
LOG2_E = 1.4426950408889634
LN_2 = 0.6931471805599453
MIB = 1024 * 1024


def _params(semantics, vmem_mib):
    return pltpu.CompilerParams(dimension_semantics=semantics, vmem_limit_bytes=vmem_mib * MIB)


def _layer_spec(tail, layer, single_buffer=False):
    index_map = lambda *_: (layer,) + (0,) * len(tail)
    if single_buffer:
        return pl.BlockSpec((None,) + tuple(tail), index_map, pipeline_mode=pl.Buffered(1))
    return pl.BlockSpec((None,) + tuple(tail), index_map)


def _sigmoid(x):
    return 1.0 / (1.0 + jnp.exp(-x))


def _silu(x):
    return x * _sigmoid(x)


def _nt_dot(a, b):
    return lax.dot_general(a, b, (((1,), (1,)), ((), ())), preferred_element_type=F32)


def _tn_dot(a, b):
    return lax.dot_general(a, b, (((0,), (0,)), ((), ())), preferred_element_type=F32)


def _repack_kernel(a_ref, b_ref, o_ref, *, blocks_before):
    j = pl.program_id(1)

    @pl.when(j < blocks_before)
    def _():
        o_ref[...] = a_ref[...].astype(BF16)

    @pl.when(j >= blocks_before)
    def _():
        shifted = jnp.concatenate([a_ref[W_IN_GATE_COLS:, :], b_ref[...]], axis=0)
        o_ref[...] = shifted.astype(BF16)


def _repack_w_in_t(w_in_t):
    depth, width, d = w_in_t.shape
    tn = BRANCH_WIDTH
    out_width = width - W_IN_GATE_COLS
    assert out_width % tn == 0 and W_IN_GATES_START % tn == 0
    kernel = functools.partial(_repack_kernel, blocks_before=W_IN_GATES_START // tn)
    return pl.pallas_call(
        kernel,
        grid=(depth, out_width // tn),
        in_specs=[pl.BlockSpec((None, tn, d), lambda l, j: (l, j, 0)),
                  pl.BlockSpec((None, W_IN_GATE_COLS, d), lambda l, j: (l, (j + 1) * (tn // W_IN_GATE_COLS), 0))],
        out_specs=pl.BlockSpec((None, tn, d), lambda l, j: (l, j, 0)),
        out_shape=jax.ShapeDtypeStruct((depth, out_width, d), BF16),
        compiler_params=_params(("parallel", "arbitrary"), 40),
        name="repack_w_in",
    )(w_in_t, w_in_t)


def _inproj_kernel(x_ref, g_ref, w_ref, wg_ref, gq_ref, gk_ref, o_ref, og_ref, ogt_ref, h_ref):
    j = pl.program_id(1)

    @pl.when(j == 0)
    def _():
        x = x_ref[...]
        ms = jnp.mean(x * x, axis=-1, keepdims=True)
        h = (x * lax.rsqrt(ms + EPS) * g_ref[...]).astype(BF16)
        h_ref[...] = h
        og_ref[...] = _nt_dot(h, wg_ref[...])
        ogt_ref[...] = _nt_dot(wg_ref[...], h)

    is_q = j == COL_AQ
    head_normed = jnp.logical_or(is_q, j == COL_AK)

    @pl.when(jnp.logical_not(head_normed))
    def _():
        o_ref[...] = _nt_dot(h_ref[...], w_ref[...]).astype(o_ref.dtype)

    @pl.when(head_normed)
    def _():
        acc = _nt_dot(h_ref[...], w_ref[...])
        gain = jnp.where(is_q, gq_ref[...], gk_ref[...])
        for h in range(ATT_HEADS):
            sl = slice(h * ATT_HEAD_DIM, (h + 1) * ATT_HEAD_DIM)
            a = acc[:, sl]
            r = lax.rsqrt(jnp.mean(a * a, axis=-1, keepdims=True) + EPS)
            o_ref[:, sl] = (a * r * gain).astype(o_ref.dtype)


def _inproj(x2, gain, w_main_t, w_gate_t, gq, gk, layer, tm):
    n, d = x2.shape
    nw = w_main_t.shape[1]
    tn = BRANCH_WIDTH
    assert tn == ATT_HEADS * ATT_HEAD_DIM
    return pl.pallas_call(
        _inproj_kernel,
        grid=(n // tm, nw // tn),
        in_specs=[pl.BlockSpec((tm, d), lambda i, j: (i, 0)),
                  _layer_spec((1, d), layer),
                  pl.BlockSpec((None, tn, d), lambda i, j: (layer, j, 0)),
                  _layer_spec((LANES, d), layer),
                  _layer_spec((1, ATT_HEAD_DIM), layer),
                  _layer_spec((1, ATT_HEAD_DIM), layer)],
        out_specs=[pl.BlockSpec((None, tm, tn), lambda i, j: (j, i, 0)),
                   pl.BlockSpec((tm, LANES), lambda i, j: (i, 0)),
                   pl.BlockSpec((LANES, tm), lambda i, j: (0, i))],
        out_shape=[jax.ShapeDtypeStruct((nw // tn, n, tn), PROJ_DTYPE),
                   jax.ShapeDtypeStruct((n, LANES), F32),
                   jax.ShapeDtypeStruct((LANES, n), F32)],
        scratch_shapes=[pltpu.VMEM((tm, d), BF16)],
        compiler_params=_params(("parallel", "arbitrary"), 48),
        name="inproj",
    )(x2, gain, w_main_t, w_gate_t, gq, gk)


def _att_keys(group):
    return -(-(ATT_WINDOW + group) // LANES) * LANES


def _attn_bias(rel_bias, group):
    keys = _att_keys(group)
    period = keys + group
    m = np.arange(period)
    m = np.where(m <= keys, m, m - period)
    idx = np.clip(ATT_WINDOW - m, -REL_CLIP, REL_CLIP) + REL_CLIP
    vec = rel_bias[:, idx].astype(F32)
    heads = rel_bias.shape[0]
    bias = jnp.tile(vec, (1, group))[:, :group * (period - 1)].reshape(heads, group, period - 1)[:, :, :keys]
    i = np.arange(group)[:, None]
    j = np.arange(keys)[None, :]
    lo = (i // CHUNK) * CHUNK
    band = (j >= lo) & (j < lo + ATT_WINDOW + CHUNK)
    return jnp.where(jnp.asarray(band)[None], bias * LOG2_E, NEG_BIG)


def _attn_kernel(q_ref, kp_ref, kc_ref, vp_ref, vc_ref, az_ref, bias_ref, o_ref, kw_ref, vw_ref, s_ref, m_ref,
                 *, tq, group, first_prev_invalid, prev_by_head):
    rows = kw_ref.shape[0]
    keys = bias_ref.shape[2]
    if prev_by_head:
        for h in range(ATT_HEADS):
            sl = slice(h * ATT_HEAD_DIM, (h + 1) * ATT_HEAD_DIM)
            kw_ref[0:ATT_WINDOW, sl] = kp_ref[pl.ds(h, ATT_WINDOW, stride=ATT_HEADS), :].astype(BF16)
            vw_ref[0:ATT_WINDOW, sl] = vp_ref[pl.ds(h, ATT_WINDOW, stride=ATT_HEADS), :].astype(BF16)
    else:
        kw_ref[0:ATT_WINDOW, :] = kp_ref[0].astype(BF16)
        vw_ref[0:ATT_WINDOW, :] = vp_ref[0].astype(BF16)
    kw_ref[ATT_WINDOW:ATT_WINDOW + tq, :] = kc_ref[0].astype(BF16)
    vw_ref[ATT_WINDOW:ATT_WINDOW + tq, :] = vc_ref[0].astype(BF16)
    if rows > ATT_WINDOW + tq:
        pad = jnp.zeros((rows - ATT_WINDOW - tq, kw_ref.shape[1]), BF16)
        kw_ref[ATT_WINDOW + tq:rows, :] = pad
        vw_ref[ATT_WINDOW + tq:rows, :] = pad

    def attend(mask_prev):
        col = lax.broadcasted_iota(jnp.int32, (group, keys), 1)
        ones = jnp.ones((keys, ATT_HEAD_DIM), BF16)
        for g in range(tq // group):
            r0 = g * group
            for h in range(ATT_HEADS):
                sl = slice(h * ATT_HEAD_DIM, (h + 1) * ATT_HEAD_DIM)
                q = q_ref[0, r0:r0 + group, sl].astype(BF16)
                k = kw_ref[r0:r0 + keys, sl]
                s = _nt_dot(q, k) + bias_ref[h]
                if mask_prev:
                    s = jnp.where(col + r0 < ATT_WINDOW, NEG_BIG, s)
                s_ref[h] = s
                m_ref[h] = jnp.max(s, axis=-1, keepdims=True)
            for h in range(ATT_HEADS):
                sl = slice(h * ATT_HEAD_DIM, (h + 1) * ATT_HEAD_DIM)
                v1 = jnp.concatenate([vw_ref[r0:r0 + keys, sl], ones], axis=1)
                p = jnp.exp2(s_ref[h] - m_ref[h])
                o = jnp.dot(p.astype(BF16), v1, preferred_element_type=F32)
                o = o[:, :ATT_HEAD_DIM] / o[:, ATT_HEAD_DIM:]
                z = az_ref[0, r0:r0 + group, sl].astype(F32)
                o_ref[0, r0:r0 + group, sl] = (o * _silu(z)).astype(o_ref.dtype)

    if first_prev_invalid:
        first = pl.program_id(1) == 0
        pl.when(first)(functools.partial(attend, True))
        pl.when(jnp.logical_not(first))(functools.partial(attend, False))
    else:
        attend(False)


def _attention(proj4, kprev, vprev, bias, *, tq, group, cache_layer):
    _, b, t, w = proj4.shape
    nt = t // tq
    prompt = cache_layer is None
    if prompt:
        assert tq % ATT_WINDOW == 0
        kprev = vprev = proj4
        per_tile = tq // ATT_WINDOW
        prev = lambda c: pl.BlockSpec((None, 1, ATT_WINDOW, w),
                                      lambda bi, i: (c, bi, jnp.maximum(i * per_tile - 1, 0), 0))
        kprev_spec, vprev_spec = prev(COL_AK), prev(COL_AV)
    else:
        assert nt == 1 and kprev.shape[2:] == (ATT_WINDOW, ATT_HEADS, ATT_HEAD_DIM)
        depth = kprev.shape[0]
        kprev = kprev.reshape(depth, b, ATT_WINDOW * ATT_HEADS, ATT_HEAD_DIM)
        vprev = vprev.reshape(depth, b, ATT_WINDOW * ATT_HEADS, ATT_HEAD_DIM)
        kprev_spec = pl.BlockSpec((None, None, ATT_WINDOW * ATT_HEADS, ATT_HEAD_DIM),
                                  lambda bi, i: (cache_layer, bi, 0, 0))
        vprev_spec = kprev_spec
    keys = bias.shape[2]
    rows = tq + keys - group
    cur = lambda c: pl.BlockSpec((None, 1, tq, w), lambda bi, i: (c, bi, i, 0))
    kernel = functools.partial(_attn_kernel, tq=tq, group=group, first_prev_invalid=prompt, prev_by_head=not prompt)
    return pl.pallas_call(
        kernel,
        grid=(b, nt),
        in_specs=[cur(COL_AQ), kprev_spec, cur(COL_AK), vprev_spec, cur(COL_AV), cur(COL_AZ),
                  pl.BlockSpec((ATT_HEADS, group, keys), lambda bi, i: (0, 0, 0))],
        out_specs=pl.BlockSpec((1, tq, w), lambda bi, i: (bi, i, 0)),
        out_shape=jax.ShapeDtypeStruct((b, t, w), BF16),
        scratch_shapes=[pltpu.VMEM((rows, w), BF16), pltpu.VMEM((rows, w), BF16),
                        pltpu.VMEM((ATT_HEADS, group, keys), F32), pltpu.VMEM((ATT_HEADS, group, 1), F32)],
        compiler_params=_params(("parallel", "arbitrary"), 48),
        name="attention",
    )(proj4, kprev, proj4, vprev, proj4, proj4, bias)


def _log_sigmoid(x):
    return jnp.minimum(x, 0.0) - jnp.log1p(jnp.exp(-jnp.abs(x)))


def _split3(x):
    hi = x.astype(BF16)
    rest = x - hi.astype(F32)
    mid = rest.astype(BF16)
    lo = (rest - mid.astype(F32)).astype(BF16)
    return hi, mid, lo


def _mlstm_kernel(q_ref, k_ref, v_ref, o_ref, z_ref, g_ref, gt_ref, gb_ref, gbt_ref, hn_ref, tri_ref,
                  c0_ref, n0_ref, m0_ref, *rest, chunk, rows_from_input):
    y_ref, c_ref, n_ref, m_ref = rest[-4:]
    L = chunk
    D = ML_HEAD_DIM

    @pl.when(pl.program_id(1) == 0)
    def _():
        c_ref[...] = c0_ref[...]
        n_ref[...] = n0_ref[...]
        m_ref[...] = m0_ref[...]

    row = lax.broadcasted_iota(jnp.int32, (L, L), 0)
    col = lax.broadcasted_iota(jnp.int32, (L, L), 1)
    causal = col <= row

    n_gates = 2 * ML_HEADS
    gates = (g_ref[0] + gb_ref[...]) * LOG2_E
    if rows_from_input:
        gates_t = gt_ref[0:n_gates, :] + gbt_ref[0:n_gates, :]
        fill = (jnp.zeros((n_gates, L), BF16),)
        a_rows = jnp.dot(jnp.concatenate(_split3(_log_sigmoid(gates_t)) + fill, axis=0), tri_ref[1],
                         preferred_element_type=F32)
        a_rows = (a_rows[:n_gates] + a_rows[n_gates:2 * n_gates] + a_rows[2 * n_gates:3 * n_gates]) * LOG2_E
        gates_t = gates_t * LOG2_E
        pick = (lax.broadcasted_iota(jnp.int32, (4 * n_gates, LANES), 0) % n_gates
                == lax.broadcasted_iota(jnp.int32, (4 * n_gates, LANES), 1)).astype(BF16)
        a_cols = _tn_dot(jnp.concatenate(_split3(a_rows) + fill, axis=0), pick)
    else:
        logf = _log_sigmoid(g_ref[0] + gb_ref[...])
        a_cols = jnp.dot(tri_ref[0], jnp.concatenate(_split3(logf), axis=1), preferred_element_type=F32)
        a_cols = (a_cols[:, :LANES] + a_cols[:, LANES:2 * LANES] + a_cols[:, 2 * LANES:]) * LOG2_E
        eye = row == col

    ones = jnp.ones((L, LANES), BF16)
    k_scale = jnp.asarray(D ** -0.5, BF16)

    for hd in range(ML_HEADS):
        sl = slice(hd * D, (hd + 1) * D)
        ig_col = gates[:, hd:hd + 1]
        a_col = a_cols[:, ML_HEADS + hd:ML_HEADS + hd + 1]
        if rows_from_input:
            ig_row = gates_t[hd:hd + 1, :]
            a_row = a_rows[ML_HEADS + hd:ML_HEADS + hd + 1, :]
        else:
            ig_row = jnp.sum(jnp.where(eye, ig_col, 0.0), axis=0, keepdims=True)
            a_row = jnp.sum(jnp.where(eye, a_col, 0.0), axis=0, keepdims=True)
        b = a_col[L - 1:L, :]
        m_prev = m_ref[0, hd:hd + 1, 0:1] * LOG2_E
        c_prev = c_ref[0, hd]
        n_prev = n_ref[0, hd:hd + 1, :]

        logd = jnp.where(causal, a_col - a_row + ig_row, NEG_BIG)
        inter = a_col + m_prev
        m_row = jnp.maximum(inter, jnp.max(logd, axis=1, keepdims=True))
        dmat = jnp.exp2(logd - m_row)
        w_inter = jnp.exp2(inter - m_row)

        qb = q_ref[0, :, sl].astype(BF16)
        kb = k_ref[0, :, sl].astype(BF16) * k_scale
        v1 = jnp.concatenate([v_ref[0, :, sl].astype(BF16), ones], axis=1)
        c1 = jnp.concatenate([c_prev.astype(BF16), jnp.broadcast_to(n_prev.astype(BF16), (LANES, D))], axis=0)

        s = _nt_dot(qb, kb) * dmat
        sv = jnp.dot(s.astype(BF16), v1, preferred_element_type=F32)
        qc = _nt_dot(qb, c1)
        num = sv[:, :D] + w_inter * qc[:, :D]
        den = sv[:, D:] + w_inter * qc[:, D:]
        r = 1.0 / jnp.maximum(jnp.abs(den), jnp.exp2(-m_row))
        h = num * jnp.concatenate([r] * (D // LANES), axis=1)

        g_col = b - a_col + ig_col
        m_new = jnp.maximum(b + m_prev, jnp.max(g_col, axis=0, keepdims=True))
        wk = jnp.exp2(g_col - m_new)
        decay = jnp.exp2(b + m_prev - m_new)
        kw = kb * wk.astype(BF16)
        upd = _tn_dot(v1, kw)
        c_ref[0, hd] = decay * c_prev + upd[:D]
        n_ref[0, hd:hd + 1, :] = decay * n_prev + upd[D:D + 1]
        m_ref[0, hd:hd + 1, :] = jnp.broadcast_to(m_new * LN_2, (1, LANES))

        hm = h * _sigmoid(o_ref[0, :, sl].astype(F32))
        hm = hm * lax.rsqrt(jnp.mean(hm * hm, axis=-1, keepdims=True) + EPS) * hn_ref[:, sl]
        y_ref[0, :, sl] = (hm * _silu(z_ref[0, :, sl].astype(F32))).astype(y_ref.dtype)


def _mlstm(proj4, gates3, gates_t, gate_bias, head_norm, c0, n0, m0, layer, state_layer, chunk, c_stack):
    depth = gate_bias.shape[0]
    _, b, t, w = proj4.shape
    h, d = ML_HEADS, ML_HEAD_DIM
    nc = t // chunk
    rows_from_input = chunk % LANES == 0
    col = lambda c: pl.BlockSpec((None, 1, chunk, w), lambda bi, i: (c, bi, i, 0))
    state_in = lambda shape: pl.BlockSpec((None, 1) + shape, lambda bi, i: (state_layer, bi) + (0,) * len(shape))
    state_out = lambda shape: pl.BlockSpec((1,) + shape, lambda bi, i: (bi,) + (0,) * len(shape))
    if rows_from_input:
        gates_t_spec = pl.BlockSpec((LANES, chunk), lambda bi, i: (0, bi * nc + i))
    else:
        gates_t_spec = pl.BlockSpec((LANES, LANES), lambda bi, i: (0, 0))
    lower = jnp.tril(jnp.ones((chunk, chunk), BF16))
    triangles = jnp.stack([lower, lower.T])
    gate_bias_t = jnp.swapaxes(gate_bias, 1, 2)
    kernel = functools.partial(_mlstm_kernel, chunk=chunk, rows_from_input=rows_from_input)
    in_specs = [col(COL_MQ), col(COL_MK), col(COL_MV), col(COL_MO), col(COL_MZ),
                pl.BlockSpec((1, chunk, LANES), lambda bi, i: (bi, i, 0)),
                gates_t_spec,
                _layer_spec((1, LANES), layer),
                _layer_spec((LANES, 1), layer),
                _layer_spec((1, w), layer),
                pl.BlockSpec((2, chunk, chunk), lambda bi, i: (0, 0, 0)),
                state_in((h, d, d)), state_in((h, d)), state_in((h, LANES))]
    args = [proj4, proj4, proj4, proj4, proj4, gates3, gates_t, gate_bias, gate_bias_t, head_norm, triangles,
            c0, n0, m0]
    aliases = {}
    if c_stack is not None:
        aliases = {len(args): 1}
        in_specs.append(pl.BlockSpec(memory_space=pl.ANY))
        args.append(c_stack)
    return pl.pallas_call(
        kernel,
        grid=(b, nc),
        in_specs=in_specs,
        out_specs=[pl.BlockSpec((1, chunk, w), lambda bi, i: (bi, i, 0)),
                   pl.BlockSpec((None, 1, h, d, d), lambda bi, i: (layer, bi, 0, 0, 0)),
                   state_out((h, d)), state_out((h, LANES))],
        out_shape=[jax.ShapeDtypeStruct((b, t, w), BF16),
                   jax.ShapeDtypeStruct((depth, b, h, d, d), F32),
                   jax.ShapeDtypeStruct((b, h, d), F32),
                   jax.ShapeDtypeStruct((b, h, LANES), F32)],
        input_output_aliases=aliases,
        compiler_params=_params(("parallel", "arbitrary"), 48),
        name="mlstm",
    )(*args)


def _pool_prev_rows(hist_ref, halo_ref, first):
    hist = hist_ref[0].astype(F32)
    hist = jnp.concatenate([jnp.zeros((POOL_HALO - POOL_BUF, hist.shape[1]), F32), hist], axis=0)
    return jnp.where(first, hist, halo_ref[...].astype(F32))


def _pool_mix_tile(u, prev, z_ref, w_ref, sc_ref, pos):
    outs = []
    for gi, win in enumerate(POOL_WINDOWS):
        sl = slice(gi * POOL_GROUP_DIM, (gi + 1) * POOL_GROUP_DIM)
        ext = jnp.concatenate([prev[:, sl], u[:, sl]], axis=0)
        acc = ext
        span = 1
        while span < win:
            acc = acc + pltpu.roll(acc, span, 0)
            span *= 2
        cnt = jnp.minimum(pos + 1, win).astype(F32)
        mean = acc[POOL_HALO:, :] / cnt
        m = (mean - u[:, sl]).astype(BF16)
        y = jnp.dot(m, w_ref[gi], preferred_element_type=F32) * sc_ref[:, sl]
        outs.append((y * _silu(z_ref[:, sl].astype(F32))).astype(BF16))
    return jnp.concatenate(outs, axis=1)


def _pool_kernel(u_ref, halo_ref, hist_ref, z_ref, w_ref, sc_ref, y_ref, *, tp, pos0):
    i = pl.program_id(1)
    prev = _pool_prev_rows(hist_ref, halo_ref.at[0], i == 0)
    pos = pos0 + i * tp + lax.broadcasted_iota(jnp.int32, (tp, 1), 0)
    y_ref[0] = _pool_mix_tile(u_ref[0].astype(F32), prev, z_ref.at[0], w_ref, sc_ref, pos)


def _pool(proj4, hist, w_group, scale, layer, hist_layer, *, tp, pos0):
    _, b, t, w = proj4.shape
    halo_blocks = tp // POOL_HALO
    kernel = functools.partial(_pool_kernel, tp=tp, pos0=pos0)
    return pl.pallas_call(
        kernel,
        grid=(b, t // tp),
        in_specs=[pl.BlockSpec((None, 1, tp, w), lambda bi, i: (COL_PU, bi, i, 0)),
                  pl.BlockSpec((None, 1, POOL_HALO, w),
                               lambda bi, i: (COL_PU, bi, jnp.maximum(i * halo_blocks - 1, 0), 0)),
                  pl.BlockSpec((None, 1, POOL_BUF, w), lambda bi, i: (hist_layer, bi, 0, 0)),
                  pl.BlockSpec((None, 1, tp, w), lambda bi, i: (COL_PZ, bi, i, 0)),
                  _layer_spec((len(POOL_WINDOWS), POOL_GROUP_DIM, POOL_GROUP_DIM), layer),
                  _layer_spec((1, w), layer)],
        out_specs=pl.BlockSpec((1, tp, w), lambda bi, i: (bi, i, 0)),
        out_shape=jax.ShapeDtypeStruct((b, t, w), BF16),
        compiler_params=_params(("parallel", "arbitrary"), 32),
        name="pool",
    )(proj4, proj4, hist, proj4, w_group, scale)


def _merge_kernel(*refs):
    y_refs = refs[:N_BRANCH]
    gate_refs = refs[N_BRANCH:-4]
    x_ref, wb_ref, wo_ref, o_ref = refs[-4:]
    per_branch = len(gate_refs) // N_BRANCH
    merged = None
    for b in range(N_BRANCH):
        gate = jnp.concatenate([_sigmoid(r[...].astype(F32)) for r in gate_refs[b * per_branch:(b + 1) * per_branch]],
                               axis=1)
        term = gate * jnp.dot(y_refs[b][...], wb_ref[b], preferred_element_type=F32)
        merged = term if merged is None else merged + term
    o_ref[...] = x_ref[...] + jnp.dot(merged.astype(BF16), wo_ref[...], preferred_element_type=F32)


def _merge(y_pool, y_ml, y_att, proj, x2, w_branch, w_out, layer, tm):
    n, d = x2.shape
    w = BRANCH_WIDTH
    row = lambda width: pl.BlockSpec((tm, width), lambda i: (i, 0))
    n_gate_blocks = N_BRANCH * (d // w)
    gate_specs = [pl.BlockSpec((None, tm, w), lambda i, k=k: (COL_GATES + k, i, 0)) for k in range(n_gate_blocks)]
    return pl.pallas_call(
        _merge_kernel,
        grid=(n // tm,),
        in_specs=[row(w)] * N_BRANCH + gate_specs + [row(d), _layer_spec((N_BRANCH, w, d), layer, True),
                                                    _layer_spec((d, d), layer, True)],
        out_specs=row(d),
        out_shape=jax.ShapeDtypeStruct((n, d), F32),
        compiler_params=_params(("parallel",), 56),
        name="merge",
    )(y_pool, y_ml, y_att, *([proj] * n_gate_blocks), x2, w_branch, w_out)


def _ple_kernel(x_ref, p_ref, g_ref, wg_ref, wp_ref, o_ref):
    x = x_ref[...]
    h = (x * lax.rsqrt(jnp.mean(x * x, axis=-1, keepdims=True) + EPS) * g_ref[...]).astype(BF16)
    pg = _sigmoid(jnp.dot(h, wg_ref[...], preferred_element_type=F32))
    pp = jnp.dot(p_ref[...].astype(BF16), wp_ref[...], preferred_element_type=F32)
    o_ref[...] = x + pg * pp


def _ple(x2, p_all, gain, w_gate, w_proj, layer, tm):
    n, d = x2.shape
    dp = p_all.shape[2]
    return pl.pallas_call(
        _ple_kernel,
        grid=(n // tm,),
        in_specs=[pl.BlockSpec((tm, d), lambda i: (i, 0)),
                  pl.BlockSpec((None, tm, dp), lambda i: (layer, i, 0)),
                  _layer_spec((1, d), layer, True), _layer_spec((d, d), layer, True),
                  _layer_spec((dp, d), layer, True)],
        out_specs=pl.BlockSpec((tm, d), lambda i: (i, 0)),
        out_shape=jax.ShapeDtypeStruct((n, d), F32),
        compiler_params=_params(("parallel",), 48),
        name="ple",
    )(x2, p_all, gain, w_gate, w_proj)


def _tile(n, preferred):
    t = min(n, preferred)
    assert n % t == 0, (n, t)
    return t


def _prepare_weights(norm_mix, w_in, w_pool_group, pool_scale, b_ig, b_fg, ml_head_norm, att_q_norm, att_k_norm,
                     att_rel_bias, w_branch, w_out, ple_norm, w_ple_gate, w_ple_proj):
    depth, d, width = w_in.shape
    assert width == W_IN_GATES_END + (COL_GATES - COL_AQ) * BRANCH_WIDTH + N_BRANCH * d
    pad = LANES - W_IN_GATE_COLS
    w_in_t = jnp.swapaxes(w_in, 1, 2)
    w_gate = jnp.pad(w_in_t[:, W_IN_GATES_START:W_IN_GATES_END, :], ((0, 0), (0, pad), (0, 0))).astype(BF16)
    gate_bias = jnp.pad(jnp.concatenate([b_ig, b_fg], axis=1), ((0, 0), (0, pad)))[:, None, :].astype(F32)
    gq = att_q_norm * (ATT_HEAD_DIM ** -0.5 * LOG2_E)
    return dict(
        norm_g=norm_mix[:, None, :], w_main=_repack_w_in_t(w_in_t), w_gate=w_gate, gate_bias=gate_bias,
        w_pool=w_pool_group.astype(BF16), pool_scale=pool_scale[:, None, :], head_norm=ml_head_norm[:, None, :],
        gq=gq[:, None, :], gk=att_k_norm[:, None, :], rel_bias=att_rel_bias,
        w_branch=w_branch.astype(BF16), w_out=w_out.astype(BF16), ple_norm=ple_norm[:, None, :],
        w_ple_gate=w_ple_gate.astype(BF16), w_ple_proj=w_ple_proj.astype(BF16))


def _layer(x3, p_all, hist, c0, n0, m0, kv_cache, state_layer, pos0, layer, lw, c_stack):
    b, t, d = x3.shape
    n = b * t
    w = BRANCH_WIDTH
    prompt = kv_cache is None
    assert t % CHUNK == 0 and t >= POOL_BUF

    proj, gates, gates_t = _inproj(x3.reshape(n, d), lw["norm_g"], lw["w_main"], lw["w_gate"], lw["gq"], lw["gk"],
                                   layer, _tile(n, 1024))
    proj4 = proj.reshape(-1, b, t, w)

    if prompt:
        tq, group = _tile(t, 2 * ATT_WINDOW), 2 * CHUNK
        kprev, vprev, cache_layer = None, None, None
    else:
        tq, group = t, CHUNK
        assert t == CHUNK
        kprev, vprev = kv_cache
        cache_layer = state_layer
    bias = _attn_bias(lw["rel_bias"][layer], group)
    y_att = _attention(proj4, kprev, vprev, bias, tq=tq, group=group, cache_layer=cache_layer)

    m0b = jnp.broadcast_to(m0[..., None], m0.shape + (LANES,))
    y_ml, c1, n1, m1 = _mlstm(proj4, gates.reshape(b, t, LANES), gates_t, lw["gate_bias"], lw["head_norm"],
                              c0, n0, m0b, layer, state_layer, _tile(t, 256), c_stack)

    y_pool = _pool(proj4, hist, lw["w_pool"], lw["pool_scale"], layer, state_layer, tp=_tile(t, 512), pos0=pos0)

    x1 = _merge(y_pool.reshape(n, w), y_ml.reshape(n, w), y_att.reshape(n, w), proj, x3.reshape(n, d),
                lw["w_branch"], lw["w_out"], layer, _tile(n, 256))
    x2 = _ple(x1, p_all.reshape(p_all.shape[0], n, -1), lw["ple_norm"], lw["w_ple_gate"], lw["w_ple_proj"],
              layer, _tile(n, 512))

    keep = min(ATT_WINDOW, t) if prompt else t
    new_pool = proj4[COL_PU, :, t - POOL_BUF:].astype(F32)
    new_k = proj4[COL_AK, :, t - keep:].astype(F32).reshape(b, keep, ATT_HEADS, ATT_HEAD_DIM)
    new_v = proj4[COL_AV, :, t - keep:].astype(F32).reshape(b, keep, ATT_HEADS, ATT_HEAD_DIM)
    return x2.reshape(b, t, d), (new_pool, c1, n1, m1[:, :, 0], new_k, new_v)


def kernel(x_prompt, x_sample, cache_att_k, cache_att_v, state_pool, state_mlstm_c, state_mlstm_n, state_mlstm_m, p_prompt, p_sample, norm_mix, w_in, w_pool_group, pool_scale, b_ig, b_fg, ml_head_norm, att_q_norm, att_k_norm, att_rel_bias, w_branch, w_out, ple_norm, w_ple_gate, w_ple_proj):
    xp, xs = x_prompt, x_sample
    bp = x_prompt.shape[0]
    depth = w_in.shape[0]
    lw = _prepare_weights(norm_mix, w_in, w_pool_group, pool_scale, b_ig, b_fg, ml_head_norm, att_q_norm,
                          att_k_norm, att_rel_bias, w_branch, w_out, ple_norm, w_ple_gate, w_ple_proj)
    hist0 = jnp.zeros((1, bp, POOL_BUF, BRANCH_WIDTH), F32)
    c0 = jnp.zeros((1, bp, ML_HEADS, ML_HEAD_DIM, ML_HEAD_DIM), F32)
    n0 = jnp.zeros((1, bp, ML_HEADS, ML_HEAD_DIM), F32)
    m0 = jnp.zeros((1, bp, ML_HEADS), F32)
    sp = [[] for _ in range(6)]
    ss = [[] for _ in range(6)]
    c_p = c_s = None
    for i in range(depth):
        xp, st_p = _layer(xp, p_prompt, hist0, c0, n0, m0, None, 0, 0, i, lw, c_p)
        xs, st_s = _layer(xs, p_sample, state_pool, state_mlstm_c, state_mlstm_n, state_mlstm_m,
                          (cache_att_k, cache_att_v), i, PAST_LEN, i, lw, c_s)
        c_p, c_s = st_p[1], st_s[1]
        for j in range(6):
            sp[j].append(st_p[j])
            ss[j].append(st_s[j])
    pool_p, _, n_p, m_p, k_p, v_p = [jnp.stack(a) if j != 1 else None for j, a in enumerate(sp)]
    pool_s, _, n_s, m_s, k_s, v_s = [jnp.stack(a) if j != 1 else None for j, a in enumerate(ss)]
    return (xp, xs, pool_p, pool_s, c_p, c_s, n_p, n_s, m_p, m_s, k_p, k_s, v_p, v_s)
```

```python
import functools

import numpy as np
import jax
import jax.numpy as jnp
from jax import lax
from jax.experimental import pallas as pl
from jax.experimental.pallas import tpu as pltpu

F32 = jnp.float32
BF16 = jnp.bfloat16

EPS = 1e-6
CHUNK = 64
PAST_LEN = 1024

POOL_WINDOWS = (2, 4, 8, 16)
POOL_GROUP_DIM = 256
POOL_BUF = 15
POOL_HALO = 16

ML_HEADS = 4
ML_HEAD_DIM = 256

ATT_HEADS = 8
ATT_HEAD_DIM = 128
ATT_WINDOW = 512
REL_CLIP = 256

BRANCH_WIDTH = 1024
LANES = 128

N_BRANCH = 3
COL_PU, COL_PZ, COL_MQ, COL_MK, COL_MV, COL_MO, COL_MZ, COL_AQ, COL_AK, COL_AV, COL_AZ = range(11)
COL_GATES = 11
W_IN_GATES_START = 7168
W_IN_GATES_END = 7176
W_IN_GATE_COLS = W_IN_GATES_END - W_IN_GATES_START

PROJ_DTYPE = jnp.bfloat16
NEG_BIG = -1e30
LOG2_E = 1.4426950408889634
LN_2 = 0.6931471805599453
MIB = 1024 * 1024


def _params(semantics, vmem_mib):
    return pltpu.CompilerParams(dimension_semantics=semantics, vmem_limit_bytes=vmem_mib * MIB)


def _layer_spec(tail, layer, single_buffer=False):
    index_map = lambda *_: (layer,) + (0,) * len(tail)
    if single_buffer:
        return pl.BlockSpec((None,) + tuple(tail), index_map, pipeline_mode=pl.Buffered(1))
    return pl.BlockSpec((None,) + tuple(tail), index_map)


def _sigmoid(x):
    return 1.0 / (1.0 + jnp.exp(-x))


def _silu(x):
    return x * _sigmoid(x)


def _nt_dot(a, b):
    return lax.dot_general(a, b, (((1,), (1,)), ((), ())), preferred_element_type=F32)


def _tn_dot(a, b):
    return lax.dot_general(a, b, (((0,), (0,)), ((), ())), preferred_element_type=F32)


def _repack_kernel(a_ref, b_ref, o_ref, *, blocks_before):
    j = pl.program_id(1)

    @pl.when(j < blocks_before)
    def _():
        o_ref[...] = a_ref[...].astype(BF16)

    @pl.when(j >= blocks_before)
    def _():
        shifted = jnp.concatenate([a_ref[W_IN_GATE_COLS:, :], b_ref[...]], axis=0)
        o_ref[...] = shifted.astype(BF16)


def _repack_w_in_t(w_in_t):
    depth, width, d = w_in_t.shape
    tn = BRANCH_WIDTH
    out_width = width - W_IN_GATE_COLS
    assert out_width % tn == 0 and W_IN_GATES_START % tn == 0
    kernel = functools.partial(_repack_kernel, blocks_before=W_IN_GATES_START // tn)
    return pl.pallas_call(
        kernel,
        grid=(depth, out_width // tn),
        in_specs=[pl.BlockSpec((None, tn, d), lambda l, j: (l, j, 0)),
                  pl.BlockSpec((None, W_IN_GATE_COLS, d), lambda l, j: (l, (j + 1) * (tn // W_IN_GATE_COLS), 0))],
        out_specs=pl.BlockSpec((None, tn, d), lambda l, j: (l, j, 0)),
        out_shape=jax.ShapeDtypeStruct((depth, out_width, d), BF16),
        compiler_params=_params(("parallel", "arbitrary"), 40),
        name="repack_w_in",
    )(w_in_t, w_in_t)


def _inproj_kernel(x_ref, g_ref, wg_ref, gq_ref, gk_ref, w_hbm, og_ref, ogt_ref, o_hbm, h_ref, wbuf, obuf,
                   wsem, osem, *, layer, n_blocks):
    i = pl.program_id(0)
    tm, tn = obuf.shape[1], obuf.shape[2]

    def w_copy(j, slot):
        return pltpu.make_async_copy(w_hbm.at[layer, pl.ds(j * tn, tn), :], wbuf.at[slot], wsem.at[slot])

    def o_copy(j, slot):
        return pltpu.make_async_copy(obuf.at[slot], o_hbm.at[j, pl.ds(i * tm, tm), :], osem.at[slot])

    w_copy(0, 0).start()

    x = x_ref[...]
    ms = jnp.mean(x * x, axis=-1, keepdims=True)
    h = (x * lax.rsqrt(ms + EPS) * g_ref[...]).astype(BF16)
    h_ref[...] = h
    og_ref[...] = _nt_dot(h, wg_ref[...])
    ogt_ref[...] = _nt_dot(wg_ref[...], h)

    def block(j, carry):
        slot = j % 2

        @pl.when(j + 1 < n_blocks)
        def _():
            w_copy(j + 1, 1 - slot).start()

        w_copy(j, slot).wait()

        @pl.when(j >= 2)
        def _():
            o_copy(j - 2, slot).wait()

        is_q = j == COL_AQ
        head_normed = jnp.logical_or(is_q, j == COL_AK)

        @pl.when(jnp.logical_not(head_normed))
        def _():
            obuf[slot] = _nt_dot(h_ref[...], wbuf[slot]).astype(obuf.dtype)

        @pl.when(head_normed)
        def _():
            acc = _nt_dot(h_ref[...], wbuf[slot])
            gain = jnp.where(is_q, gq_ref[...], gk_ref[...])
            for hd in range(ATT_HEADS):
                sl = slice(hd * ATT_HEAD_DIM, (hd + 1) * ATT_HEAD_DIM)
                a = acc[:, sl]
                r = lax.rsqrt(jnp.mean(a * a, axis=-1, keepdims=True) + EPS)
                obuf[slot, :, sl] = (a * r * gain).astype(obuf.dtype)

        o_copy(j, slot).start()
        return carry

    lax.fori_loop(0, n_blocks, block, 0)
    o_copy(n_blocks - 2, (n_blocks - 2) % 2).wait()
    o_copy(n_blocks - 1, (n_blocks - 1) % 2).wait()


def _inproj(x2, gain, w_main_t, w_gate_t, gq, gk, layer, tm):
    n, d = x2.shape
    nw = w_main_t.shape[1]
    tn = BRANCH_WIDTH
    n_blocks = nw // tn
    assert tn == ATT_HEADS * ATT_HEAD_DIM and n_blocks >= 2
    kernel = functools.partial(_inproj_kernel, layer=layer, n_blocks=n_blocks)
    return pl.pallas_call(
        kernel,
        grid=(n // tm,),
        in_specs=[pl.BlockSpec((tm, d), lambda i: (i, 0)),
                  _layer_spec((1, d), layer),
                  _layer_spec((LANES, d), layer),
                  _layer_spec((1, ATT_HEAD_DIM), layer),
                  _layer_spec((1, ATT_HEAD_DIM), layer),
                  pl.BlockSpec(memory_space=pl.ANY)],
        out_specs=[pl.BlockSpec((tm, LANES), lambda i: (i, 0)),
                   pl.BlockSpec((LANES, tm), lambda i: (0, i)),
                   pl.BlockSpec(memory_space=pl.ANY)],
        out_shape=[jax.ShapeDtypeStruct((n, LANES), F32),
                   jax.ShapeDtypeStruct((LANES, n), F32),
                   jax.ShapeDtypeStruct((n_blocks, n, tn), PROJ_DTYPE)],
        scratch_shapes=[pltpu.VMEM((tm, d), BF16),
                        pltpu.VMEM((2, tn, d), BF16),
                        pltpu.VMEM((2, tm, tn), PROJ_DTYPE),
                        pltpu.SemaphoreType.DMA((2,)),
                        pltpu.SemaphoreType.DMA((2,))],
        compiler_params=_params(("arbitrary",), 48),
        name="inproj",
    )(x2, gain, w_gate_t, gq, gk, w_main_t)


def _att_keys(group):
    return -(-(ATT_WINDOW + group) // LANES) * LANES


def _attn_bias(rel_bias, group):
    keys = _att_keys(group)
    period = keys + group
    m = np.arange(period)
    m = np.where(m <= keys, m, m - period)
    idx = np.clip(ATT_WINDOW - m, -REL_CLIP, REL_CLIP) + REL_CLIP
    vec = rel_bias[:, idx].astype(F32)
    heads = rel_bias.shape[0]
    bias = jnp.tile(vec, (1, group))[:, :group * (period - 1)].reshape(heads, group, period - 1)[:, :, :keys]
    i = np.arange(group)[:, None]
    j = np.arange(keys)[None, :]
    lo = (i // CHUNK) * CHUNK
    band = (j >= lo) & (j < lo + ATT_WINDOW + CHUNK)
    return jnp.where(jnp.asarray(band)[None], bias * LOG2_E, NEG_BIG)


def _attn_kernel(q_ref, kp_ref, kc_ref, vp_ref, vc_ref, az_ref, bias_ref, o_ref, kw_ref, vw_ref, s_ref, m_ref,
                 *, tq, group, first_prev_invalid, prev_by_head):
    rows = kw_ref.shape[0]
    keys = bias_ref.shape[2]
    if prev_by_head:
        for h in range(ATT_HEADS):
            sl = slice(h * ATT_HEAD_DIM, (h + 1) * ATT_HEAD_DIM)
            kw_ref[0:ATT_WINDOW, sl] = kp_ref[pl.ds(h, ATT_WINDOW, stride=ATT_HEADS), :].astype(BF16)
            vw_ref[0:ATT_WINDOW, sl] = vp_ref[pl.ds(h, ATT_WINDOW, stride=ATT_HEADS), :].astype(BF16)
    else:
        kw_ref[0:ATT_WINDOW, :] = kp_ref[0].astype(BF16)
        vw_ref[0:ATT_WINDOW, :] = vp_ref[0].astype(BF16)
    kw_ref[ATT_WINDOW:ATT_WINDOW + tq, :] = kc_ref[0].astype(BF16)
    vw_ref[ATT_WINDOW:ATT_WINDOW + tq, :] = vc_ref[0].astype(BF16)
    if rows > ATT_WINDOW + tq:
        pad = jnp.zeros((rows - ATT_WINDOW - tq, kw_ref.shape[1]), BF16)
        kw_ref[ATT_WINDOW + tq:rows, :] = pad
        vw_ref[ATT_WINDOW + tq:rows, :] = pad

    def attend(mask_prev):
        col = lax.broadcasted_iota(jnp.int32, (group, keys), 1)
        ones = jnp.ones((keys, ATT_HEAD_DIM), BF16)
        for g in range(tq // group):
            r0 = g * group
            for h in range(ATT_HEADS):
                sl = slice(h * ATT_HEAD_DIM, (h + 1) * ATT_HEAD_DIM)
                q = q_ref[0, r0:r0 + group, sl].astype(BF16)
                k = kw_ref[r0:r0 + keys, sl]
                s = _nt_dot(q, k) + bias_ref[h]
                if mask_prev:
                    s = jnp.where(col + r0 < ATT_WINDOW, NEG_BIG, s)
                s_ref[h] = s
                m_ref[h] = jnp.max(s, axis=-1, keepdims=True)
            for h in range(ATT_HEADS):
                sl = slice(h * ATT_HEAD_DIM, (h + 1) * ATT_HEAD_DIM)
                v1 = jnp.concatenate([vw_ref[r0:r0 + keys, sl], ones], axis=1)
                p = jnp.exp2(s_ref[h] - m_ref[h])
                o = jnp.dot(p.astype(BF16), v1, preferred_element_type=F32)
                o = o[:, :ATT_HEAD_DIM] / o[:, ATT_HEAD_DIM:]
                z = az_ref[0, r0:r0 + group, sl].astype(F32)
                o_ref[0, r0:r0 + group, sl] = (o * _silu(z)).astype(o_ref.dtype)

    if first_prev_invalid:
        first = pl.program_id(1) == 0
        pl.when(first)(functools.partial(attend, True))
        pl.when(jnp.logical_not(first))(functools.partial(attend, False))
    else:
        attend(False)


def _attention(proj4, kprev, vprev, bias, *, tq, group, cache_layer):
    _, b, t, w = proj4.shape
    nt = t // tq
    prompt = cache_layer is None
    if prompt:
        assert tq == ATT_WINDOW
        kprev = vprev = proj4
        kprev_spec = pl.BlockSpec((None, 1, ATT_WINDOW, w), lambda bi, i: (COL_AK, bi, jnp.maximum(i - 1, 0), 0))
        vprev_spec = pl.BlockSpec((None, 1, ATT_WINDOW, w), lambda bi, i: (COL_AV, bi, jnp.maximum(i - 1, 0), 0))
    else:
        assert nt == 1 and kprev.shape[2:] == (ATT_WINDOW, ATT_HEADS, ATT_HEAD_DIM)
        depth = kprev.shape[0]
        kprev = kprev.reshape(depth, b, ATT_WINDOW * ATT_HEADS, ATT_HEAD_DIM)
        vprev = vprev.reshape(depth, b, ATT_WINDOW * ATT_HEADS, ATT_HEAD_DIM)
        kprev_spec = pl.BlockSpec((None, None, ATT_WINDOW * ATT_HEADS, ATT_HEAD_DIM),
                                  lambda bi, i: (cache_layer, bi, 0, 0))
        vprev_spec = kprev_spec
    keys = bias.shape[2]
    rows = tq + keys - group
    cur = lambda c: pl.BlockSpec((None, 1, tq, w), lambda bi, i: (c, bi, i, 0))
    kernel = functools.partial(_attn_kernel, tq=tq, group=group, first_prev_invalid=prompt, prev_by_head=not prompt)
    return pl.pallas_call(
        kernel,
        grid=(b, nt),
        in_specs=[cur(COL_AQ), kprev_spec, cur(COL_AK), vprev_spec, cur(COL_AV), cur(COL_AZ),
                  pl.BlockSpec((ATT_HEADS, group, keys), lambda bi, i: (0, 0, 0))],
        out_specs=pl.BlockSpec((1, tq, w), lambda bi, i: (bi, i, 0)),
        out_shape=jax.ShapeDtypeStruct((b, t, w), BF16),
        scratch_shapes=[pltpu.VMEM((rows, w), BF16), pltpu.VMEM((rows, w), BF16),
                        pltpu.VMEM((ATT_HEADS, group, keys), F32), pltpu.VMEM((ATT_HEADS, group, 1), F32)],
        compiler_params=_params(("parallel", "arbitrary"), 48),
        name="attention",
    )(proj4, kprev, proj4, vprev, proj4, proj4, bias)


def _log_sigmoid(x):
    return jnp.minimum(x, 0.0) - jnp.log1p(jnp.exp(-jnp.abs(x)))


def _split3(x):
    hi = x.astype(BF16)
    rest = x - hi.astype(F32)
    mid = rest.astype(BF16)
    lo = (rest - mid.astype(F32)).astype(BF16)
    return hi, mid, lo


def _mlstm_kernel(q_ref, k_ref, v_ref, o_ref, z_ref, g_ref, gt_ref, gb_ref, gbt_ref, hn_ref, tri_ref,
                  c0_ref, n0_ref, m0_ref, c_stack_ref, y_ref, c_ref, n_ref, m_ref, *, chunk, rows_from_input):
    del c_stack_ref
    L = chunk
    D = ML_HEAD_DIM

    @pl.when(pl.program_id(1) == 0)
    def _():
        c_ref[...] = c0_ref[...]
        n_ref[...] = n0_ref[...]
        m_ref[...] = m0_ref[...]

    row = lax.broadcasted_iota(jnp.int32, (L, L), 0)
    col = lax.broadcasted_iota(jnp.int32, (L, L), 1)
    causal = col <= row

    n_gates = 2 * ML_HEADS
    gates = (g_ref[0] + gb_ref[...]) * LOG2_E
    if rows_from_input:
        gates_t = gt_ref[0:n_gates, :] + gbt_ref[0:n_gates, :]
        fill = (jnp.zeros((n_gates, L), BF16),)
        a_rows = jnp.dot(jnp.concatenate(_split3(_log_sigmoid(gates_t)) + fill, axis=0), tri_ref[1],
                         preferred_element_type=F32)
        a_rows = (a_rows[:n_gates] + a_rows[n_gates:2 * n_gates] + a_rows[2 * n_gates:3 * n_gates]) * LOG2_E
        gates_t = gates_t * LOG2_E
        pick = (lax.broadcasted_iota(jnp.int32, (4 * n_gates, LANES), 0) % n_gates
                == lax.broadcasted_iota(jnp.int32, (4 * n_gates, LANES), 1)).astype(BF16)
        a_cols = _tn_dot(jnp.concatenate(_split3(a_rows) + fill, axis=0), pick)
    else:
        logf = _log_sigmoid(g_ref[0] + gb_ref[...])
        a_cols = jnp.dot(tri_ref[0], jnp.concatenate(_split3(logf), axis=1), preferred_element_type=F32)
        a_cols = (a_cols[:, :LANES] + a_cols[:, LANES:2 * LANES] + a_cols[:, 2 * LANES:]) * LOG2_E
        eye = row == col

    ones = jnp.ones((L, LANES), BF16)
    k_scale = jnp.asarray(D ** -0.5, BF16)

    for hd in range(ML_HEADS):
        sl = slice(hd * D, (hd + 1) * D)
        ig_col = gates[:, hd:hd + 1]
        a_col = a_cols[:, ML_HEADS + hd:ML_HEADS + hd + 1]
        if rows_from_input:
            ig_row = gates_t[hd:hd + 1, :]
            a_row = a_rows[ML_HEADS + hd:ML_HEADS + hd + 1, :]
        else:
            ig_row = jnp.sum(jnp.where(eye, ig_col, 0.0), axis=0, keepdims=True)
            a_row = jnp.sum(jnp.where(eye, a_col, 0.0), axis=0, keepdims=True)
        b = a_col[L - 1:L, :]
        m_prev = m_ref[0, hd:hd + 1, 0:1] * LOG2_E
        c_prev = c_ref[0, hd]
        n_prev = n_ref[0, hd:hd + 1, :]

        logd = jnp.where(causal, a_col - a_row + ig_row, NEG_BIG)
        inter = a_col + m_prev
        m_row = jnp.maximum(inter, jnp.max(logd, axis=1, keepdims=True))
        dmat = jnp.exp2(logd - m_row)
        w_inter = jnp.exp2(inter - m_row)

        qb = q_ref[0, :, sl].astype(BF16)
        kb = k_ref[0, :, sl].astype(BF16) * k_scale
        v1 = jnp.concatenate([v_ref[0, :, sl].astype(BF16), ones], axis=1)
        c1 = jnp.concatenate([c_prev.astype(BF16), jnp.broadcast_to(n_prev.astype(BF16), (LANES, D))], axis=0)

        s = _nt_dot(qb, kb) * dmat
        sv = jnp.dot(s.astype(BF16), v1, preferred_element_type=F32)
        qc = _nt_dot(qb, c1)
        num = sv[:, :D] + w_inter * qc[:, :D]
        den = sv[:, D:] + w_inter * qc[:, D:]
        r = 1.0 / jnp.maximum(jnp.abs(den), jnp.exp2(-m_row))
        h = num * jnp.concatenate([r] * (D // LANES), axis=1)

        g_col = b - a_col + ig_col
        m_new = jnp.maximum(b + m_prev, jnp.max(g_col, axis=0, keepdims=True))
        wk = jnp.exp2(g_col - m_new)
        decay = jnp.exp2(b + m_prev - m_new)
        kw = kb * wk.astype(BF16)
        upd = _tn_dot(v1, kw)
        c_ref[0, hd] = decay * c_prev + upd[:D]
        n_ref[0, hd:hd + 1, :] = decay * n_prev + upd[D:D + 1]
        m_ref[0, hd:hd + 1, :] = jnp.broadcast_to(m_new * LN_2, (1, LANES))

        hm = h * _sigmoid(o_ref[0, :, sl].astype(F32))
        hm = hm * lax.rsqrt(jnp.mean(hm * hm, axis=-1, keepdims=True) + EPS) * hn_ref[:, sl]
        y_ref[0, :, sl] = (hm * _silu(z_ref[0, :, sl].astype(F32))).astype(y_ref.dtype)


def _mlstm(proj4, gates3, gates_t, gate_bias, head_norm, c0, n0, m0, layer, state_layer, chunk, c_stack):
    _, b, t, w = proj4.shape
    h, d = ML_HEADS, ML_HEAD_DIM
    nc = t // chunk
    rows_from_input = chunk % LANES == 0
    col = lambda c: pl.BlockSpec((None, 1, chunk, w), lambda bi, i: (c, bi, i, 0))
    state_in = lambda shape: pl.BlockSpec((None, 1) + shape, lambda bi, i: (state_layer, bi) + (0,) * len(shape))
    state_out = lambda shape: pl.BlockSpec((1,) + shape, lambda bi, i: (bi,) + (0,) * len(shape))
    if rows_from_input:
        gates_t_spec = pl.BlockSpec((LANES, chunk), lambda bi, i: (0, bi * nc + i))
    else:
        gates_t_spec = pl.BlockSpec((LANES, LANES), lambda bi, i: (0, 0))
    lower = jnp.tril(jnp.ones((chunk, chunk), BF16))
    triangles = jnp.stack([lower, lower.T])
    gate_bias_t = jnp.swapaxes(gate_bias, 1, 2)
    kernel = functools.partial(_mlstm_kernel, chunk=chunk, rows_from_input=rows_from_input)
    in_specs = [col(COL_MQ), col(COL_MK), col(COL_MV), col(COL_MO), col(COL_MZ),
                pl.BlockSpec((1, chunk, LANES), lambda bi, i: (bi, i, 0)),
                gates_t_spec,
                _layer_spec((1, LANES), layer),
                _layer_spec((LANES, 1), layer),
                _layer_spec((1, w), layer),
                pl.BlockSpec((2, chunk, chunk), lambda bi, i: (0, 0, 0)),
                state_in((h, d, d)), state_in((h, d)), state_in((h, LANES)),
                pl.BlockSpec(memory_space=pl.ANY)]
    args = [proj4, proj4, proj4, proj4, proj4, gates3, gates_t, gate_bias, gate_bias_t, head_norm, triangles,
            c0, n0, m0, c_stack]
    return pl.pallas_call(
        kernel,
        grid=(b, nc),
        in_specs=in_specs,
        out_specs=[pl.BlockSpec((1, chunk, w), lambda bi, i: (bi, i, 0)),
                   pl.BlockSpec((None, 1, h, d, d), lambda bi, i: (layer, bi, 0, 0, 0)),
                   state_out((h, d)), state_out((h, LANES))],
        out_shape=[jax.ShapeDtypeStruct((b, t, w), BF16),
                   jax.ShapeDtypeStruct(c_stack.shape, F32),
                   jax.ShapeDtypeStruct((b, h, d), F32),
                   jax.ShapeDtypeStruct((b, h, LANES), F32)],
        input_output_aliases={len(args) - 1: 1},
        compiler_params=_params(("parallel", "arbitrary"), 48),
        name="mlstm",
    )(*args)


def _pool_prev_rows(hist_ref, halo_ref, first):
    hist = hist_ref[0].astype(F32)
    hist = jnp.concatenate([jnp.zeros((POOL_HALO - POOL_BUF, hist.shape[1]), F32), hist], axis=0)
    return jnp.where(first, hist, halo_ref[...].astype(F32))


def _pool_mix_tile(u, prev, z_ref, w_ref, sc_ref, pos):
    outs = []
    for gi, win in enumerate(POOL_WINDOWS):
        sl = slice(gi * POOL_GROUP_DIM, (gi + 1) * POOL_GROUP_DIM)
        ext = jnp.concatenate([prev[:, sl], u[:, sl]], axis=0)
        acc = ext
        span = 1
        while span < win:
            acc = acc + pltpu.roll(acc, span, 0)
            span *= 2
        cnt = jnp.minimum(pos + 1, win).astype(F32)
        mean = acc[POOL_HALO:, :] / cnt
        m = (mean - u[:, sl]).astype(BF16)
        y = jnp.dot(m, w_ref[gi], preferred_element_type=F32) * sc_ref[:, sl]
        outs.append((y * _silu(z_ref[:, sl].astype(F32))).astype(BF16))
    return jnp.concatenate(outs, axis=1)


def _pool_kernel(u_ref, halo_ref, hist_ref, z_ref, w_ref, sc_ref, y_ref, *, tp, pos0):
    i = pl.program_id(1)
    prev = _pool_prev_rows(hist_ref, halo_ref.at[0], i == 0)
    pos = pos0 + i * tp + lax.broadcasted_iota(jnp.int32, (tp, 1), 0)
    y_ref[0] = _pool_mix_tile(u_ref[0].astype(F32), prev, z_ref.at[0], w_ref, sc_ref, pos)


def _pool(proj4, hist, w_group, scale, layer, hist_layer, *, tp, pos0):
    _, b, t, w = proj4.shape
    halo_blocks = tp // POOL_HALO
    kernel = functools.partial(_pool_kernel, tp=tp, pos0=pos0)
    return pl.pallas_call(
        kernel,
        grid=(b, t // tp),
        in_specs=[pl.BlockSpec((None, 1, tp, w), lambda bi, i: (COL_PU, bi, i, 0)),
                  pl.BlockSpec((None, 1, POOL_HALO, w),
                               lambda bi, i: (COL_PU, bi, jnp.maximum(i * halo_blocks - 1, 0), 0)),
                  pl.BlockSpec((None, 1, POOL_BUF, w), lambda bi, i: (hist_layer, bi, 0, 0)),
                  pl.BlockSpec((None, 1, tp, w), lambda bi, i: (COL_PZ, bi, i, 0)),
                  _layer_spec((len(POOL_WINDOWS), POOL_GROUP_DIM, POOL_GROUP_DIM), layer),
                  _layer_spec((1, w), layer)],
        out_specs=pl.BlockSpec((1, tp, w), lambda bi, i: (bi, i, 0)),
        out_shape=jax.ShapeDtypeStruct((b, t, w), BF16),
        compiler_params=_params(("parallel", "arbitrary"), 32),
        name="pool",
    )(proj4, proj4, hist, proj4, w_group, scale)


def _merge_kernel(*refs):
    y_refs = refs[:N_BRANCH]
    gate_refs = refs[N_BRANCH:-4]
    x_ref, wb_ref, wo_ref, o_ref = refs[-4:]
    per_branch = len(gate_refs) // N_BRANCH
    merged = None
    for b in range(N_BRANCH):
        gate = jnp.concatenate([_sigmoid(r[...].astype(F32)) for r in gate_refs[b * per_branch:(b + 1) * per_branch]],
                               axis=1)
        term = gate * jnp.dot(y_refs[b][...], wb_ref[b], preferred_element_type=F32)
        merged = term if merged is None else merged + term
    o_ref[...] = x_ref[...] + jnp.dot(merged.astype(BF16), wo_ref[...], preferred_element_type=F32)


def _merge(y_pool, y_ml, y_att, proj, x2, w_branch, w_out, layer, tm):
    n, d = x2.shape
    w = BRANCH_WIDTH
    row = lambda width: pl.BlockSpec((tm, width), lambda i: (i, 0))
    n_gate_blocks = N_BRANCH * (d // w)
    gate_specs = [pl.BlockSpec((None, tm, w), lambda i, k=k: (COL_GATES + k, i, 0)) for k in range(n_gate_blocks)]
    return pl.pallas_call(
        _merge_kernel,
        grid=(n // tm,),
        in_specs=[row(w)] * N_BRANCH + gate_specs + [row(d), _layer_spec((N_BRANCH, w, d), layer, True),
                                                    _layer_spec((d, d), layer, True)],
        out_specs=row(d),
        out_shape=jax.ShapeDtypeStruct((n, d), F32),
        compiler_params=_params(("parallel",), 56),
        name="merge",
    )(y_pool, y_ml, y_att, *([proj] * n_gate_blocks), x2, w_branch, w_out)


def _ple_kernel(x_ref, p_ref, g_ref, wg_ref, wp_ref, o_ref):
    x = x_ref[...]
    h = (x * lax.rsqrt(jnp.mean(x * x, axis=-1, keepdims=True) + EPS) * g_ref[...]).astype(BF16)
    pg = _sigmoid(jnp.dot(h, wg_ref[...], preferred_element_type=F32))
    pp = jnp.dot(p_ref[...].astype(BF16), wp_ref[...], preferred_element_type=F32)
    o_ref[...] = x + pg * pp


def _ple(x2, p_all, gain, w_gate, w_proj, layer, tm):
    n, d = x2.shape
    dp = p_all.shape[2]
    return pl.pallas_call(
        _ple_kernel,
        grid=(n // tm,),
        in_specs=[pl.BlockSpec((tm, d), lambda i: (i, 0)),
                  pl.BlockSpec((None, tm, dp), lambda i: (layer, i, 0)),
                  _layer_spec((1, d), layer, True), _layer_spec((d, d), layer, True),
                  _layer_spec((dp, d), layer, True)],
        out_specs=pl.BlockSpec((tm, d), lambda i: (i, 0)),
        out_shape=jax.ShapeDtypeStruct((n, d), F32),
        compiler_params=_params(("parallel",), 48),
        name="ple",
    )(x2, p_all, gain, w_gate, w_proj)


def _tile(n, preferred):
    t = min(n, preferred)
    assert n % t == 0, (n, t)
    return t


def _prepare_weights(norm_mix, w_in, w_pool_group, pool_scale, b_ig, b_fg, ml_head_norm, att_q_norm, att_k_norm,
                     att_rel_bias, w_branch, w_out, ple_norm, w_ple_gate, w_ple_proj):
    depth, d, width = w_in.shape
    assert width == W_IN_GATES_END + (COL_GATES - COL_AQ) * BRANCH_WIDTH + N_BRANCH * d
    pad = LANES - W_IN_GATE_COLS
    w_in_t = jnp.swapaxes(w_in, 1, 2)
    w_gate = jnp.pad(w_in_t[:, W_IN_GATES_START:W_IN_GATES_END, :], ((0, 0), (0, pad), (0, 0))).astype(BF16)
    gate_bias = jnp.pad(jnp.concatenate([b_ig, b_fg], axis=1), ((0, 0), (0, pad)))[:, None, :].astype(F32)
    gq = att_q_norm * (ATT_HEAD_DIM ** -0.5 * LOG2_E)
    return dict(
        norm_g=norm_mix[:, None, :], w_main=_repack_w_in_t(w_in_t), w_gate=w_gate, gate_bias=gate_bias,
        w_pool=w_pool_group.astype(BF16), pool_scale=pool_scale[:, None, :], head_norm=ml_head_norm[:, None, :],
        gq=gq[:, None, :], gk=att_k_norm[:, None, :], rel_bias=att_rel_bias,
        w_branch=w_branch.astype(BF16), w_out=w_out.astype(BF16), ple_norm=ple_norm[:, None, :],
        w_ple_gate=w_ple_gate.astype(BF16), w_ple_proj=w_ple_proj.astype(BF16))


def _layer(x3, p_all, hist, c0, n0, m0, kv_cache, state_layer, pos0, layer, lw, c_stack):
    b, t, d = x3.shape
    n = b * t
    w = BRANCH_WIDTH
    prompt = kv_cache is None
    assert t % CHUNK == 0 and t >= POOL_BUF

    gates, gates_t, proj = _inproj(x3.reshape(n, d), lw["norm_g"], lw["w_main"], lw["w_gate"], lw["gq"], lw["gk"],
                                   layer, _tile(n, 1024))
    proj4 = proj.reshape(-1, b, t, w)

    if prompt:
        tq, group = _tile(t, ATT_WINDOW), 2 * CHUNK
        kprev, vprev, cache_layer = None, None, None
    else:
        tq, group = t, CHUNK
        assert t == CHUNK
        kprev, vprev = kv_cache
        cache_layer = state_layer
    bias = _attn_bias(lw["rel_bias"][layer], group)
    y_att = _attention(proj4, kprev, vprev, bias, tq=tq, group=group, cache_layer=cache_layer)

    m0b = jnp.broadcast_to(m0[..., None], m0.shape + (LANES,))
    y_ml, c1, n1, m1 = _mlstm(proj4, gates.reshape(b, t, LANES), gates_t, lw["gate_bias"], lw["head_norm"],
                              c0, n0, m0b, layer, state_layer, _tile(t, 256), c_stack)

    y_pool = _pool(proj4, hist, lw["w_pool"], lw["pool_scale"], layer, state_layer, tp=_tile(t, 512), pos0=pos0)

    x1 = _merge(y_pool.reshape(n, w), y_ml.reshape(n, w), y_att.reshape(n, w), proj, x3.reshape(n, d),
                lw["w_branch"], lw["w_out"], layer, _tile(n, 256))
    x2 = _ple(x1, p_all.reshape(p_all.shape[0], n, -1), lw["ple_norm"], lw["w_ple_gate"], lw["w_ple_proj"],
              layer, _tile(n, 512))

    keep = min(ATT_WINDOW, t) if prompt else t
    new_pool = proj4[COL_PU, :, t - POOL_BUF:].astype(F32)
    new_k = proj4[COL_AK, :, t - keep:].astype(F32).reshape(b, keep, ATT_HEADS, ATT_HEAD_DIM)
    new_v = proj4[COL_AV, :, t - keep:].astype(F32).reshape(b, keep, ATT_HEADS, ATT_HEAD_DIM)
    return x2.reshape(b, t, d), (new_pool, c1, n1, m1[:, :, 0], new_k, new_v)


def kernel(x_prompt, x_sample, cache_att_k, cache_att_v, state_pool, state_mlstm_c, state_mlstm_n, state_mlstm_m, p_prompt, p_sample, norm_mix, w_in, w_pool_group, pool_scale, b_ig, b_fg, ml_head_norm, att_q_norm, att_k_norm, att_rel_bias, w_branch, w_out, ple_norm, w_ple_gate, w_ple_proj):
    xp, xs = x_prompt, x_sample
    bp = x_prompt.shape[0]
    depth = w_in.shape[0]
    lw = _prepare_weights(norm_mix, w_in, w_pool_group, pool_scale, b_ig, b_fg, ml_head_norm, att_q_norm,
                          att_k_norm, att_rel_bias, w_branch, w_out, ple_norm, w_ple_gate, w_ple_proj)
    hist0 = jnp.zeros((1, bp, POOL_BUF, BRANCH_WIDTH), F32)
    c0 = jnp.zeros((1, bp, ML_HEADS, ML_HEAD_DIM, ML_HEAD_DIM), F32)
    n0 = jnp.zeros((1, bp, ML_HEADS, ML_HEAD_DIM), F32)
    m0 = jnp.zeros((1, bp, ML_HEADS), F32)
    sp = [[] for _ in range(6)]
    ss = [[] for _ in range(6)]
    c_p = jnp.zeros((depth,) + c0.shape[1:], F32)
    c_s = jnp.zeros(state_mlstm_c.shape, F32)
    for i in range(depth):
        xp, st_p = _layer(xp, p_prompt, hist0, c0, n0, m0, None, 0, 0, i, lw, c_p)
        xs, st_s = _layer(xs, p_sample, state_pool, state_mlstm_c, state_mlstm_n, state_mlstm_m,
                          (cache_att_k, cache_att_v), i, PAST_LEN, i, lw, c_s)
        c_p, c_s = st_p[1], st_s[1]
        for j in range(6):
            sp[j].append(st_p[j])
            ss[j].append(st_s[j])
    pool_p, _, n_p, m_p, k_p, v_p = [jnp.stack(a) if j != 1 else None for j, a in enumerate(sp)]
    pool_s, _, n_s, m_s, k_s, v_s = [jnp.stack(a) if j != 1 else None for j, a in enumerate(ss)]
    return (xp, xs, pool_p, pool_s, c_p, c_s, n_p, n_s, m_p, m_s, k_p, k_s, v_p, v_s)
```

```python
import functools

import numpy as np
import jax
import jax.numpy as jnp
from jax import lax
from jax.experimental import pallas as pl
from jax.experimental.pallas import tpu as pltpu

F32 = jnp.float32
BF16 = jnp.bfloat16

EPS = 1e-6
CHUNK = 64
PAST_LEN = 1024

POOL_WINDOWS = (2, 4, 8, 16)
POOL_GROUP_DIM = 256
POOL_BUF = 15
POOL_HALO = 16

ML_HEADS = 4
ML_HEAD_DIM = 256

ATT_HEADS = 8
ATT_HEAD_DIM = 128
ATT_WINDOW = 512
REL_CLIP = 256

BRANCH_WIDTH = 1024
LANES = 128

N_BRANCH = 3
COL_PU, COL_PZ, COL_MQ, COL_MK, COL_MV, COL_MO, COL_MZ, COL_AQ, COL_AK, COL_AV, COL_AZ = range(11)
COL_GATES = 11
W_IN_GATES_START = 7168
W_IN_GATES_END = 7176
W_IN_GATE_COLS = W_IN_GATES_END - W_IN_GATES_START

PROJ_DTYPE = jnp.bfloat16
NEG_BIG = -1e30
LOG2_E = 1.4426950408889634
LN_2 = 0.6931471805599453
MIB = 1024 * 1024


def _params(semantics, vmem_mib):
    return pltpu.CompilerParams(dimension_semantics=semantics, vmem_limit_bytes=vmem_mib * MIB)


def _layer_spec(tail, layer, single_buffer=False):
    index_map = lambda *_: (layer,) + (0,) * len(tail)
    if single_buffer:
        return pl.BlockSpec((None,) + tuple(tail), index_map, pipeline_mode=pl.Buffered(1))
    return pl.BlockSpec((None,) + tuple(tail), index_map)


def _sigmoid(x):
    return 1.0 / (1.0 + jnp.exp(-x))


def _silu(x):
    return x * _sigmoid(x)


def _nt_dot(a, b):
    return lax.dot_general(a, b, (((1,), (1,)), ((), ())), preferred_element_type=F32)


def _tn_dot(a, b):
    return lax.dot_general(a, b, (((0,), (0,)), ((), ())), preferred_element_type=F32)


def _repack_kernel(a_ref, b_ref, o_ref, *, blocks_before):
    j = pl.program_id(1)

    @pl.when(j < blocks_before)
    def _():
        o_ref[...] = a_ref[...].astype(BF16)

    @pl.when(j >= blocks_before)
    def _():
        shifted = jnp.concatenate([a_ref[W_IN_GATE_COLS:, :], b_ref[...]], axis=0)
        o_ref[...] = shifted.astype(BF16)


def _repack_w_in_t(w_in_t):
    depth, width, d = w_in_t.shape
    tn = BRANCH_WIDTH
    out_width = width - W_IN_GATE_COLS
    assert out_width % tn == 0 and W_IN_GATES_START % tn == 0
    kernel = functools.partial(_repack_kernel, blocks_before=W_IN_GATES_START // tn)
    return pl.pallas_call(
        kernel,
        grid=(depth, out_width // tn),
        in_specs=[pl.BlockSpec((None, tn, d), lambda l, j: (l, j, 0)),
                  pl.BlockSpec((None, W_IN_GATE_COLS, d), lambda l, j: (l, (j + 1) * (tn // W_IN_GATE_COLS), 0))],
        out_specs=pl.BlockSpec((None, tn, d), lambda l, j: (l, j, 0)),
        out_shape=jax.ShapeDtypeStruct((depth, out_width, d), BF16),
        compiler_params=_params(("parallel", "arbitrary"), 40),
        name="repack_w_in",
    )(w_in_t, w_in_t)


def _inproj_kernel(x_ref, g_ref, w_ref, wg_ref, gq_ref, gk_ref, o_ref, og_ref, ogt_ref, h_ref):
    j = pl.program_id(1)

    @pl.when(j == 0)
    def _():
        x = x_ref[...]
        ms = jnp.mean(x * x, axis=-1, keepdims=True)
        h = (x * lax.rsqrt(ms + EPS) * g_ref[...]).astype(BF16)
        h_ref[...] = h
        og_ref[...] = _nt_dot(h, wg_ref[...])
        ogt_ref[...] = _nt_dot(wg_ref[...], h)

    is_q = j == COL_AQ
    head_normed = jnp.logical_or(is_q, j == COL_AK)

    @pl.when(jnp.logical_not(head_normed))
    def _():
        o_ref[...] = _nt_dot(h_ref[...], w_ref[...]).astype(o_ref.dtype)

    @pl.when(head_normed)
    def _():
        acc = _nt_dot(h_ref[...], w_ref[...])
        gain = jnp.where(is_q, gq_ref[...], gk_ref[...])
        for h in range(ATT_HEADS):
            sl = slice(h * ATT_HEAD_DIM, (h + 1) * ATT_HEAD_DIM)
            a = acc[:, sl]
            r = lax.rsqrt(jnp.mean(a * a, axis=-1, keepdims=True) + EPS)
            o_ref[:, sl] = (a * r * gain).astype(o_ref.dtype)


def _inproj(x2, gain, w_main_t, w_gate_t, gq, gk, layer, tm):
    n, d = x2.shape
    nw = w_main_t.shape[1]
    tn = BRANCH_WIDTH
    assert tn == ATT_HEADS * ATT_HEAD_DIM
    return pl.pallas_call(
        _inproj_kernel,
        grid=(n // tm, nw // tn),
        in_specs=[pl.BlockSpec((tm, d), lambda i, j: (i, 0)),
                  _layer_spec((1, d), layer),
                  pl.BlockSpec((None, tn, d), lambda i, j: (layer, j, 0)),
                  _layer_spec((LANES, d), layer),
                  _layer_spec((1, ATT_HEAD_DIM), layer),
                  _layer_spec((1, ATT_HEAD_DIM), layer)],
        out_specs=[pl.BlockSpec((None, tm, tn), lambda i, j: (j, i, 0)),
                   pl.BlockSpec((tm, LANES), lambda i, j: (i, 0)),
                   pl.BlockSpec((LANES, tm), lambda i, j: (0, i))],
        out_shape=[jax.ShapeDtypeStruct((nw // tn, n, tn), PROJ_DTYPE),
                   jax.ShapeDtypeStruct((n, LANES), F32),
                   jax.ShapeDtypeStruct((LANES, n), F32)],
        scratch_shapes=[pltpu.VMEM((tm, d), BF16)],
        compiler_params=_params(("parallel", "arbitrary"), 48),
        name="inproj",
    )(x2, gain, w_main_t, w_gate_t, gq, gk)


def _att_keys(group):
    return -(-(ATT_WINDOW + group) // LANES) * LANES


def _attn_bias(rel_bias, group):
    keys = _att_keys(group)
    period = keys + group
    m = np.arange(period)
    m = np.where(m <= keys, m, m - period)
    idx = np.clip(ATT_WINDOW - m, -REL_CLIP, REL_CLIP) + REL_CLIP
    vec = rel_bias[:, idx].astype(F32)
    heads = rel_bias.shape[0]
    bias = jnp.tile(vec, (1, group))[:, :group * (period - 1)].reshape(heads, group, period - 1)[:, :, :keys]
    i = np.arange(group)[:, None]
    j = np.arange(keys)[None, :]
    lo = (i // CHUNK) * CHUNK
    band = (j >= lo) & (j < lo + ATT_WINDOW + CHUNK)
    return jnp.where(jnp.asarray(band)[None], bias * LOG2_E, NEG_BIG)


def _attn_kernel(q_ref, kp_ref, kc_ref, vp_ref, vc_ref, az_ref, bias_ref, o_ref, kw_ref, vw_ref, s_ref, m_ref,
                 *, tq, group, first_prev_invalid, prev_by_head):
    rows = kw_ref.shape[0]
    keys = bias_ref.shape[2]
    if prev_by_head:
        for h in range(ATT_HEADS):
            sl = slice(h * ATT_HEAD_DIM, (h + 1) * ATT_HEAD_DIM)
            kw_ref[0:ATT_WINDOW, sl] = kp_ref[pl.ds(h, ATT_WINDOW, stride=ATT_HEADS), :].astype(BF16)
            vw_ref[0:ATT_WINDOW, sl] = vp_ref[pl.ds(h, ATT_WINDOW, stride=ATT_HEADS), :].astype(BF16)
    else:
        kw_ref[0:ATT_WINDOW, :] = kp_ref[0].astype(BF16)
        vw_ref[0:ATT_WINDOW, :] = vp_ref[0].astype(BF16)
    kw_ref[ATT_WINDOW:ATT_WINDOW + tq, :] = kc_ref[0].astype(BF16)
    vw_ref[ATT_WINDOW:ATT_WINDOW + tq, :] = vc_ref[0].astype(BF16)
    if rows > ATT_WINDOW + tq:
        pad = jnp.zeros((rows - ATT_WINDOW - tq, kw_ref.shape[1]), BF16)
        kw_ref[ATT_WINDOW + tq:rows, :] = pad
        vw_ref[ATT_WINDOW + tq:rows, :] = pad

    def attend(mask_prev):
        col = lax.broadcasted_iota(jnp.int32, (group, keys), 1)
        ones = jnp.ones((keys, ATT_HEAD_DIM), BF16)
        for g in range(tq // group):
            r0 = g * group
            for h in range(ATT_HEADS):
                sl = slice(h * ATT_HEAD_DIM, (h + 1) * ATT_HEAD_DIM)
                q = q_ref[0, r0:r0 + group, sl].astype(BF16)
                k = kw_ref[r0:r0 + keys, sl]
                s = _nt_dot(q, k) + bias_ref[h]
                if mask_prev:
                    s = jnp.where(col + r0 < ATT_WINDOW, NEG_BIG, s)
                s_ref[h] = s
                m_ref[h] = jnp.max(s, axis=-1, keepdims=True)
            for h in range(ATT_HEADS):
                sl = slice(h * ATT_HEAD_DIM, (h + 1) * ATT_HEAD_DIM)
                v1 = jnp.concatenate([vw_ref[r0:r0 + keys, sl], ones], axis=1)
                p = jnp.exp2(s_ref[h] - m_ref[h])
                o = jnp.dot(p.astype(BF16), v1, preferred_element_type=F32)
                o = o[:, :ATT_HEAD_DIM] / o[:, ATT_HEAD_DIM:]
                z = az_ref[0, r0:r0 + group, sl].astype(F32)
                o_ref[0, r0:r0 + group, sl] = (o * _silu(z)).astype(o_ref.dtype)

    if first_prev_invalid:
        first = pl.program_id(1) == 0
        pl.when(first)(functools.partial(attend, True))
        pl.when(jnp.logical_not(first))(functools.partial(attend, False))
    else:
        attend(False)


def _attention(proj4, kprev, vprev, bias, *, tq, group, cache_layer):
    _, b, t, w = proj4.shape
    nt = t // tq
    prompt = cache_layer is None
    if prompt:
        assert tq == ATT_WINDOW
        kprev = vprev = proj4
        kprev_spec = pl.BlockSpec((None, 1, ATT_WINDOW, w), lambda bi, i: (COL_AK, bi, jnp.maximum(i - 1, 0), 0))
        vprev_spec = pl.BlockSpec((None, 1, ATT_WINDOW, w), lambda bi, i: (COL_AV, bi, jnp.maximum(i - 1, 0), 0))
    else:
        assert nt == 1 and kprev.shape[2:] == (ATT_WINDOW, ATT_HEADS, ATT_HEAD_DIM)
        depth = kprev.shape[0]
        kprev = kprev.reshape(depth, b, ATT_WINDOW * ATT_HEADS, ATT_HEAD_DIM)
        vprev = vprev.reshape(depth, b, ATT_WINDOW * ATT_HEADS, ATT_HEAD_DIM)
        kprev_spec = pl.BlockSpec((None, None, ATT_WINDOW * ATT_HEADS, ATT_HEAD_DIM),
                                  lambda bi, i: (cache_layer, bi, 0, 0))
        vprev_spec = kprev_spec
    keys = bias.shape[2]
    rows = tq + keys - group
    cur = lambda c: pl.BlockSpec((None, 1, tq, w), lambda bi, i: (c, bi, i, 0))
    kernel = functools.partial(_attn_kernel, tq=tq, group=group, first_prev_invalid=prompt, prev_by_head=not prompt)
    return pl.pallas_call(
        kernel,
        grid=(b, nt),
        in_specs=[cur(COL_AQ), kprev_spec, cur(COL_AK), vprev_spec, cur(COL_AV), cur(COL_AZ),
                  pl.BlockSpec((ATT_HEADS, group, keys), lambda bi, i: (0, 0, 0))],
        out_specs=pl.BlockSpec((1, tq, w), lambda bi, i: (bi, i, 0)),
        out_shape=jax.ShapeDtypeStruct((b, t, w), BF16),
        scratch_shapes=[pltpu.VMEM((rows, w), BF16), pltpu.VMEM((rows, w), BF16),
                        pltpu.VMEM((ATT_HEADS, group, keys), F32), pltpu.VMEM((ATT_HEADS, group, 1), F32)],
        compiler_params=_params(("parallel", "arbitrary"), 48),
        name="attention",
    )(proj4, kprev, proj4, vprev, proj4, proj4, bias)


def _log_sigmoid(x):
    return jnp.minimum(x, 0.0) - jnp.log1p(jnp.exp(-jnp.abs(x)))


def _split3(x):
    hi = x.astype(BF16)
    rest = x - hi.astype(F32)
    mid = rest.astype(BF16)
    lo = (rest - mid.astype(F32)).astype(BF16)
    return hi, mid, lo


def _mlstm_kernel(q_ref, k_ref, v_ref, o_ref, z_ref, g_ref, gt_ref, gb_ref, gbt_ref, hn_ref, tri_ref,
                  c0_ref, n0_ref, m0_ref, c_stack_ref, y_ref, c_ref, n_ref, m_ref, *, chunk, rows_from_input):
    del c_stack_ref
    L = chunk
    D = ML_HEAD_DIM

    @pl.when(pl.program_id(1) == 0)
    def _():
        c_ref[...] = c0_ref[...]
        n_ref[...] = n0_ref[...]
        m_ref[...] = m0_ref[...]

    row = lax.broadcasted_iota(jnp.int32, (L, L), 0)
    col = lax.broadcasted_iota(jnp.int32, (L, L), 1)
    causal = col <= row

    n_gates = 2 * ML_HEADS
    gates = (g_ref[0] + gb_ref[...]) * LOG2_E
    if rows_from_input:
        gates_t = gt_ref[0:n_gates, :] + gbt_ref[0:n_gates, :]
        fill = (jnp.zeros((n_gates, L), BF16),)
        a_rows = jnp.dot(jnp.concatenate(_split3(_log_sigmoid(gates_t)) + fill, axis=0), tri_ref[1],
                         preferred_element_type=F32)
        a_rows = (a_rows[:n_gates] + a_rows[n_gates:2 * n_gates] + a_rows[2 * n_gates:3 * n_gates]) * LOG2_E
        gates_t = gates_t * LOG2_E
        pick = (lax.broadcasted_iota(jnp.int32, (4 * n_gates, LANES), 0) % n_gates
                == lax.broadcasted_iota(jnp.int32, (4 * n_gates, LANES), 1)).astype(BF16)
        a_cols = _tn_dot(jnp.concatenate(_split3(a_rows) + fill, axis=0), pick)
    else:
        logf = _log_sigmoid(g_ref[0] + gb_ref[...])
        a_cols = jnp.dot(tri_ref[0], jnp.concatenate(_split3(logf), axis=1), preferred_element_type=F32)
        a_cols = (a_cols[:, :LANES] + a_cols[:, LANES:2 * LANES] + a_cols[:, 2 * LANES:]) * LOG2_E
        eye = row == col

    ones = jnp.ones((L, LANES), BF16)
    k_scale = jnp.asarray(D ** -0.5, BF16)

    for hd in range(ML_HEADS):
        sl = slice(hd * D, (hd + 1) * D)
        ig_col = gates[:, hd:hd + 1]
        a_col = a_cols[:, ML_HEADS + hd:ML_HEADS + hd + 1]
        if rows_from_input:
            ig_row = gates_t[hd:hd + 1, :]
            a_row = a_rows[ML_HEADS + hd:ML_HEADS + hd + 1, :]
        else:
            ig_row = jnp.sum(jnp.where(eye, ig_col, 0.0), axis=0, keepdims=True)
            a_row = jnp.sum(jnp.where(eye, a_col, 0.0), axis=0, keepdims=True)
        b = a_col[L - 1:L, :]
        m_prev = m_ref[0, hd:hd + 1, 0:1] * LOG2_E
        c_prev = c_ref[0, hd]
        n_prev = n_ref[0, hd:hd + 1, :]

        logd = jnp.where(causal, a_col - a_row + ig_row, NEG_BIG)
        inter = a_col + m_prev
        m_row = jnp.maximum(inter, jnp.max(logd, axis=1, keepdims=True))
        dmat = jnp.exp2(logd - m_row)
        w_inter = jnp.exp2(inter - m_row)

        qb = q_ref[0, :, sl].astype(BF16)
        kb = k_ref[0, :, sl].astype(BF16) * k_scale
        v1 = jnp.concatenate([v_ref[0, :, sl].astype(BF16), ones], axis=1)
        c1 = jnp.concatenate([c_prev.astype(BF16), jnp.broadcast_to(n_prev.astype(BF16), (LANES, D))], axis=0)

        s = _nt_dot(qb, kb) * dmat
        sv = jnp.dot(s.astype(BF16), v1, preferred_element_type=F32)
        qc = _nt_dot(qb, c1)
        num = sv[:, :D] + w_inter * qc[:, :D]
        den = sv[:, D:] + w_inter * qc[:, D:]
        r = 1.0 / jnp.maximum(jnp.abs(den), jnp.exp2(-m_row))
        h = num * jnp.concatenate([r] * (D // LANES), axis=1)

        g_col = b - a_col + ig_col
        m_new = jnp.maximum(b + m_prev, jnp.max(g_col, axis=0, keepdims=True))
        wk = jnp.exp2(g_col - m_new)
        decay = jnp.exp2(b + m_prev - m_new)
        kw = kb * wk.astype(BF16)
        upd = _tn_dot(v1, kw)
        c_ref[0, hd] = decay * c_prev + upd[:D]
        n_ref[0, hd:hd + 1, :] = decay * n_prev + upd[D:D + 1]
        m_ref[0, hd:hd + 1, :] = jnp.broadcast_to(m_new * LN_2, (1, LANES))

        hm = h * _sigmoid(o_ref[0, :, sl].astype(F32))
        hm = hm * lax.rsqrt(jnp.mean(hm * hm, axis=-1, keepdims=True) + EPS) * hn_ref[:, sl]
        y_ref[0, :, sl] = (hm * _silu(z_ref[0, :, sl].astype(F32))).astype(y_ref.dtype)


def _mlstm(proj4, gates3, gates_t, gate_bias, head_norm, c0, n0, m0, layer, state_layer, chunk, c_stack):
    _, b, t, w = proj4.shape
    h, d = ML_HEADS, ML_HEAD_DIM
    nc = t // chunk
    rows_from_input = chunk % LANES == 0
    col = lambda c: pl.BlockSpec((None, 1, chunk, w), lambda bi, i: (c, bi, i, 0))
    state_in = lambda shape: pl.BlockSpec((None, 1) + shape, lambda bi, i: (state_layer, bi) + (0,) * len(shape))
    state_out = lambda shape: pl.BlockSpec((1,) + shape, lambda bi, i: (bi,) + (0,) * len(shape))
    if rows_from_input:
        gates_t_spec = pl.BlockSpec((LANES, chunk), lambda bi, i: (0, bi * nc + i))
    else:
        gates_t_spec = pl.BlockSpec((LANES, LANES), lambda bi, i: (0, 0))
    lower = jnp.tril(jnp.ones((chunk, chunk), BF16))
    triangles = jnp.stack([lower, lower.T])
    gate_bias_t = jnp.swapaxes(gate_bias, 1, 2)
    kernel = functools.partial(_mlstm_kernel, chunk=chunk, rows_from_input=rows_from_input)
    in_specs = [col(COL_MQ), col(COL_MK), col(COL_MV), col(COL_MO), col(COL_MZ),
                pl.BlockSpec((1, chunk, LANES), lambda bi, i: (bi, i, 0)),
                gates_t_spec,
                _layer_spec((1, LANES), layer),
                _layer_spec((LANES, 1), layer),
                _layer_spec((1, w), layer),
                pl.BlockSpec((2, chunk, chunk), lambda bi, i: (0, 0, 0)),
                state_in((h, d, d)), state_in((h, d)), state_in((h, LANES)),
                pl.BlockSpec(memory_space=pl.ANY)]
    args = [proj4, proj4, proj4, proj4, proj4, gates3, gates_t, gate_bias, gate_bias_t, head_norm, triangles,
            c0, n0, m0, c_stack]
    return pl.pallas_call(
        kernel,
        grid=(b, nc),
        in_specs=in_specs,
        out_specs=[pl.BlockSpec((1, chunk, w), lambda bi, i: (bi, i, 0)),
                   pl.BlockSpec((None, 1, h, d, d), lambda bi, i: (layer, bi, 0, 0, 0)),
                   state_out((h, d)), state_out((h, LANES))],
        out_shape=[jax.ShapeDtypeStruct((b, t, w), BF16),
                   jax.ShapeDtypeStruct(c_stack.shape, F32),
                   jax.ShapeDtypeStruct((b, h, d), F32),
                   jax.ShapeDtypeStruct((b, h, LANES), F32)],
        input_output_aliases={len(args) - 1: 1},
        compiler_params=_params(("parallel", "arbitrary"), 48),
        name="mlstm",
    )(*args)


def _pool_prev_rows(hist_ref, halo_ref, first):
    hist = hist_ref[0].astype(F32)
    hist = jnp.concatenate([jnp.zeros((POOL_HALO - POOL_BUF, hist.shape[1]), F32), hist], axis=0)
    return jnp.where(first, hist, halo_ref[...].astype(F32))


def _pool_mix_tile(u, prev, z_ref, w_ref, sc_ref, pos):
    outs = []
    for gi, win in enumerate(POOL_WINDOWS):
        sl = slice(gi * POOL_GROUP_DIM, (gi + 1) * POOL_GROUP_DIM)
        ext = jnp.concatenate([prev[:, sl], u[:, sl]], axis=0)
        acc = ext
        span = 1
        while span < win:
            acc = acc + pltpu.roll(acc, span, 0)
            span *= 2
        cnt = jnp.minimum(pos + 1, win).astype(F32)
        mean = acc[POOL_HALO:, :] / cnt
        m = (mean - u[:, sl]).astype(BF16)
        y = jnp.dot(m, w_ref[gi], preferred_element_type=F32) * sc_ref[:, sl]
        outs.append((y * _silu(z_ref[:, sl].astype(F32))).astype(BF16))
    return jnp.concatenate(outs, axis=1)


def _pool_kernel(u_ref, halo_ref, hist_ref, z_ref, w_ref, sc_ref, y_ref, *, tp, pos0):
    i = pl.program_id(1)
    prev = _pool_prev_rows(hist_ref, halo_ref.at[0], i == 0)
    pos = pos0 + i * tp + lax.broadcasted_iota(jnp.int32, (tp, 1), 0)
    y_ref[0] = _pool_mix_tile(u_ref[0].astype(F32), prev, z_ref.at[0], w_ref, sc_ref, pos)


def _pool(proj4, hist, w_group, scale, layer, hist_layer, *, tp, pos0):
    _, b, t, w = proj4.shape
    halo_blocks = tp // POOL_HALO
    kernel = functools.partial(_pool_kernel, tp=tp, pos0=pos0)
    return pl.pallas_call(
        kernel,
        grid=(b, t // tp),
        in_specs=[pl.BlockSpec((None, 1, tp, w), lambda bi, i: (COL_PU, bi, i, 0)),
                  pl.BlockSpec((None, 1, POOL_HALO, w),
                               lambda bi, i: (COL_PU, bi, jnp.maximum(i * halo_blocks - 1, 0), 0)),
                  pl.BlockSpec((None, 1, POOL_BUF, w), lambda bi, i: (hist_layer, bi, 0, 0)),
                  pl.BlockSpec((None, 1, tp, w), lambda bi, i: (COL_PZ, bi, i, 0)),
                  _layer_spec((len(POOL_WINDOWS), POOL_GROUP_DIM, POOL_GROUP_DIM), layer),
                  _layer_spec((1, w), layer)],
        out_specs=pl.BlockSpec((1, tp, w), lambda bi, i: (bi, i, 0)),
        out_shape=jax.ShapeDtypeStruct((b, t, w), BF16),
        compiler_params=_params(("parallel", "arbitrary"), 32),
        name="pool",
    )(proj4, proj4, hist, proj4, w_group, scale)


def _merge_kernel(*refs):
    y_refs = refs[:N_BRANCH]
    gate_refs = refs[N_BRANCH:-4]
    x_ref, wb_ref, wo_ref, o_ref = refs[-4:]
    per_branch = len(gate_refs) // N_BRANCH
    merged = None
    for b in range(N_BRANCH):
        gate = jnp.concatenate([_sigmoid(r[...].astype(F32)) for r in gate_refs[b * per_branch:(b + 1) * per_branch]],
                               axis=1)
        term = gate * jnp.dot(y_refs[b][...], wb_ref[b], preferred_element_type=F32)
        merged = term if merged is None else merged + term
    o_ref[...] = x_ref[...] + jnp.dot(merged.astype(BF16), wo_ref[...], preferred_element_type=F32)


def _merge(y_pool, y_ml, y_att, proj, x2, w_branch, w_out, layer, tm):
    n, d = x2.shape
    w = BRANCH_WIDTH
    row = lambda width: pl.BlockSpec((tm, width), lambda i: (i, 0))
    n_gate_blocks = N_BRANCH * (d // w)
    gate_specs = [pl.BlockSpec((None, tm, w), lambda i, k=k: (COL_GATES + k, i, 0)) for k in range(n_gate_blocks)]
    return pl.pallas_call(
        _merge_kernel,
        grid=(n // tm,),
        in_specs=[row(w)] * N_BRANCH + gate_specs + [row(d), _layer_spec((N_BRANCH, w, d), layer, True),
                                                    _layer_spec((d, d), layer, True)],
        out_specs=row(d),
        out_shape=jax.ShapeDtypeStruct((n, d), F32),
        compiler_params=_params(("parallel",), 56),
        name="merge",
    )(y_pool, y_ml, y_att, *([proj] * n_gate_blocks), x2, w_branch, w_out)


def _ple_kernel(x_ref, p_ref, g_ref, wg_ref, wp_ref, o_ref):
    x = x_ref[...]
    h = (x * lax.rsqrt(jnp.mean(x * x, axis=-1, keepdims=True) + EPS) * g_ref[...]).astype(BF16)
    pg = _sigmoid(jnp.dot(h, wg_ref[...], preferred_element_type=F32))
    pp = jnp.dot(p_ref[...].astype(BF16), wp_ref[...], preferred_element_type=F32)
    o_ref[...] = x + pg * pp


def _ple(x2, p_all, gain, w_gate, w_proj, layer, tm):
    n, d = x2.shape
    dp = p_all.shape[2]
    return pl.pallas_call(
        _ple_kernel,
        grid=(n // tm,),
        in_specs=[pl.BlockSpec((tm, d), lambda i: (i, 0)),
                  pl.BlockSpec((None, tm, dp), lambda i: (layer, i, 0)),
                  _layer_spec((1, d), layer, True), _layer_spec((d, d), layer, True),
                  _layer_spec((dp, d), layer, True)],
        out_specs=pl.BlockSpec((tm, d), lambda i: (i, 0)),
        out_shape=jax.ShapeDtypeStruct((n, d), F32),
        compiler_params=_params(("parallel",), 48),
        name="ple",
    )(x2, p_all, gain, w_gate, w_proj)


def _tile(n, preferred):
    t = min(n, preferred)
    assert n % t == 0, (n, t)
    return t


def _prepare_weights(norm_mix, w_in, w_pool_group, pool_scale, b_ig, b_fg, ml_head_norm, att_q_norm, att_k_norm,
                     att_rel_bias, w_branch, w_out, ple_norm, w_ple_gate, w_ple_proj):
    depth, d, width = w_in.shape
    assert width == W_IN_GATES_END + (COL_GATES - COL_AQ) * BRANCH_WIDTH + N_BRANCH * d
    pad = LANES - W_IN_GATE_COLS
    w_in_t = jnp.swapaxes(w_in, 1, 2)
    w_gate = jnp.pad(w_in_t[:, W_IN_GATES_START:W_IN_GATES_END, :], ((0, 0), (0, pad), (0, 0))).astype(BF16)
    gate_bias = jnp.pad(jnp.concatenate([b_ig, b_fg], axis=1), ((0, 0), (0, pad)))[:, None, :].astype(F32)
    gq = att_q_norm * (ATT_HEAD_DIM ** -0.5 * LOG2_E)
    return dict(
        norm_g=norm_mix[:, None, :], w_main=_repack_w_in_t(w_in_t), w_gate=w_gate, gate_bias=gate_bias,
        w_pool=w_pool_group.astype(BF16), pool_scale=pool_scale[:, None, :], head_norm=ml_head_norm[:, None, :],
        gq=gq[:, None, :], gk=att_k_norm[:, None, :], rel_bias=att_rel_bias,
        w_branch=w_branch.astype(BF16), w_out=w_out.astype(BF16), ple_norm=ple_norm[:, None, :],
        w_ple_gate=w_ple_gate.astype(BF16), w_ple_proj=w_ple_proj.astype(BF16))


def _layer(x3, p_all, hist, c0, n0, m0, kv_cache, state_layer, pos0, layer, lw, c_stack):
    b, t, d = x3.shape
    n = b * t
    w = BRANCH_WIDTH
    prompt = kv_cache is None
    assert t % CHUNK == 0 and t >= POOL_BUF

    proj, gates, gates_t = _inproj(x3.reshape(n, d), lw["norm_g"], lw["w_main"], lw["w_gate"], lw["gq"], lw["gk"],
                                   layer, _tile(n, 1024))
    proj4 = proj.reshape(-1, b, t, w)

    if prompt:
        tq, group = _tile(t, ATT_WINDOW), 2 * CHUNK
        kprev, vprev, cache_layer = None, None, None
    else:
        tq, group = t, CHUNK
        assert t == CHUNK
        kprev, vprev = kv_cache
        cache_layer = state_layer
    bias = _attn_bias(lw["rel_bias"][layer], group)
    y_att = _attention(proj4, kprev, vprev, bias, tq=tq, group=group, cache_layer=cache_layer)

    m0b = jnp.broadcast_to(m0[..., None], m0.shape + (LANES,))
    y_ml, c1, n1, m1 = _mlstm(proj4, gates.reshape(b, t, LANES), gates_t, lw["gate_bias"], lw["head_norm"],
                              c0, n0, m0b, layer, state_layer, _tile(t, 256), c_stack)

    y_pool = _pool(proj4, hist, lw["w_pool"], lw["pool_scale"], layer, state_layer, tp=_tile(t, 512), pos0=pos0)

    x1 = _merge(y_pool.reshape(n, w), y_ml.reshape(n, w), y_att.reshape(n, w), proj, x3.reshape(n, d),
                lw["w_branch"], lw["w_out"], layer, _tile(n, 256))
    x2 = _ple(x1, p_all.reshape(p_all.shape[0], n, -1), lw["ple_norm"], lw["w_ple_gate"], lw["w_ple_proj"],
              layer, _tile(n, 512))

    keep = min(ATT_WINDOW, t) if prompt else t
    new_pool = proj4[COL_PU, :, t - POOL_BUF:].astype(F32)
    new_k = proj4[COL_AK, :, t - keep:].astype(F32).reshape(b, keep, ATT_HEADS, ATT_HEAD_DIM)
    new_v = proj4[COL_AV, :, t - keep:].astype(F32).reshape(b, keep, ATT_HEADS, ATT_HEAD_DIM)
    return x2.reshape(b, t, d), (new_pool, c1, n1, m1[:, :, 0], new_k, new_v)


def kernel(x_prompt, x_sample, cache_att_k, cache_att_v, state_pool, state_mlstm_c, state_mlstm_n, state_mlstm_m, p_prompt, p_sample, norm_mix, w_in, w_pool_group, pool_scale, b_ig, b_fg, ml_head_norm, att_q_norm, att_k_norm, att_rel_bias, w_branch, w_out, ple_norm, w_ple_gate, w_ple_proj):
    xp, xs = x_prompt, x_sample
    bp = x_prompt.shape[0]
    depth = w_in.shape[0]
    lw = _prepare_weights(norm_mix, w_in, w_pool_group, pool_scale, b_ig, b_fg, ml_head_norm, att_q_norm,
                          att_k_norm, att_rel_bias, w_branch, w_out, ple_norm, w_ple_gate, w_ple_proj)
    hist0 = jnp.zeros((1, bp, POOL_BUF, BRANCH_WIDTH), F32)
    c0 = jnp.zeros((1, bp, ML_HEADS, ML_HEAD_DIM, ML_HEAD_DIM), F32)
    n0 = jnp.zeros((1, bp, ML_HEADS, ML_HEAD_DIM), F32)
    m0 = jnp.zeros((1, bp, ML_HEADS), F32)
    sp = [[] for _ in range(6)]
    ss = [[] for _ in range(6)]
    c_p = jnp.zeros((depth,) + c0.shape[1:], F32)
    c_s = jnp.zeros(state_mlstm_c.shape, F32)
    for i in range(depth):
        xp, st_p = _layer(xp, p_prompt, hist0, c0, n0, m0, None, 0, 0, i, lw, c_p)
        xs, st_s = _layer(xs, p_sample, state_pool, state_mlstm_c, state_mlstm_n, state_mlstm_m,
                          (cache_att_k, cache_att_v), i, PAST_LEN, i, lw, c_s)
        c_p, c_s = st_p[1], st_s[1]
        for j in range(6):
            sp[j].append(st_p[j])
            ss[j].append(st_s[j])
    pool_p, _, n_p, m_p, k_p, v_p = [jnp.stack(a) if j != 1 else None for j, a in enumerate(sp)]
    pool_s, _, n_s, m_s, k_s, v_s = [jnp.stack(a) if j != 1 else None for j, a in enumerate(ss)]
    return (xp, xs, pool_p, pool_s, c_p, c_s, n_p, n_s, m_p, m_s, k_p, k_s, v_p, v_s)
```

```python
import functools

import numpy as np
import jax
import jax.numpy as jnp
from jax import lax
from jax.experimental import pallas as pl
from jax.experimental.pallas import tpu as pltpu

F32 = jnp.float32
BF16 = jnp.bfloat16

EPS = 1e-6
CHUNK = 64
PAST_LEN = 1024

POOL_WINDOWS = (2, 4, 8, 16)
POOL_GROUP_DIM = 256
POOL_BUF = 15
POOL_HALO = 16

ML_HEADS = 4
ML_HEAD_DIM = 256

ATT_HEADS = 8
ATT_HEAD_DIM = 128
ATT_WINDOW = 512
REL_CLIP = 256

BRANCH_WIDTH = 1024
LANES = 128

N_BRANCH = 3
COL_PU, COL_PZ, COL_MQ, COL_MK, COL_MV, COL_MO, COL_MZ, COL_AQ, COL_AK, COL_AV, COL_AZ = range(11)
COL_GATES = 11
W_IN_GATES_START = 7168
W_IN_GATES_END = 7176
W_IN_GATE_COLS = W_IN_GATES_END - W_IN_GATES_START

PROJ_DTYPE = jnp.bfloat16
NEG_BIG = -1e30
LOG2_E = 1.4426950408889634
LN_2 = 0.6931471805599453
MIB = 1024 * 1024


def _params(semantics, vmem_mib):
    return pltpu.CompilerParams(dimension_semantics=semantics, vmem_limit_bytes=vmem_mib * MIB)


def _layer_spec(tail, layer, single_buffer=False):
    index_map = lambda *_: (layer,) + (0,) * len(tail)
    if single_buffer:
        return pl.BlockSpec((None,) + tuple(tail), index_map, pipeline_mode=pl.Buffered(1))
    return pl.BlockSpec((None,) + tuple(tail), index_map)


def _sigmoid(x):
    return 1.0 / (1.0 + jnp.exp(-x))


def _silu(x):
    return x * _sigmoid(x)


def _nt_dot(a, b):
    return lax.dot_general(a, b, (((1,), (1,)), ((), ())), preferred_element_type=F32)


def _tn_dot(a, b):
    return lax.dot_general(a, b, (((0,), (0,)), ((), ())), preferred_element_type=F32)


def _repack_kernel(a_ref, b_ref, o_ref, *, blocks_before):
    j = pl.program_id(1)

    @pl.when(j < blocks_before)
    def _():
        o_ref[...] = a_ref[...].astype(BF16)

    @pl.when(j >= blocks_before)
    def _():
        shifted = jnp.concatenate([a_ref[W_IN_GATE_COLS:, :], b_ref[...]], axis=0)
        o_ref[...] = shifted.astype(BF16)


def _repack_w_in_t(w_in_t):
    depth, width, d = w_in_t.shape
    tn = BRANCH_WIDTH
    out_width = width - W_IN_GATE_COLS
    assert out_width % tn == 0 and W_IN_GATES_START % tn == 0
    kernel = functools.partial(_repack_kernel, blocks_before=W_IN_GATES_START // tn)
    return pl.pallas_call(
        kernel,
        grid=(depth, out_width // tn),
        in_specs=[pl.BlockSpec((None, tn, d), lambda l, j: (l, j, 0)),
                  pl.BlockSpec((None, W_IN_GATE_COLS, d), lambda l, j: (l, (j + 1) * (tn // W_IN_GATE_COLS), 0))],
        out_specs=pl.BlockSpec((None, tn, d), lambda l, j: (l, j, 0)),
        out_shape=jax.ShapeDtypeStruct((depth, out_width, d), BF16),
        compiler_params=_params(("parallel", "arbitrary"), 40),
        name="repack_w_in",
    )(w_in_t, w_in_t)


def _inproj_kernel(x_ref, g_ref, wa_ref, wb_ref, wg_ref, gq_ref, gk_ref, o_ref, og_ref, ogt_ref, h_ref,
                   *, blocks_before):
    j = pl.program_id(1)

    @pl.when(j == 0)
    def _():
        x = x_ref[...]
        ms = jnp.mean(x * x, axis=-1, keepdims=True)
        h = (x * lax.rsqrt(ms + EPS) * g_ref[...]).astype(BF16)
        h_ref[...] = h
        og_ref[...] = _nt_dot(h, wg_ref[...])
        ogt_ref[...] = _nt_dot(wg_ref[...], h)

    def weights(shifted):
        if shifted:
            return jnp.concatenate([wa_ref[W_IN_GATE_COLS:, :], wb_ref[...]], axis=0).astype(BF16)
        return wa_ref[...].astype(BF16)

    is_q = j == COL_AQ
    head_normed = jnp.logical_or(is_q, j == COL_AK)
    after = j >= blocks_before

    @pl.when(jnp.logical_not(after))
    def _():
        o_ref[...] = _nt_dot(h_ref[...], weights(False)).astype(o_ref.dtype)

    @pl.when(jnp.logical_and(after, jnp.logical_not(head_normed)))
    def _():
        o_ref[...] = _nt_dot(h_ref[...], weights(True)).astype(o_ref.dtype)

    @pl.when(head_normed)
    def _():
        acc = _nt_dot(h_ref[...], weights(True))
        gain = jnp.where(is_q, gq_ref[...], gk_ref[...])
        for h in range(ATT_HEADS):
            sl = slice(h * ATT_HEAD_DIM, (h + 1) * ATT_HEAD_DIM)
            a = acc[:, sl]
            r = lax.rsqrt(jnp.mean(a * a, axis=-1, keepdims=True) + EPS)
            o_ref[:, sl] = (a * r * gain).astype(o_ref.dtype)


def _inproj(x2, gain, w_main_t, w_gate_t, gq, gk, layer, tm):
    n, d = x2.shape
    nw = w_main_t.shape[1] - W_IN_GATE_COLS
    tn = BRANCH_WIDTH
    assert tn == ATT_HEADS * ATT_HEAD_DIM and nw % tn == 0 and W_IN_GATES_START % tn == 0
    blocks_before = W_IN_GATES_START // tn
    assert min(COL_AQ, COL_AK) >= blocks_before
    kernel = functools.partial(_inproj_kernel, blocks_before=blocks_before)
    return pl.pallas_call(
        kernel,
        grid=(n // tm, nw // tn),
        in_specs=[pl.BlockSpec((tm, d), lambda i, j: (i, 0)),
                  _layer_spec((1, d), layer),
                  pl.BlockSpec((None, tn, d), lambda i, j: (layer, j, 0)),
                  pl.BlockSpec((None, W_IN_GATE_COLS, d),
                               lambda i, j: (layer, (j + 1) * (tn // W_IN_GATE_COLS), 0)),
                  _layer_spec((LANES, d), layer),
                  _layer_spec((1, ATT_HEAD_DIM), layer),
                  _layer_spec((1, ATT_HEAD_DIM), layer)],
        out_specs=[pl.BlockSpec((None, tm, tn), lambda i, j: (j, i, 0)),
                   pl.BlockSpec((tm, LANES), lambda i, j: (i, 0)),
                   pl.BlockSpec((LANES, tm), lambda i, j: (0, i))],
        out_shape=[jax.ShapeDtypeStruct((nw // tn, n, tn), PROJ_DTYPE),
                   jax.ShapeDtypeStruct((n, LANES), F32),
                   jax.ShapeDtypeStruct((LANES, n), F32)],
        scratch_shapes=[pltpu.VMEM((tm, d), BF16)],
        compiler_params=_params(("parallel", "arbitrary"), 48),
        name="inproj",
    )(x2, gain, w_main_t, w_main_t, w_gate_t, gq, gk)


def _att_keys(group):
    return -(-(ATT_WINDOW + group) // LANES) * LANES


def _attn_bias(rel_bias, group):
    keys = _att_keys(group)
    period = keys + group
    m = np.arange(period)
    m = np.where(m <= keys, m, m - period)
    idx = np.clip(ATT_WINDOW - m, -REL_CLIP, REL_CLIP) + REL_CLIP
    vec = rel_bias[:, idx].astype(F32)
    heads = rel_bias.shape[0]
    bias = jnp.tile(vec, (1, group))[:, :group * (period - 1)].reshape(heads, group, period - 1)[:, :, :keys]
    i = np.arange(group)[:, None]
    j = np.arange(keys)[None, :]
    lo = (i // CHUNK) * CHUNK
    band = (j >= lo) & (j < lo + ATT_WINDOW + CHUNK)
    return jnp.where(jnp.asarray(band)[None], bias * LOG2_E, NEG_BIG)


def _attn_kernel(q_ref, kp_ref, kc_ref, vp_ref, vc_ref, az_ref, bias_ref, o_ref, kw_ref, vw_ref, s_ref, m_ref,
                 *, tq, group, first_prev_invalid, prev_by_head):
    rows = kw_ref.shape[0]
    keys = bias_ref.shape[2]
    if prev_by_head:
        for h in range(ATT_HEADS):
            sl = slice(h * ATT_HEAD_DIM, (h + 1) * ATT_HEAD_DIM)
            kw_ref[0:ATT_WINDOW, sl] = kp_ref[pl.ds(h, ATT_WINDOW, stride=ATT_HEADS), :].astype(BF16)
            vw_ref[0:ATT_WINDOW, sl] = vp_ref[pl.ds(h, ATT_WINDOW, stride=ATT_HEADS), :].astype(BF16)
    else:
        kw_ref[0:ATT_WINDOW, :] = kp_ref[0].astype(BF16)
        vw_ref[0:ATT_WINDOW, :] = vp_ref[0].astype(BF16)
    kw_ref[ATT_WINDOW:ATT_WINDOW + tq, :] = kc_ref[0].astype(BF16)
    vw_ref[ATT_WINDOW:ATT_WINDOW + tq, :] = vc_ref[0].astype(BF16)
    if rows > ATT_WINDOW + tq:
        pad = jnp.zeros((rows - ATT_WINDOW - tq, kw_ref.shape[1]), BF16)
        kw_ref[ATT_WINDOW + tq:rows, :] = pad
        vw_ref[ATT_WINDOW + tq:rows, :] = pad

    def attend(mask_prev):
        col = lax.broadcasted_iota(jnp.int32, (group, keys), 1)
        ones = jnp.ones((keys, ATT_HEAD_DIM), BF16)
        for g in range(tq // group):
            r0 = g * group
            for h in range(ATT_HEADS):
                sl = slice(h * ATT_HEAD_DIM, (h + 1) * ATT_HEAD_DIM)
                q = q_ref[0, r0:r0 + group, sl].astype(BF16)
                k = kw_ref[r0:r0 + keys, sl]
                s = _nt_dot(q, k) + bias_ref[h]
                if mask_prev:
                    s = jnp.where(col + r0 < ATT_WINDOW, NEG_BIG, s)
                s_ref[h] = s
                m_ref[h] = jnp.max(s, axis=-1, keepdims=True)
            for h in range(ATT_HEADS):
                sl = slice(h * ATT_HEAD_DIM, (h + 1) * ATT_HEAD_DIM)
                v1 = jnp.concatenate([vw_ref[r0:r0 + keys, sl], ones], axis=1)
                p = jnp.exp2(s_ref[h] - m_ref[h])
                o = jnp.dot(p.astype(BF16), v1, preferred_element_type=F32)
                o = o[:, :ATT_HEAD_DIM] / o[:, ATT_HEAD_DIM:]
                z = az_ref[0, r0:r0 + group, sl].astype(F32)
                o_ref[0, r0:r0 + group, sl] = (o * _silu(z)).astype(o_ref.dtype)

    if first_prev_invalid:
        first = pl.program_id(1) == 0
        pl.when(first)(functools.partial(attend, True))
        pl.when(jnp.logical_not(first))(functools.partial(attend, False))
    else:
        attend(False)


def _attention(proj4, kprev, vprev, bias, *, tq, group, cache_layer):
    _, b, t, w = proj4.shape
    nt = t // tq
    prompt = cache_layer is None
    if prompt:
        assert tq == ATT_WINDOW
        kprev = vprev = proj4
        kprev_spec = pl.BlockSpec((None, 1, ATT_WINDOW, w), lambda bi, i: (COL_AK, bi, jnp.maximum(i - 1, 0), 0))
        vprev_spec = pl.BlockSpec((None, 1, ATT_WINDOW, w), lambda bi, i: (COL_AV, bi, jnp.maximum(i - 1, 0), 0))
    else:
        assert nt == 1 and kprev.shape[2:] == (ATT_WINDOW, ATT_HEADS, ATT_HEAD_DIM)
        depth = kprev.shape[0]
        kprev = kprev.reshape(depth, b, ATT_WINDOW * ATT_HEADS, ATT_HEAD_DIM)
        vprev = vprev.reshape(depth, b, ATT_WINDOW * ATT_HEADS, ATT_HEAD_DIM)
        kprev_spec = pl.BlockSpec((None, None, ATT_WINDOW * ATT_HEADS, ATT_HEAD_DIM),
                                  lambda bi, i: (cache_layer, bi, 0, 0))
        vprev_spec = kprev_spec
    keys = bias.shape[2]
    rows = tq + keys - group
    cur = lambda c: pl.BlockSpec((None, 1, tq, w), lambda bi, i: (c, bi, i, 0))
    kernel = functools.partial(_attn_kernel, tq=tq, group=group, first_prev_invalid=prompt, prev_by_head=not prompt)
    return pl.pallas_call(
        kernel,
        grid=(b, nt),
        in_specs=[cur(COL_AQ), kprev_spec, cur(COL_AK), vprev_spec, cur(COL_AV), cur(COL_AZ),
                  pl.BlockSpec((ATT_HEADS, group, keys), lambda bi, i: (0, 0, 0))],
        out_specs=pl.BlockSpec((1, tq, w), lambda bi, i: (bi, i, 0)),
        out_shape=jax.ShapeDtypeStruct((b, t, w), BF16),
        scratch_shapes=[pltpu.VMEM((rows, w), BF16), pltpu.VMEM((rows, w), BF16),
                        pltpu.VMEM((ATT_HEADS, group, keys), F32), pltpu.VMEM((ATT_HEADS, group, 1), F32)],
        compiler_params=_params(("parallel", "arbitrary"), 48),
        name="attention",
    )(proj4, kprev, proj4, vprev, proj4, proj4, bias)


def _log_sigmoid(x):
    return jnp.minimum(x, 0.0) - jnp.log1p(jnp.exp(-jnp.abs(x)))


def _split3(x):
    hi = x.astype(BF16)
    rest = x - hi.astype(F32)
    mid = rest.astype(BF16)
    lo = (rest - mid.astype(F32)).astype(BF16)
    return hi, mid, lo


def _mlstm_kernel(q_ref, k_ref, v_ref, o_ref, z_ref, g_ref, gt_ref, gb_ref, gbt_ref, hn_ref, tri_ref,
                  c0_ref, n0_ref, m0_ref, c_stack_ref, y_ref, c_ref, n_ref, m_ref, *, chunk, rows_from_input):
    del c_stack_ref
    L = chunk
    D = ML_HEAD_DIM

    @pl.when(pl.program_id(1) == 0)
    def _():
        c_ref[...] = c0_ref[...]
        n_ref[...] = n0_ref[...]
        m_ref[...] = m0_ref[...]

    row = lax.broadcasted_iota(jnp.int32, (L, L), 0)
    col = lax.broadcasted_iota(jnp.int32, (L, L), 1)
    causal = col <= row

    n_gates = 2 * ML_HEADS
    gates = (g_ref[0] + gb_ref[...]) * LOG2_E
    if rows_from_input:
        gates_t = gt_ref[0:n_gates, :] + gbt_ref[0:n_gates, :]
        fill = (jnp.zeros((n_gates, L), BF16),)
        a_rows = jnp.dot(jnp.concatenate(_split3(_log_sigmoid(gates_t)) + fill, axis=0), tri_ref[1],
                         preferred_element_type=F32)
        a_rows = (a_rows[:n_gates] + a_rows[n_gates:2 * n_gates] + a_rows[2 * n_gates:3 * n_gates]) * LOG2_E
        gates_t = gates_t * LOG2_E
        pick = (lax.broadcasted_iota(jnp.int32, (4 * n_gates, LANES), 0) % n_gates
                == lax.broadcasted_iota(jnp.int32, (4 * n_gates, LANES), 1)).astype(BF16)
        a_cols = _tn_dot(jnp.concatenate(_split3(a_rows) + fill, axis=0), pick)
    else:
        logf = _log_sigmoid(g_ref[0] + gb_ref[...])
        a_cols = jnp.dot(tri_ref[0], jnp.concatenate(_split3(logf), axis=1), preferred_element_type=F32)
        a_cols = (a_cols[:, :LANES] + a_cols[:, LANES:2 * LANES] + a_cols[:, 2 * LANES:]) * LOG2_E
        eye = row == col

    ones = jnp.ones((L, LANES), BF16)
    k_scale = jnp.asarray(D ** -0.5, BF16)

    for hd in range(ML_HEADS):
        sl = slice(hd * D, (hd + 1) * D)
        ig_col = gates[:, hd:hd + 1]
        a_col = a_cols[:, ML_HEADS + hd:ML_HEADS + hd + 1]
        if rows_from_input:
            ig_row = gates_t[hd:hd + 1, :]
            a_row = a_rows[ML_HEADS + hd:ML_HEADS + hd + 1, :]
        else:
            ig_row = jnp.sum(jnp.where(eye, ig_col, 0.0), axis=0, keepdims=True)
            a_row = jnp.sum(jnp.where(eye, a_col, 0.0), axis=0, keepdims=True)
        b = a_col[L - 1:L, :]
        m_prev = m_ref[0, hd:hd + 1, 0:1] * LOG2_E
        c_prev = c_ref[0, hd]
        n_prev = n_ref[0, hd:hd + 1, :]

        logd = jnp.where(causal, a_col - a_row + ig_row, NEG_BIG)
        inter = a_col + m_prev
        m_row = jnp.maximum(inter, jnp.max(logd, axis=1, keepdims=True))
        dmat = jnp.exp2(logd - m_row)
        w_inter = jnp.exp2(inter - m_row)

        qb = q_ref[0, :, sl].astype(BF16)
        kb = k_ref[0, :, sl].astype(BF16) * k_scale
        v1 = jnp.concatenate([v_ref[0, :, sl].astype(BF16), ones], axis=1)
        c1 = jnp.concatenate([c_prev.astype(BF16), jnp.broadcast_to(n_prev.astype(BF16), (LANES, D))], axis=0)

        s = _nt_dot(qb, kb) * dmat
        sv = jnp.dot(s.astype(BF16), v1, preferred_element_type=F32)
        qc = _nt_dot(qb, c1)
        num = sv[:, :D] + w_inter * qc[:, :D]
        den = sv[:, D:] + w_inter * qc[:, D:]
        r = 1.0 / jnp.maximum(jnp.abs(den), jnp.exp2(-m_row))
        h = num * jnp.concatenate([r] * (D // LANES), axis=1)

        g_col = b - a_col + ig_col
        m_new = jnp.maximum(b + m_prev, jnp.max(g_col, axis=0, keepdims=True))
        wk = jnp.exp2(g_col - m_new)
        decay = jnp.exp2(b + m_prev - m_new)
        kw = kb * wk.astype(BF16)
        upd = _tn_dot(v1, kw)
        c_ref[0, hd] = decay * c_prev + upd[:D]
        n_ref[0, hd:hd + 1, :] = decay * n_prev + upd[D:D + 1]
        m_ref[0, hd:hd + 1, :] = jnp.broadcast_to(m_new * LN_2, (1, LANES))

        hm = h * _sigmoid(o_ref[0, :, sl].astype(F32))
        hm = hm * lax.rsqrt(jnp.mean(hm * hm, axis=-1, keepdims=True) + EPS) * hn_ref[:, sl]
        y_ref[0, :, sl] = (hm * _silu(z_ref[0, :, sl].astype(F32))).astype(y_ref.dtype)


def _mlstm(proj4, gates3, gates_t, gate_bias, head_norm, c0, n0, m0, layer, state_layer, chunk, c_stack):
    _, b, t, w = proj4.shape
    h, d = ML_HEADS, ML_HEAD_DIM
    nc = t // chunk
    rows_from_input = chunk % LANES == 0
    col = lambda c: pl.BlockSpec((None, 1, chunk, w), lambda bi, i: (c, bi, i, 0))
    state_in = lambda shape: pl.BlockSpec((None, 1) + shape, lambda bi, i: (state_layer, bi) + (0,) * len(shape))
    state_out = lambda shape: pl.BlockSpec((1,) + shape, lambda bi, i: (bi,) + (0,) * len(shape))
    if rows_from_input:
        gates_t_spec = pl.BlockSpec((LANES, chunk), lambda bi, i: (0, bi * nc + i))
    else:
        gates_t_spec = pl.BlockSpec((LANES, LANES), lambda bi, i: (0, 0))
    lower = jnp.tril(jnp.ones((chunk, chunk), BF16))
    triangles = jnp.stack([lower, lower.T])
    gate_bias_t = jnp.swapaxes(gate_bias, 1, 2)
    kernel = functools.partial(_mlstm_kernel, chunk=chunk, rows_from_input=rows_from_input)
    in_specs = [col(COL_MQ), col(COL_MK), col(COL_MV), col(COL_MO), col(COL_MZ),
                pl.BlockSpec((1, chunk, LANES), lambda bi, i: (bi, i, 0)),
                gates_t_spec,
                _layer_spec((1, LANES), layer),
                _layer_spec((LANES, 1), layer),
                _layer_spec((1, w), layer),
                pl.BlockSpec((2, chunk, chunk), lambda bi, i: (0, 0, 0)),
                state_in((h, d, d)), state_in((h, d)), state_in((h, LANES)),
                pl.BlockSpec(memory_space=pl.ANY)]
    args = [proj4, proj4, proj4, proj4, proj4, gates3, gates_t, gate_bias, gate_bias_t, head_norm, triangles,
            c0, n0, m0, c_stack]
    return pl.pallas_call(
        kernel,
        grid=(b, nc),
        in_specs=in_specs,
        out_specs=[pl.BlockSpec((1, chunk, w), lambda bi, i: (bi, i, 0)),
                   pl.BlockSpec((None, 1, h, d, d), lambda bi, i: (layer, bi, 0, 0, 0)),
                   state_out((h, d)), state_out((h, LANES))],
        out_shape=[jax.ShapeDtypeStruct((b, t, w), BF16),
                   jax.ShapeDtypeStruct(c_stack.shape, F32),
                   jax.ShapeDtypeStruct((b, h, d), F32),
                   jax.ShapeDtypeStruct((b, h, LANES), F32)],
        input_output_aliases={len(args) - 1: 1},
        compiler_params=_params(("parallel", "arbitrary"), 48),
        name="mlstm",
    )(*args)


def _pool_prev_rows(hist_ref, halo_ref, first):
    hist = hist_ref[0].astype(F32)
    hist = jnp.concatenate([jnp.zeros((POOL_HALO - POOL_BUF, hist.shape[1]), F32), hist], axis=0)
    return jnp.where(first, hist, halo_ref[...].astype(F32))


def _pool_mix_tile(u, prev, z_ref, w_ref, sc_ref, pos):
    outs = []
    for gi, win in enumerate(POOL_WINDOWS):
        sl = slice(gi * POOL_GROUP_DIM, (gi + 1) * POOL_GROUP_DIM)
        ext = jnp.concatenate([prev[:, sl], u[:, sl]], axis=0)
        acc = ext
        span = 1
        while span < win:
            acc = acc + pltpu.roll(acc, span, 0)
            span *= 2
        cnt = jnp.minimum(pos + 1, win).astype(F32)
        mean = acc[POOL_HALO:, :] / cnt
        m = (mean - u[:, sl]).astype(BF16)
        y = jnp.dot(m, w_ref[gi], preferred_element_type=F32) * sc_ref[:, sl]
        outs.append((y * _silu(z_ref[:, sl].astype(F32))).astype(BF16))
    return jnp.concatenate(outs, axis=1)


def _pool_kernel(u_ref, halo_ref, hist_ref, z_ref, w_ref, sc_ref, y_ref, *, tp, pos0):
    i = pl.program_id(1)
    prev = _pool_prev_rows(hist_ref, halo_ref.at[0], i == 0)
    pos = pos0 + i * tp + lax.broadcasted_iota(jnp.int32, (tp, 1), 0)
    y_ref[0] = _pool_mix_tile(u_ref[0].astype(F32), prev, z_ref.at[0], w_ref, sc_ref, pos)


def _pool(proj4, hist, w_group, scale, layer, hist_layer, *, tp, pos0):
    _, b, t, w = proj4.shape
    halo_blocks = tp // POOL_HALO
    kernel = functools.partial(_pool_kernel, tp=tp, pos0=pos0)
    return pl.pallas_call(
        kernel,
        grid=(b, t // tp),
        in_specs=[pl.BlockSpec((None, 1, tp, w), lambda bi, i: (COL_PU, bi, i, 0)),
                  pl.BlockSpec((None, 1, POOL_HALO, w),
                               lambda bi, i: (COL_PU, bi, jnp.maximum(i * halo_blocks - 1, 0), 0)),
                  pl.BlockSpec((None, 1, POOL_BUF, w), lambda bi, i: (hist_layer, bi, 0, 0)),
                  pl.BlockSpec((None, 1, tp, w), lambda bi, i: (COL_PZ, bi, i, 0)),
                  _layer_spec((len(POOL_WINDOWS), POOL_GROUP_DIM, POOL_GROUP_DIM), layer),
                  _layer_spec((1, w), layer)],
        out_specs=pl.BlockSpec((1, tp, w), lambda bi, i: (bi, i, 0)),
        out_shape=jax.ShapeDtypeStruct((b, t, w), BF16),
        compiler_params=_params(("parallel", "arbitrary"), 32),
        name="pool",
    )(proj4, proj4, hist, proj4, w_group, scale)


def _merge_kernel(*refs):
    y_refs = refs[:N_BRANCH]
    gate_refs = refs[N_BRANCH:-4]
    x_ref, wb_ref, wo_ref, o_ref = refs[-4:]
    per_branch = len(gate_refs) // N_BRANCH
    merged = None
    for b in range(N_BRANCH):
        gate = jnp.concatenate([_sigmoid(r[...].astype(F32)) for r in gate_refs[b * per_branch:(b + 1) * per_branch]],
                               axis=1)
        term = gate * jnp.dot(y_refs[b][...], wb_ref[b], preferred_element_type=F32)
        merged = term if merged is None else merged + term
    o_ref[...] = x_ref[...] + jnp.dot(merged.astype(BF16), wo_ref[...], preferred_element_type=F32)


def _merge(y_pool, y_ml, y_att, proj, x2, w_branch, w_out, layer, tm):
    n, d = x2.shape
    w = BRANCH_WIDTH
    row = lambda width: pl.BlockSpec((tm, width), lambda i: (i, 0))
    n_gate_blocks = N_BRANCH * (d // w)
    gate_specs = [pl.BlockSpec((None, tm, w), lambda i, k=k: (COL_GATES + k, i, 0)) for k in range(n_gate_blocks)]
    return pl.pallas_call(
        _merge_kernel,
        grid=(n // tm,),
        in_specs=[row(w)] * N_BRANCH + gate_specs + [row(d), _layer_spec((N_BRANCH, w, d), layer, True),
                                                    _layer_spec((d, d), layer, True)],
        out_specs=row(d),
        out_shape=jax.ShapeDtypeStruct((n, d), F32),
        compiler_params=_params(("parallel",), 56),
        name="merge",
    )(y_pool, y_ml, y_att, *([proj] * n_gate_blocks), x2, w_branch, w_out)


def _ple_kernel(x_ref, p_ref, g_ref, wg_ref, wp_ref, o_ref):
    x = x_ref[...]
    h = (x * lax.rsqrt(jnp.mean(x * x, axis=-1, keepdims=True) + EPS) * g_ref[...]).astype(BF16)
    pg = _sigmoid(jnp.dot(h, wg_ref[...], preferred_element_type=F32))
    pp = jnp.dot(p_ref[...].astype(BF16), wp_ref[...], preferred_element_type=F32)
    o_ref[...] = x + pg * pp


def _ple(x2, p_all, gain, w_gate, w_proj, layer, tm):
    n, d = x2.shape
    dp = p_all.shape[2]
    return pl.pallas_call(
        _ple_kernel,
        grid=(n // tm,),
        in_specs=[pl.BlockSpec((tm, d), lambda i: (i, 0)),
                  pl.BlockSpec((None, tm, dp), lambda i: (layer, i, 0)),
                  _layer_spec((1, d), layer, True), _layer_spec((d, d), layer, True),
                  _layer_spec((dp, d), layer, True)],
        out_specs=pl.BlockSpec((tm, d), lambda i: (i, 0)),
        out_shape=jax.ShapeDtypeStruct((n, d), F32),
        compiler_params=_params(("parallel",), 48),
        name="ple",
    )(x2, p_all, gain, w_gate, w_proj)


def _tile(n, preferred):
    t = min(n, preferred)
    assert n % t == 0, (n, t)
    return t


def _prepare_weights(norm_mix, w_in, w_pool_group, pool_scale, b_ig, b_fg, ml_head_norm, att_q_norm, att_k_norm,
                     att_rel_bias, w_branch, w_out, ple_norm, w_ple_gate, w_ple_proj):
    depth, d, width = w_in.shape
    assert width == W_IN_GATES_END + (COL_GATES - COL_AQ) * BRANCH_WIDTH + N_BRANCH * d
    pad = LANES - W_IN_GATE_COLS
    w_in_t = jnp.swapaxes(w_in, 1, 2)
    w_gate = jnp.pad(w_in_t[:, W_IN_GATES_START:W_IN_GATES_END, :], ((0, 0), (0, pad), (0, 0))).astype(BF16)
    gate_bias = jnp.pad(jnp.concatenate([b_ig, b_fg], axis=1), ((0, 0), (0, pad)))[:, None, :].astype(F32)
    gq = att_q_norm * (ATT_HEAD_DIM ** -0.5 * LOG2_E)
    return dict(
        norm_g=norm_mix[:, None, :], w_main=w_in_t, w_gate=w_gate, gate_bias=gate_bias,
        w_pool=w_pool_group.astype(BF16), pool_scale=pool_scale[:, None, :], head_norm=ml_head_norm[:, None, :],
        gq=gq[:, None, :], gk=att_k_norm[:, None, :], rel_bias=att_rel_bias,
        w_branch=w_branch.astype(BF16), w_out=w_out.astype(BF16), ple_norm=ple_norm[:, None, :],
        w_ple_gate=w_ple_gate.astype(BF16), w_ple_proj=w_ple_proj.astype(BF16))


def _layer(x3, p_all, hist, c0, n0, m0, kv_cache, state_layer, pos0, layer, lw, c_stack):
    b, t, d = x3.shape
    n = b * t
    w = BRANCH_WIDTH
    prompt = kv_cache is None
    assert t % CHUNK == 0 and t >= POOL_BUF

    proj, gates, gates_t = _inproj(x3.reshape(n, d), lw["norm_g"], lw["w_main"], lw["w_gate"], lw["gq"], lw["gk"],
                                   layer, _tile(n, 1024))
    proj4 = proj.reshape(-1, b, t, w)

    if prompt:
        tq, group = _tile(t, ATT_WINDOW), 2 * CHUNK
        kprev, vprev, cache_layer = None, None, None
    else:
        tq, group = t, CHUNK
        assert t == CHUNK
        kprev, vprev = kv_cache
        cache_layer = state_layer
    bias = _attn_bias(lw["rel_bias"][layer], group)
    y_att = _attention(proj4, kprev, vprev, bias, tq=tq, group=group, cache_layer=cache_layer)

    m0b = jnp.broadcast_to(m0[..., None], m0.shape + (LANES,))
    y_ml, c1, n1, m1 = _mlstm(proj4, gates.reshape(b, t, LANES), gates_t, lw["gate_bias"], lw["head_norm"],
                              c0, n0, m0b, layer, state_layer, _tile(t, 256), c_stack)

    y_pool = _pool(proj4, hist, lw["w_pool"], lw["pool_scale"], layer, state_layer, tp=_tile(t, 512), pos0=pos0)

    x1 = _merge(y_pool.reshape(n, w), y_ml.reshape(n, w), y_att.reshape(n, w), proj, x3.reshape(n, d),
                lw["w_branch"], lw["w_out"], layer, _tile(n, 256))
    x2 = _ple(x1, p_all.reshape(p_all.shape[0], n, -1), lw["ple_norm"], lw["w_ple_gate"], lw["w_ple_proj"],
              layer, _tile(n, 512))

    keep = min(ATT_WINDOW, t) if prompt else t
    new_pool = proj4[COL_PU, :, t - POOL_BUF:].astype(F32)
    new_k = proj4[COL_AK, :, t - keep:].astype(F32).reshape(b, keep, ATT_HEADS, ATT_HEAD_DIM)
    new_v = proj4[COL_AV, :, t - keep:].astype(F32).reshape(b, keep, ATT_HEADS, ATT_HEAD_DIM)
    return x2.reshape(b, t, d), (new_pool, c1, n1, m1[:, :, 0], new_k, new_v)


def kernel(x_prompt, x_sample, cache_att_k, cache_att_v, state_pool, state_mlstm_c, state_mlstm_n, state_mlstm_m, p_prompt, p_sample, norm_mix, w_in, w_pool_group, pool_scale, b_ig, b_fg, ml_head_norm, att_q_norm, att_k_norm, att_rel_bias, w_branch, w_out, ple_norm, w_ple_gate, w_ple_proj):
    xp, xs = x_prompt, x_sample
    bp = x_prompt.shape[0]
    depth = w_in.shape[0]
    lw = _prepare_weights(norm_mix, w_in, w_pool_group, pool_scale, b_ig, b_fg, ml_head_norm, att_q_norm,
                          att_k_norm, att_rel_bias, w_branch, w_out, ple_norm, w_ple_gate, w_ple_proj)
    hist0 = jnp.zeros((1, bp, POOL_BUF, BRANCH_WIDTH), F32)
    c0 = jnp.zeros((1, bp, ML_HEADS, ML_HEAD_DIM, ML_HEAD_DIM), F32)
    n0 = jnp.zeros((1, bp, ML_HEADS, ML_HEAD_DIM), F32)
    m0 = jnp.zeros((1, bp, ML_HEADS), F32)
    sp = [[] for _ in range(6)]
    ss = [[] for _ in range(6)]
    c_p = jnp.zeros((depth,) + c0.shape[1:], F32)
    c_s = jnp.zeros(state_mlstm_c.shape, F32)
    for i in range(depth):
        xp, st_p = _layer(xp, p_prompt, hist0, c0, n0, m0, None, 0, 0, i, lw, c_p)
        xs, st_s = _layer(xs, p_sample, state_pool, state_mlstm_c, state_mlstm_n, state_mlstm_m,
                          (cache_att_k, cache_att_v), i, PAST_LEN, i, lw, c_s)
        c_p, c_s = st_p[1], st_s[1]
        for j in range(6):
            sp[j].append(st_p[j])
            ss[j].append(st_s[j])
    pool_p, _, n_p, m_p, k_p, v_p = [jnp.stack(a) if j != 1 else None for j, a in enumerate(sp)]
    pool_s, _, n_s, m_s, k_s, v_s = [jnp.stack(a) if j != 1 else None for j, a in enumerate(ss)]
    return (xp, xs, pool_p, pool_s, c_p, c_s, n_p, n_s, m_p, m_s, k_p, k_s, v_p, v_s)
```

```python
import functools

import numpy as np
import jax
import jax.numpy as jnp
from jax import lax
from jax.experimental import pallas as pl
from jax.experimental.pallas import tpu as pltpu

F32 = jnp.float32
BF16 = jnp.bfloat16

EPS = 1e-6
CHUNK = 64
PAST_LEN = 1024

POOL_WINDOWS = (2, 4, 8, 16)
POOL_GROUP_DIM = 256
POOL_BUF = 15
POOL_HALO = 16

ML_HEADS = 4
ML_HEAD_DIM = 256

ATT_HEADS = 8
ATT_HEAD_DIM = 128
ATT_WINDOW = 512
REL_CLIP = 256

BRANCH_WIDTH = 1024
LANES = 128

N_BRANCH = 3
COL_PU, COL_PZ, COL_MQ, COL_MK, COL_MV, COL_MO, COL_MZ, COL_AQ, COL_AK, COL_AV, COL_AZ = range(11)
COL_GATES = 11
W_IN_GATES_START = 7168
W_IN_GATES_END = 7176
W_IN_GATE_COLS = W_IN_GATES_END - W_IN_GATES_START

PROJ_DTYPE = jnp.bfloat16
NEG_BIG = -1e30
LOG2_E = 1.4426950408889634
LN_2 = 0.6931471805599453
MIB = 1024 * 1024


def _params(semantics, vmem_mib):
    return pltpu.CompilerParams(dimension_semantics=semantics, vmem_limit_bytes=vmem_mib * MIB)


def _layer_spec(tail, layer, single_buffer=False):
    index_map = lambda *_: (layer,) + (0,) * len(tail)
    if single_buffer:
        return pl.BlockSpec((None,) + tuple(tail), index_map, pipeline_mode=pl.Buffered(1))
    return pl.BlockSpec((None,) + tuple(tail), index_map)


def _sigmoid(x):
    return 1.0 / (1.0 + jnp.exp(-x))


def _silu(x):
    return x * _sigmoid(x)


def _nt_dot(a, b):
    return lax.dot_general(a, b, (((1,), (1,)), ((), ())), preferred_element_type=F32)


def _tn_dot(a, b):
    return lax.dot_general(a, b, (((0,), (0,)), ((), ())), preferred_element_type=F32)


def _inproj_kernel(x_ref, g_ref, wa_ref, wb_ref, wg_ref, gq_ref, gk_ref, o_ref, og_ref, ogt_ref, h_ref,
                   *, blocks_before):
    j = pl.program_id(1)

    @pl.when(j == 0)
    def _():
        x = x_ref[...]
        ms = jnp.mean(x * x, axis=-1, keepdims=True)
        h = (x * lax.rsqrt(ms + EPS) * g_ref[...]).astype(BF16)
        h_ref[...] = h
        og_ref[...] = _nt_dot(h, wg_ref[...])
        ogt_ref[...] = _nt_dot(wg_ref[...], h)

    def weights(shifted):
        if shifted:
            return jnp.concatenate([wa_ref[W_IN_GATE_COLS:, :], wb_ref[...]], axis=0).astype(BF16)
        return wa_ref[...].astype(BF16)

    is_q = j == COL_AQ
    head_normed = jnp.logical_or(is_q, j == COL_AK)
    after = j >= blocks_before

    @pl.when(jnp.logical_not(after))
    def _():
        o_ref[...] = _nt_dot(h_ref[...], weights(False)).astype(o_ref.dtype)

    @pl.when(jnp.logical_and(after, jnp.logical_not(head_normed)))
    def _():
        o_ref[...] = _nt_dot(h_ref[...], weights(True)).astype(o_ref.dtype)

    @pl.when(head_normed)
    def _():
        acc = _nt_dot(h_ref[...], weights(True))
        gain = jnp.where(is_q, gq_ref[...], gk_ref[...])
        for h in range(ATT_HEADS):
            sl = slice(h * ATT_HEAD_DIM, (h + 1) * ATT_HEAD_DIM)
            a = acc[:, sl]
            r = lax.rsqrt(jnp.mean(a * a, axis=-1, keepdims=True) + EPS)
            o_ref[:, sl] = (a * r * gain).astype(o_ref.dtype)


def _inproj(x2, gain, w_main_t, w_gate_t, gq, gk, layer, tm):
    n, d = x2.shape
    nw = w_main_t.shape[1] - W_IN_GATE_COLS
    tn = BRANCH_WIDTH
    assert tn == ATT_HEADS * ATT_HEAD_DIM and nw % tn == 0 and W_IN_GATES_START % tn == 0
    blocks_before = W_IN_GATES_START // tn
    assert min(COL_AQ, COL_AK) >= blocks_before
    kernel = functools.partial(_inproj_kernel, blocks_before=blocks_before)
    return pl.pallas_call(
        kernel,
        grid=(n // tm, nw // tn),
        in_specs=[pl.BlockSpec((tm, d), lambda i, j: (i, 0)),
                  _layer_spec((1, d), layer),
                  pl.BlockSpec((None, tn, d), lambda i, j: (layer, j, 0)),
                  pl.BlockSpec((None, W_IN_GATE_COLS, d),
                               lambda i, j: (layer, (j + 1) * (tn // W_IN_GATE_COLS), 0)),
                  _layer_spec((LANES, d), layer),
                  _layer_spec((1, ATT_HEAD_DIM), layer),
                  _layer_spec((1, ATT_HEAD_DIM), layer)],
        out_specs=[pl.BlockSpec((None, tm, tn), lambda i, j: (j, i, 0)),
                   pl.BlockSpec((tm, LANES), lambda i, j: (i, 0)),
                   pl.BlockSpec((LANES, tm), lambda i, j: (0, i))],
        out_shape=[jax.ShapeDtypeStruct((nw // tn, n, tn), PROJ_DTYPE),
                   jax.ShapeDtypeStruct((n, LANES), F32),
                   jax.ShapeDtypeStruct((LANES, n), F32)],
        scratch_shapes=[pltpu.VMEM((tm, d), BF16)],
        compiler_params=_params(("parallel", "arbitrary"), 48),
        name="inproj",
    )(x2, gain, w_main_t, w_main_t, w_gate_t, gq, gk)


def _att_keys(group):
    return -(-(ATT_WINDOW + group) // LANES) * LANES


def _attn_bias(rel_bias, group):
    keys = _att_keys(group)
    period = keys + group
    m = np.arange(period)
    m = np.where(m <= keys, m, m - period)
    idx = np.clip(ATT_WINDOW - m, -REL_CLIP, REL_CLIP) + REL_CLIP
    vec = rel_bias[:, idx].astype(F32)
    heads = rel_bias.shape[0]
    bias = jnp.tile(vec, (1, group))[:, :group * (period - 1)].reshape(heads, group, period - 1)[:, :, :keys]
    i = np.arange(group)[:, None]
    j = np.arange(keys)[None, :]
    lo = (i // CHUNK) * CHUNK
    band = (j >= lo) & (j < lo + ATT_WINDOW + CHUNK)
    return jnp.where(jnp.asarray(band)[None], bias * LOG2_E, NEG_BIG)


def _attn_kernel(q_ref, kp_ref, kc_ref, vp_ref, vc_ref, az_ref, bias_ref, o_ref, kw_ref, vw_ref, s_ref, m_ref,
                 *, tq, group, first_prev_invalid, prev_by_head):
    rows = kw_ref.shape[0]
    keys = bias_ref.shape[2]
    if prev_by_head:
        for h in range(ATT_HEADS):
            sl = slice(h * ATT_HEAD_DIM, (h + 1) * ATT_HEAD_DIM)
            kw_ref[0:ATT_WINDOW, sl] = kp_ref[pl.ds(h, ATT_WINDOW, stride=ATT_HEADS), :].astype(BF16)
            vw_ref[0:ATT_WINDOW, sl] = vp_ref[pl.ds(h, ATT_WINDOW, stride=ATT_HEADS), :].astype(BF16)
    else:
        kw_ref[0:ATT_WINDOW, :] = kp_ref[0].astype(BF16)
        vw_ref[0:ATT_WINDOW, :] = vp_ref[0].astype(BF16)
    kw_ref[ATT_WINDOW:ATT_WINDOW + tq, :] = kc_ref[0].astype(BF16)
    vw_ref[ATT_WINDOW:ATT_WINDOW + tq, :] = vc_ref[0].astype(BF16)
    if rows > ATT_WINDOW + tq:
        pad = jnp.zeros((rows - ATT_WINDOW - tq, kw_ref.shape[1]), BF16)
        kw_ref[ATT_WINDOW + tq:rows, :] = pad
        vw_ref[ATT_WINDOW + tq:rows, :] = pad

    def attend(mask_prev):
        col = lax.broadcasted_iota(jnp.int32, (group, keys), 1)
        ones = jnp.ones((keys, ATT_HEAD_DIM), BF16)
        for g in range(tq // group):
            r0 = g * group
            for h in range(ATT_HEADS):
                sl = slice(h * ATT_HEAD_DIM, (h + 1) * ATT_HEAD_DIM)
                q = q_ref[0, r0:r0 + group, sl].astype(BF16)
                k = kw_ref[r0:r0 + keys, sl]
                s = _nt_dot(q, k) + bias_ref[h]
                if mask_prev:
                    s = jnp.where(col + r0 < ATT_WINDOW, NEG_BIG, s)
                s_ref[h] = s
                m_ref[h] = jnp.max(s, axis=-1, keepdims=True)
            for h in range(ATT_HEADS):
                sl = slice(h * ATT_HEAD_DIM, (h + 1) * ATT_HEAD_DIM)
                v1 = jnp.concatenate([vw_ref[r0:r0 + keys, sl], ones], axis=1)
                p = jnp.exp2(s_ref[h] - m_ref[h])
                o = jnp.dot(p.astype(BF16), v1, preferred_element_type=F32)
                o = o[:, :ATT_HEAD_DIM] / o[:, ATT_HEAD_DIM:]
                z = az_ref[0, r0:r0 + group, sl].astype(F32)
                o_ref[0, r0:r0 + group, sl] = (o * _silu(z)).astype(o_ref.dtype)

    if first_prev_invalid:
        first = pl.program_id(1) == 0
        pl.when(first)(functools.partial(attend, True))
        pl.when(jnp.logical_not(first))(functools.partial(attend, False))
    else:
        attend(False)


def _attention(proj4, kprev, vprev, bias, *, tq, group, cache_layer):
    _, b, t, w = proj4.shape
    nt = t // tq
    prompt = cache_layer is None
    if prompt:
        assert tq == ATT_WINDOW
        kprev = vprev = proj4
        kprev_spec = pl.BlockSpec((None, 1, ATT_WINDOW, w), lambda bi, i: (COL_AK, bi, jnp.maximum(i - 1, 0), 0))
        vprev_spec = pl.BlockSpec((None, 1, ATT_WINDOW, w), lambda bi, i: (COL_AV, bi, jnp.maximum(i - 1, 0), 0))
    else:
        assert nt == 1 and kprev.shape[2:] == (ATT_WINDOW, ATT_HEADS, ATT_HEAD_DIM)
        depth = kprev.shape[0]
        kprev = kprev.reshape(depth, b, ATT_WINDOW * ATT_HEADS, ATT_HEAD_DIM)
        vprev = vprev.reshape(depth, b, ATT_WINDOW * ATT_HEADS, ATT_HEAD_DIM)
        kprev_spec = pl.BlockSpec((None, None, ATT_WINDOW * ATT_HEADS, ATT_HEAD_DIM),
                                  lambda bi, i: (cache_layer, bi, 0, 0))
        vprev_spec = kprev_spec
    keys = bias.shape[2]
    rows = tq + keys - group
    cur = lambda c: pl.BlockSpec((None, 1, tq, w), lambda bi, i: (c, bi, i, 0))
    kernel = functools.partial(_attn_kernel, tq=tq, group=group, first_prev_invalid=prompt, prev_by_head=not prompt)
    return pl.pallas_call(
        kernel,
        grid=(b, nt),
        in_specs=[cur(COL_AQ), kprev_spec, cur(COL_AK), vprev_spec, cur(COL_AV), cur(COL_AZ),
                  pl.BlockSpec((ATT_HEADS, group, keys), lambda bi, i: (0, 0, 0))],
        out_specs=pl.BlockSpec((1, tq, w), lambda bi, i: (bi, i, 0)),
        out_shape=jax.ShapeDtypeStruct((b, t, w), BF16),
        scratch_shapes=[pltpu.VMEM((rows, w), BF16), pltpu.VMEM((rows, w), BF16),
                        pltpu.VMEM((ATT_HEADS, group, keys), F32), pltpu.VMEM((ATT_HEADS, group, 1), F32)],
        compiler_params=_params(("parallel", "arbitrary"), 48),
        name="attention",
    )(proj4, kprev, proj4, vprev, proj4, proj4, bias)


def _log_sigmoid(x):
    return jnp.minimum(x, 0.0) - jnp.log1p(jnp.exp(-jnp.abs(x)))


def _split3(x):
    hi = x.astype(BF16)
    rest = x - hi.astype(F32)
    mid = rest.astype(BF16)
    lo = (rest - mid.astype(F32)).astype(BF16)
    return hi, mid, lo


def _mlstm_kernel(q_ref, k_ref, v_ref, o_ref, z_ref, g_ref, gt_ref, gb_ref, gbt_ref, hn_ref, tri_ref,
                  c0_ref, n0_ref, m0_ref, c_stack_ref, y_ref, c_ref, n_ref, m_ref, *, chunk, rows_from_input):
    del c_stack_ref
    L = chunk
    D = ML_HEAD_DIM

    @pl.when(pl.program_id(1) == 0)
    def _():
        c_ref[...] = c0_ref[...]
        n_ref[...] = n0_ref[...]
        m_ref[...] = m0_ref[...]

    row = lax.broadcasted_iota(jnp.int32, (L, L), 0)
    col = lax.broadcasted_iota(jnp.int32, (L, L), 1)
    causal = col <= row

    n_gates = 2 * ML_HEADS
    gates = (g_ref[0] + gb_ref[...]) * LOG2_E
    if rows_from_input:
        gates_t = gt_ref[0:n_gates, :] + gbt_ref[0:n_gates, :]
        fill = (jnp.zeros((n_gates, L), BF16),)
        a_rows = jnp.dot(jnp.concatenate(_split3(_log_sigmoid(gates_t)) + fill, axis=0), tri_ref[1],
                         preferred_element_type=F32)
        a_rows = (a_rows[:n_gates] + a_rows[n_gates:2 * n_gates] + a_rows[2 * n_gates:3 * n_gates]) * LOG2_E
        gates_t = gates_t * LOG2_E
        pick = (lax.broadcasted_iota(jnp.int32, (4 * n_gates, LANES), 0) % n_gates
                == lax.broadcasted_iota(jnp.int32, (4 * n_gates, LANES), 1)).astype(BF16)
        a_cols = _tn_dot(jnp.concatenate(_split3(a_rows) + fill, axis=0), pick)
    else:
        logf = _log_sigmoid(g_ref[0] + gb_ref[...])
        a_cols = jnp.dot(tri_ref[0], jnp.concatenate(_split3(logf), axis=1), preferred_element_type=F32)
        a_cols = (a_cols[:, :LANES] + a_cols[:, LANES:2 * LANES] + a_cols[:, 2 * LANES:]) * LOG2_E
        eye = row == col

    ones = jnp.ones((L, LANES), BF16)
    k_scale = jnp.asarray(D ** -0.5, BF16)

    for hd in range(ML_HEADS):
        sl = slice(hd * D, (hd + 1) * D)
        ig_col = gates[:, hd:hd + 1]
        a_col = a_cols[:, ML_HEADS + hd:ML_HEADS + hd + 1]
        if rows_from_input:
            ig_row = gates_t[hd:hd + 1, :]
            a_row = a_rows[ML_HEADS + hd:ML_HEADS + hd + 1, :]
        else:
            ig_row = jnp.sum(jnp.where(eye, ig_col, 0.0), axis=0, keepdims=True)
            a_row = jnp.sum(jnp.where(eye, a_col, 0.0), axis=0, keepdims=True)
        b = a_col[L - 1:L, :]
        m_prev = m_ref[0, hd:hd + 1, 0:1] * LOG2_E
        c_prev = c_ref[0, hd]
        n_prev = n_ref[0, hd:hd + 1, :]

        logd = jnp.where(causal, a_col - a_row + ig_row, NEG_BIG)
        inter = a_col + m_prev
        m_row = jnp.maximum(inter, jnp.max(logd, axis=1, keepdims=True))
        dmat = jnp.exp2(logd - m_row)
        w_inter = jnp.exp2(inter - m_row)

        qb = q_ref[0, :, sl].astype(BF16)
        kb = k_ref[0, :, sl].astype(BF16) * k_scale
        v1 = jnp.concatenate([v_ref[0, :, sl].astype(BF16), ones], axis=1)
        c1 = jnp.concatenate([c_prev.astype(BF16), jnp.broadcast_to(n_prev.astype(BF16), (LANES, D))], axis=0)

        s = _nt_dot(qb, kb) * dmat
        sv = jnp.dot(s.astype(BF16), v1, preferred_element_type=F32)
        qc = _nt_dot(qb, c1)
        num = sv[:, :D] + w_inter * qc[:, :D]
        den = sv[:, D:] + w_inter * qc[:, D:]
        r = 1.0 / jnp.maximum(jnp.abs(den), jnp.exp2(-m_row))
        h = num * jnp.concatenate([r] * (D // LANES), axis=1)

        g_col = b - a_col + ig_col
        m_new = jnp.maximum(b + m_prev, jnp.max(g_col, axis=0, keepdims=True))
        wk = jnp.exp2(g_col - m_new)
        decay = jnp.exp2(b + m_prev - m_new)
        kw = kb * wk.astype(BF16)
        upd = _tn_dot(v1, kw)
        c_ref[0, hd] = decay * c_prev + upd[:D]
        n_ref[0, hd:hd + 1, :] = decay * n_prev + upd[D:D + 1]
        m_ref[0, hd:hd + 1, :] = jnp.broadcast_to(m_new * LN_2, (1, LANES))

        hm = h * _sigmoid(o_ref[0, :, sl].astype(F32))
        hm = hm * lax.rsqrt(jnp.mean(hm * hm, axis=-1, keepdims=True) + EPS) * hn_ref[:, sl]
        y_ref[0, :, sl] = (hm * _silu(z_ref[0, :, sl].astype(F32))).astype(y_ref.dtype)


def _mlstm(proj4, gates3, gates_t, gate_bias, head_norm, c0, n0, m0, layer, state_layer, chunk, c_stack):
    _, b, t, w = proj4.shape
    h, d = ML_HEADS, ML_HEAD_DIM
    nc = t // chunk
    rows_from_input = chunk % LANES == 0
    col = lambda c: pl.BlockSpec((None, 1, chunk, w), lambda bi, i: (c, bi, i, 0))
    state_in = lambda shape: pl.BlockSpec((None, 1) + shape, lambda bi, i: (state_layer, bi) + (0,) * len(shape))
    state_out = lambda shape: pl.BlockSpec((1,) + shape, lambda bi, i: (bi,) + (0,) * len(shape))
    if rows_from_input:
        gates_t_spec = pl.BlockSpec((LANES, chunk), lambda bi, i: (0, bi * nc + i))
    else:
        gates_t_spec = pl.BlockSpec((LANES, LANES), lambda bi, i: (0, 0))
    lower = jnp.tril(jnp.ones((chunk, chunk), BF16))
    triangles = jnp.stack([lower, lower.T])
    gate_bias_t = jnp.swapaxes(gate_bias, 1, 2)
    kernel = functools.partial(_mlstm_kernel, chunk=chunk, rows_from_input=rows_from_input)
    in_specs = [col(COL_MQ), col(COL_MK), col(COL_MV), col(COL_MO), col(COL_MZ),
                pl.BlockSpec((1, chunk, LANES), lambda bi, i: (bi, i, 0)),
                gates_t_spec,
                _layer_spec((1, LANES), layer),
                _layer_spec((LANES, 1), layer),
                _layer_spec((1, w), layer),
                pl.BlockSpec((2, chunk, chunk), lambda bi, i: (0, 0, 0)),
                state_in((h, d, d)), state_in((h, d)), state_in((h, LANES)),
                pl.BlockSpec(memory_space=pl.ANY)]
    args = [proj4, proj4, proj4, proj4, proj4, gates3, gates_t, gate_bias, gate_bias_t, head_norm, triangles,
            c0, n0, m0, c_stack]
    return pl.pallas_call(
        kernel,
        grid=(b, nc),
        in_specs=in_specs,
        out_specs=[pl.BlockSpec((1, chunk, w), lambda bi, i: (bi, i, 0)),
                   pl.BlockSpec((None, 1, h, d, d), lambda bi, i: (layer, bi, 0, 0, 0)),
                   state_out((h, d)), state_out((h, LANES))],
        out_shape=[jax.ShapeDtypeStruct((b, t, w), BF16),
                   jax.ShapeDtypeStruct(c_stack.shape, F32),
                   jax.ShapeDtypeStruct((b, h, d), F32),
                   jax.ShapeDtypeStruct((b, h, LANES), F32)],
        input_output_aliases={len(args) - 1: 1},
        compiler_params=_params(("parallel", "arbitrary"), 48),
        name="mlstm",
    )(*args)


def _pool_prev_rows(hist_ref, halo_ref, first):
    hist = hist_ref[0].astype(F32)
    hist = jnp.concatenate([jnp.zeros((POOL_HALO - POOL_BUF, hist.shape[1]), F32), hist], axis=0)
    return jnp.where(first, hist, halo_ref[...].astype(F32))


def _pool_mix_tile(u, prev, z_ref, w_ref, sc_ref, pos):
    outs = []
    for gi, win in enumerate(POOL_WINDOWS):
        sl = slice(gi * POOL_GROUP_DIM, (gi + 1) * POOL_GROUP_DIM)
        ext = jnp.concatenate([prev[:, sl], u[:, sl]], axis=0)
        acc = ext
        span = 1
        while span < win:
            acc = acc + pltpu.roll(acc, span, 0)
            span *= 2
        cnt = jnp.minimum(pos + 1, win).astype(F32)
        mean = acc[POOL_HALO:, :] / cnt
        m = (mean - u[:, sl]).astype(BF16)
        y = jnp.dot(m, w_ref[gi], preferred_element_type=F32) * sc_ref[:, sl]
        outs.append((y * _silu(z_ref[:, sl].astype(F32))).astype(BF16))
    return jnp.concatenate(outs, axis=1)


def _pool_kernel(u_ref, halo_ref, hist_ref, z_ref, w_ref, sc_ref, y_ref, *, tp, pos0):
    i = pl.program_id(1)
    prev = _pool_prev_rows(hist_ref, halo_ref.at[0], i == 0)
    pos = pos0 + i * tp + lax.broadcasted_iota(jnp.int32, (tp, 1), 0)
    y_ref[0] = _pool_mix_tile(u_ref[0].astype(F32), prev, z_ref.at[0], w_ref, sc_ref, pos)


def _pool(proj4, hist, w_group, scale, layer, hist_layer, *, tp, pos0):
    _, b, t, w = proj4.shape
    halo_blocks = tp // POOL_HALO
    kernel = functools.partial(_pool_kernel, tp=tp, pos0=pos0)
    return pl.pallas_call(
        kernel,
        grid=(b, t // tp),
        in_specs=[pl.BlockSpec((None, 1, tp, w), lambda bi, i: (COL_PU, bi, i, 0)),
                  pl.BlockSpec((None, 1, POOL_HALO, w),
                               lambda bi, i: (COL_PU, bi, jnp.maximum(i * halo_blocks - 1, 0), 0)),
                  pl.BlockSpec((None, 1, POOL_BUF, w), lambda bi, i: (hist_layer, bi, 0, 0)),
                  pl.BlockSpec((None, 1, tp, w), lambda bi, i: (COL_PZ, bi, i, 0)),
                  _layer_spec((len(POOL_WINDOWS), POOL_GROUP_DIM, POOL_GROUP_DIM), layer),
                  _layer_spec((1, w), layer)],
        out_specs=pl.BlockSpec((1, tp, w), lambda bi, i: (bi, i, 0)),
        out_shape=jax.ShapeDtypeStruct((b, t, w), BF16),
        compiler_params=_params(("parallel", "arbitrary"), 32),
        name="pool",
    )(proj4, proj4, hist, proj4, w_group, scale)


def _merge_kernel(*refs):
    y_refs = refs[:N_BRANCH]
    gate_refs = refs[N_BRANCH:-4]
    x_ref, wb_ref, wo_ref, o_ref = refs[-4:]
    per_branch = len(gate_refs) // N_BRANCH
    merged = None
    for b in range(N_BRANCH):
        gate = jnp.concatenate([_sigmoid(r[...].astype(F32)) for r in gate_refs[b * per_branch:(b + 1) * per_branch]],
                               axis=1)
        term = gate * jnp.dot(y_refs[b][...], wb_ref[b], preferred_element_type=F32)
        merged = term if merged is None else merged + term
    o_ref[...] = x_ref[...] + jnp.dot(merged.astype(BF16), wo_ref[...], preferred_element_type=F32)


def _merge(y_pool, y_ml, y_att, proj, x2, w_branch, w_out, layer, tm):
    n, d = x2.shape
    w = BRANCH_WIDTH
    row = lambda width: pl.BlockSpec((tm, width), lambda i: (i, 0))
    n_gate_blocks = N_BRANCH * (d // w)
    gate_specs = [pl.BlockSpec((None, tm, w), lambda i, k=k: (COL_GATES + k, i, 0)) for k in range(n_gate_blocks)]
    return pl.pallas_call(
        _merge_kernel,
        grid=(n // tm,),
        in_specs=[row(w)] * N_BRANCH + gate_specs + [row(d), _layer_spec((N_BRANCH, w, d), layer, True),
                                                    _layer_spec((d, d), layer, True)],
        out_specs=row(d),
        out_shape=jax.ShapeDtypeStruct((n, d), F32),
        compiler_params=_params(("parallel",), 56),
        name="merge",
    )(y_pool, y_ml, y_att, *([proj] * n_gate_blocks), x2, w_branch, w_out)


def _ple_kernel(x_ref, p_ref, g_ref, wg_ref, wp_ref, o_ref):
    x = x_ref[...]
    h = (x * lax.rsqrt(jnp.mean(x * x, axis=-1, keepdims=True) + EPS) * g_ref[...]).astype(BF16)
    pg = _sigmoid(jnp.dot(h, wg_ref[...], preferred_element_type=F32))
    pp = jnp.dot(p_ref[...].astype(BF16), wp_ref[...], preferred_element_type=F32)
    o_ref[...] = x + pg * pp


def _ple(x2, p_all, gain, w_gate, w_proj, layer, tm):
    n, d = x2.shape
    dp = p_all.shape[2]
    return pl.pallas_call(
        _ple_kernel,
        grid=(n // tm,),
        in_specs=[pl.BlockSpec((tm, d), lambda i: (i, 0)),
                  pl.BlockSpec((None, tm, dp), lambda i: (layer, i, 0)),
                  _layer_spec((1, d), layer, True), _layer_spec((d, d), layer, True),
                  _layer_spec((dp, d), layer, True)],
        out_specs=pl.BlockSpec((tm, d), lambda i: (i, 0)),
        out_shape=jax.ShapeDtypeStruct((n, d), F32),
        compiler_params=_params(("parallel",), 48),
        name="ple",
    )(x2, p_all, gain, w_gate, w_proj)


def _tile(n, preferred):
    t = min(n, preferred)
    assert n % t == 0, (n, t)
    return t


def _prepare_weights(norm_mix, w_in, w_pool_group, pool_scale, b_ig, b_fg, ml_head_norm, att_q_norm, att_k_norm,
                     att_rel_bias, w_branch, w_out, ple_norm, w_ple_gate, w_ple_proj):
    depth, d, width = w_in.shape
    assert width == W_IN_GATES_END + (COL_GATES - COL_AQ) * BRANCH_WIDTH + N_BRANCH * d
    pad = LANES - W_IN_GATE_COLS
    w_in_t = jnp.swapaxes(w_in, 1, 2)
    w_gate = jnp.pad(w_in_t[:, W_IN_GATES_START:W_IN_GATES_END, :], ((0, 0), (0, pad), (0, 0))).astype(BF16)
    gate_bias = jnp.pad(jnp.concatenate([b_ig, b_fg], axis=1), ((0, 0), (0, pad)))[:, None, :].astype(F32)
    gq = att_q_norm * (ATT_HEAD_DIM ** -0.5 * LOG2_E)
    return dict(
        norm_g=norm_mix[:, None, :], w_main=w_in_t, w_gate=w_gate, gate_bias=gate_bias,
        w_pool=w_pool_group.astype(BF16), pool_scale=pool_scale[:, None, :], head_norm=ml_head_norm[:, None, :],
        gq=gq[:, None, :], gk=att_k_norm[:, None, :], rel_bias=att_rel_bias,
        w_branch=w_branch.astype(BF16), w_out=w_out.astype(BF16), ple_norm=ple_norm[:, None, :],
        w_ple_gate=w_ple_gate.astype(BF16), w_ple_proj=w_ple_proj.astype(BF16))


def _layer(x3, p_all, hist, c0, n0, m0, kv_cache, state_layer, pos0, layer, lw, c_stack):
    b, t, d = x3.shape
    n = b * t
    w = BRANCH_WIDTH
    prompt = kv_cache is None
    assert t % CHUNK == 0 and t >= POOL_BUF

    proj, gates, gates_t = _inproj(x3.reshape(n, d), lw["norm_g"], lw["w_main"], lw["w_gate"], lw["gq"], lw["gk"],
                                   layer, _tile(n, 1024))
    proj4 = proj.reshape(-1, b, t, w)

    if prompt:
        tq, group = _tile(t, ATT_WINDOW), 2 * CHUNK
        kprev, vprev, cache_layer = None, None, None
    else:
        tq, group = t, CHUNK
        assert t == CHUNK
        kprev, vprev = kv_cache
        cache_layer = state_layer
    bias = _attn_bias(lw["rel_bias"][layer], group)
    y_att = _attention(proj4, kprev, vprev, bias, tq=tq, group=group, cache_layer=cache_layer)

    m0b = jnp.broadcast_to(m0[..., None], m0.shape + (LANES,))
    y_ml, c1, n1, m1 = _mlstm(proj4, gates.reshape(b, t, LANES), gates_t, lw["gate_bias"], lw["head_norm"],
                              c0, n0, m0b, layer, state_layer, _tile(t, 256), c_stack)

    y_pool = _pool(proj4, hist, lw["w_pool"], lw["pool_scale"], layer, state_layer, tp=_tile(t, 512), pos0=pos0)

    x1 = _merge(y_pool.reshape(n, w), y_ml.reshape(n, w), y_att.reshape(n, w), proj, x3.reshape(n, d),
                lw["w_branch"], lw["w_out"], layer, _tile(n, 256))
    x2 = _ple(x1, p_all.reshape(p_all.shape[0], n, -1), lw["ple_norm"], lw["w_ple_gate"], lw["w_ple_proj"],
              layer, _tile(n, 512))

    keep = min(ATT_WINDOW, t) if prompt else t
    new_pool = proj4[COL_PU, :, t - POOL_BUF:].astype(F32)
    new_k = proj4[COL_AK, :, t - keep:].astype(F32).reshape(b, keep, ATT_HEADS, ATT_HEAD_DIM)
    new_v = proj4[COL_AV, :, t - keep:].astype(F32).reshape(b, keep, ATT_HEADS, ATT_HEAD_DIM)
    return x2.reshape(b, t, d), (new_pool, c1, n1, m1[:, :, 0], new_k, new_v)


def kernel(x_prompt, x_sample, cache_att_k, cache_att_v, state_pool, state_mlstm_c, state_mlstm_n, state_mlstm_m, p_prompt, p_sample, norm_mix, w_in, w_pool_group, pool_scale, b_ig, b_fg, ml_head_norm, att_q_norm, att_k_norm, att_rel_bias, w_branch, w_out, ple_norm, w_ple_gate, w_ple_proj):
    xp, xs = x_prompt, x_sample
    bp = x_prompt.shape[0]
    depth = w_in.shape[0]
    lw = _prepare_weights(norm_mix, w_in, w_pool_group, pool_scale, b_ig, b_fg, ml_head_norm, att_q_norm,
                          att_k_norm, att_rel_bias, w_branch, w_out, ple_norm, w_ple_gate, w_ple_proj)
    hist0 = jnp.zeros((1, bp, POOL_BUF, BRANCH_WIDTH), F32)
    c0 = jnp.zeros((1, bp, ML_HEADS, ML_HEAD_DIM, ML_HEAD_DIM), F32)
    n0 = jnp.zeros((1, bp, ML_HEADS, ML_HEAD_DIM), F32)
    m0 = jnp.zeros((1, bp, ML_HEADS), F32)
    sp = [[] for _ in range(6)]
    ss = [[] for _ in range(6)]
    c_p = jnp.zeros((depth,) + c0.shape[1:], F32)
    c_s = jnp.zeros(state_mlstm_c.shape, F32)
    for i in range(depth):
        xp, st_p = _layer(xp, p_prompt, hist0, c0, n0, m0, None, 0, 0, i, lw, c_p)
        xs, st_s = _layer(xs, p_sample, state_pool, state_mlstm_c, state_mlstm_n, state_mlstm_m,
                          (cache_att_k, cache_att_v), i, PAST_LEN, i, lw, c_s)
        c_p, c_s = st_p[1], st_s[1]
        for j in range(6):
            sp[j].append(st_p[j])
            ss[j].append(st_s[j])
    pool_p, _, n_p, m_p, k_p, v_p = [jnp.stack(a) if j != 1 else None for j, a in enumerate(sp)]
    pool_s, _, n_s, m_s, k_s, v_s = [jnp.stack(a) if j != 1 else None for j, a in enumerate(ss)]
    return (xp, xs, pool_p, pool_s, c_p, c_s, n_p, n_s, m_p, m_s, k_p, k_s, v_p, v_s)
```

```python
import functools

import numpy as np
import jax
import jax.numpy as jnp
from jax import lax
from jax.experimental import pallas as pl
from jax.experimental.pallas import tpu as pltpu

F32 = jnp.float32
BF16 = jnp.bfloat16

EPS = 1e-6
CHUNK = 64
PAST_LEN = 1024

POOL_WINDOWS = (2, 4, 8, 16)
POOL_GROUP_DIM = 256
POOL_BUF = 15
POOL_HALO = 16

ML_HEADS = 4
ML_HEAD_DIM = 256

ATT_HEADS = 8
ATT_HEAD_DIM = 128
ATT_WINDOW = 512
REL_CLIP = 256

BRANCH_WIDTH = 1024
LANES = 128

N_BRANCH = 3
COL_PU, COL_PZ, COL_MQ, COL_MK, COL_MV, COL_MO, COL_MZ, COL_AQ, COL_AK, COL_AV, COL_AZ = range(11)
COL_GATES = 11
W_IN_GATES_START = 7168
W_IN_GATES_END = 7176
W_IN_GATE_COLS = W_IN_GATES_END - W_IN_GATES_START

PROJ_DTYPE = jnp.bfloat16
NEG_BIG = -1e30
LOG2_E = 1.4426950408889634
LN_2 = 0.6931471805599453
MIB = 1024 * 1024


def _params(semantics, vmem_mib):
    return pltpu.CompilerParams(dimension_semantics=semantics, vmem_limit_bytes=vmem_mib * MIB)


def _layer_spec(tail, layer, single_buffer=False):
    index_map = lambda *_: (layer,) + (0,) * len(tail)
    if single_buffer:
        return pl.BlockSpec((None,) + tuple(tail), index_map, pipeline_mode=pl.Buffered(1))
    return pl.BlockSpec((None,) + tuple(tail), index_map)


def _sigmoid(x):
    return 1.0 / (1.0 + jnp.exp(-x))


def _silu(x):
    return x * _sigmoid(x)


def _nt_dot(a, b):
    return lax.dot_general(a, b, (((1,), (1,)), ((), ())), preferred_element_type=F32)


def _tn_dot(a, b):
    return lax.dot_general(a, b, (((0,), (0,)), ((), ())), preferred_element_type=F32)


def _inproj_kernel(x_ref, g_ref, wa_ref, wb_ref, wg_ref, gq_ref, gk_ref, o_ref, og_ref, ogt_ref, h_ref,
                   *, blocks_before):
    j = pl.program_id(1)

    @pl.when(j == 0)
    def _():
        x = x_ref[...]
        ms = jnp.mean(x * x, axis=-1, keepdims=True)
        h = (x * lax.rsqrt(ms + EPS) * g_ref[...]).astype(BF16)
        h_ref[...] = h
        og_ref[...] = _nt_dot(h, wg_ref[...])
        ogt_ref[...] = _nt_dot(wg_ref[...], h)

    def weights(shifted):
        if shifted:
            return jnp.concatenate([wa_ref[W_IN_GATE_COLS:, :], wb_ref[...]], axis=0).astype(BF16)
        return wa_ref[...].astype(BF16)

    is_q = j == COL_AQ
    head_normed = jnp.logical_or(is_q, j == COL_AK)
    after = j >= blocks_before

    @pl.when(jnp.logical_not(after))
    def _():
        o_ref[...] = _nt_dot(h_ref[...], weights(False)).astype(o_ref.dtype)

    @pl.when(jnp.logical_and(after, jnp.logical_not(head_normed)))
    def _():
        o_ref[...] = _nt_dot(h_ref[...], weights(True)).astype(o_ref.dtype)

    @pl.when(head_normed)
    def _():
        acc = _nt_dot(h_ref[...], weights(True))
        gain = jnp.where(is_q, gq_ref[...], gk_ref[...])
        for h in range(ATT_HEADS):
            sl = slice(h * ATT_HEAD_DIM, (h + 1) * ATT_HEAD_DIM)
            a = acc[:, sl]
            r = lax.rsqrt(jnp.mean(a * a, axis=-1, keepdims=True) + EPS)
            o_ref[:, sl] = (a * r * gain).astype(o_ref.dtype)


def _inproj(x2, gain, w_main_t, w_gate_t, gq, gk, layer, tm):
    n, d = x2.shape
    nw = w_main_t.shape[1] - W_IN_GATE_COLS
    tn = BRANCH_WIDTH
    assert tn == ATT_HEADS * ATT_HEAD_DIM and nw % tn == 0 and W_IN_GATES_START % tn == 0
    blocks_before = W_IN_GATES_START // tn
    assert min(COL_AQ, COL_AK) >= blocks_before
    kernel = functools.partial(_inproj_kernel, blocks_before=blocks_before)
    return pl.pallas_call(
        kernel,
        grid=(n // tm, nw // tn),
        in_specs=[pl.BlockSpec((tm, d), lambda i, j: (i, 0)),
                  _layer_spec((1, d), layer),
                  pl.BlockSpec((None, tn, d), lambda i, j: (layer, j, 0)),
                  pl.BlockSpec((None, W_IN_GATE_COLS, d),
                               lambda i, j: (layer, (j + 1) * (tn // W_IN_GATE_COLS), 0)),
                  _layer_spec((LANES, d), layer),
                  _layer_spec((1, ATT_HEAD_DIM), layer),
                  _layer_spec((1, ATT_HEAD_DIM), layer)],
        out_specs=[pl.BlockSpec((None, tm, tn), lambda i, j: (j, i, 0)),
                   pl.BlockSpec((tm, LANES), lambda i, j: (i, 0)),
                   pl.BlockSpec((LANES, tm), lambda i, j: (0, i))],
        out_shape=[jax.ShapeDtypeStruct((nw // tn, n, tn), PROJ_DTYPE),
                   jax.ShapeDtypeStruct((n, LANES), F32),
                   jax.ShapeDtypeStruct((LANES, n), F32)],
        scratch_shapes=[pltpu.VMEM((tm, d), BF16)],
        compiler_params=_params(("parallel", "arbitrary"), 48),
        name="inproj",
    )(x2, gain, w_main_t, w_main_t, w_gate_t, gq, gk)


def _att_keys(group):
    return -(-(ATT_WINDOW + group) // LANES) * LANES


def _attn_bias(rel_bias, group):
    keys = _att_keys(group)
    period = keys + group
    m = np.arange(period)
    m = np.where(m <= keys, m, m - period)
    idx = np.clip(ATT_WINDOW - m, -REL_CLIP, REL_CLIP) + REL_CLIP
    vec = rel_bias[:, idx].astype(F32)
    heads = rel_bias.shape[0]
    bias = jnp.tile(vec, (1, group))[:, :group * (period - 1)].reshape(heads, group, period - 1)[:, :, :keys]
    i = np.arange(group)[:, None]
    j = np.arange(keys)[None, :]
    lo = (i // CHUNK) * CHUNK
    band = (j >= lo) & (j < lo + ATT_WINDOW + CHUNK)
    return jnp.where(jnp.asarray(band)[None], bias * LOG2_E, NEG_BIG)


def _attn_kernel(q_ref, kp_ref, kc_ref, vp_ref, vc_ref, az_ref, bias_ref, o_ref, kw_ref, vw_ref, s_ref, m_ref,
                 *, tq, group, first_prev_invalid, prev_by_head):
    rows = kw_ref.shape[0]
    keys = bias_ref.shape[2]
    if prev_by_head:
        for h in range(ATT_HEADS):
            sl = slice(h * ATT_HEAD_DIM, (h + 1) * ATT_HEAD_DIM)
            kw_ref[0:ATT_WINDOW, sl] = kp_ref[pl.ds(h, ATT_WINDOW, stride=ATT_HEADS), :].astype(BF16)
            vw_ref[0:ATT_WINDOW, sl] = vp_ref[pl.ds(h, ATT_WINDOW, stride=ATT_HEADS), :].astype(BF16)
    else:
        kw_ref[0:ATT_WINDOW, :] = kp_ref[0].astype(BF16)
        vw_ref[0:ATT_WINDOW, :] = vp_ref[0].astype(BF16)
    kw_ref[ATT_WINDOW:ATT_WINDOW + tq, :] = kc_ref[0].astype(BF16)
    vw_ref[ATT_WINDOW:ATT_WINDOW + tq, :] = vc_ref[0].astype(BF16)
    if rows > ATT_WINDOW + tq:
        pad = jnp.zeros((rows - ATT_WINDOW - tq, kw_ref.shape[1]), BF16)
        kw_ref[ATT_WINDOW + tq:rows, :] = pad
        vw_ref[ATT_WINDOW + tq:rows, :] = pad

    def attend(mask_prev):
        col = lax.broadcasted_iota(jnp.int32, (group, keys), 1)
        ones = jnp.ones((keys, ATT_HEAD_DIM), BF16)
        for g in range(tq // group):
            r0 = g * group
            for h in range(ATT_HEADS):
                sl = slice(h * ATT_HEAD_DIM, (h + 1) * ATT_HEAD_DIM)
                q = q_ref[0, r0:r0 + group, sl].astype(BF16)
                k = kw_ref[r0:r0 + keys, sl]
                s = _nt_dot(q, k) + bias_ref[h]
                if mask_prev:
                    s = jnp.where(col + r0 < ATT_WINDOW, NEG_BIG, s)
                s_ref[h] = s
                m_ref[h] = jnp.max(s, axis=-1, keepdims=True)
            for h in range(ATT_HEADS):
                sl = slice(h * ATT_HEAD_DIM, (h + 1) * ATT_HEAD_DIM)
                v1 = jnp.concatenate([vw_ref[r0:r0 + keys, sl], ones], axis=1)
                p = jnp.exp2(s_ref[h] - m_ref[h])
                o = jnp.dot(p.astype(BF16), v1, preferred_element_type=F32)
                o = o[:, :ATT_HEAD_DIM] / o[:, ATT_HEAD_DIM:]
                z = az_ref[0, r0:r0 + group, sl].astype(F32)
                o_ref[0, r0:r0 + group, sl] = (o * _silu(z)).astype(o_ref.dtype)

    if first_prev_invalid:
        first = pl.program_id(1) == 0
        pl.when(first)(functools.partial(attend, True))
        pl.when(jnp.logical_not(first))(functools.partial(attend, False))
    else:
        attend(False)


def _attention(proj4, kprev, vprev, bias, *, tq, group, cache_layer):
    _, b, t, w = proj4.shape
    nt = t // tq
    prompt = cache_layer is None
    if prompt:
        assert tq == ATT_WINDOW
        kprev = vprev = proj4
        kprev_spec = pl.BlockSpec((None, 1, ATT_WINDOW, w), lambda bi, i: (COL_AK, bi, jnp.maximum(i - 1, 0), 0))
        vprev_spec = pl.BlockSpec((None, 1, ATT_WINDOW, w), lambda bi, i: (COL_AV, bi, jnp.maximum(i - 1, 0), 0))
    else:
        assert nt == 1 and kprev.shape[2:] == (ATT_WINDOW, ATT_HEADS, ATT_HEAD_DIM)
        depth = kprev.shape[0]
        kprev = kprev.reshape(depth, b, ATT_WINDOW * ATT_HEADS, ATT_HEAD_DIM)
        vprev = vprev.reshape(depth, b, ATT_WINDOW * ATT_HEADS, ATT_HEAD_DIM)
        kprev_spec = pl.BlockSpec((None, None, ATT_WINDOW * ATT_HEADS, ATT_HEAD_DIM),
                                  lambda bi, i: (cache_layer, bi, 0, 0))
        vprev_spec = kprev_spec
    keys = bias.shape[2]
    rows = tq + keys - group
    cur = lambda c: pl.BlockSpec((None, 1, tq, w), lambda bi, i: (c, bi, i, 0))
    kernel = functools.partial(_attn_kernel, tq=tq, group=group, first_prev_invalid=prompt, prev_by_head=not prompt)
    return pl.pallas_call(
        kernel,
        grid=(b, nt),
        in_specs=[cur(COL_AQ), kprev_spec, cur(COL_AK), vprev_spec, cur(COL_AV), cur(COL_AZ),
                  pl.BlockSpec((ATT_HEADS, group, keys), lambda bi, i: (0, 0, 0))],
        out_specs=pl.BlockSpec((1, tq, w), lambda bi, i: (bi, i, 0)),
        out_shape=jax.ShapeDtypeStruct((b, t, w), BF16),
        scratch_shapes=[pltpu.VMEM((rows, w), BF16), pltpu.VMEM((rows, w), BF16),
                        pltpu.VMEM((ATT_HEADS, group, keys), F32), pltpu.VMEM((ATT_HEADS, group, 1), F32)],
        compiler_params=_params(("parallel", "arbitrary"), 48),
        name="attention",
    )(proj4, kprev, proj4, vprev, proj4, proj4, bias)


def _log_sigmoid(x):
    return jnp.minimum(x, 0.0) - jnp.log1p(jnp.exp(-jnp.abs(x)))


def _split3(x):
    hi = x.astype(BF16)
    rest = x - hi.astype(F32)
    mid = rest.astype(BF16)
    lo = (rest - mid.astype(F32)).astype(BF16)
    return hi, mid, lo


def _mlstm_kernel(q_ref, k_ref, v_ref, o_ref, z_ref, g_ref, gt_ref, gb_ref, gbt_ref, hn_ref, tri_ref,
                  c0_ref, n0_ref, m0_ref, *rest, chunk, rows_from_input, own_layer):
    y_ref, c_all_ref, n_ref, m_ref = rest[-4:]
    c_ref = c_all_ref if own_layer is None else c_all_ref.at[own_layer]
    L = chunk
    D = ML_HEAD_DIM

    @pl.when(pl.program_id(1) == 0)
    def _():
        if own_layer is not None:
            for other in range(c_all_ref.shape[0]):
                if other != own_layer:
                    c_all_ref[other] = jnp.zeros(c_all_ref.shape[1:], F32)
        c_ref[...] = c0_ref[...]
        n_ref[...] = n0_ref[...]
        m_ref[...] = m0_ref[...]

    row = lax.broadcasted_iota(jnp.int32, (L, L), 0)
    col = lax.broadcasted_iota(jnp.int32, (L, L), 1)
    causal = col <= row

    n_gates = 2 * ML_HEADS
    gates = (g_ref[0] + gb_ref[...]) * LOG2_E
    if rows_from_input:
        gates_t = gt_ref[0:n_gates, :] + gbt_ref[0:n_gates, :]
        fill = (jnp.zeros((n_gates, L), BF16),)
        a_rows = jnp.dot(jnp.concatenate(_split3(_log_sigmoid(gates_t)) + fill, axis=0), tri_ref[1],
                         preferred_element_type=F32)
        a_rows = (a_rows[:n_gates] + a_rows[n_gates:2 * n_gates] + a_rows[2 * n_gates:3 * n_gates]) * LOG2_E
        gates_t = gates_t * LOG2_E
        pick = (lax.broadcasted_iota(jnp.int32, (4 * n_gates, LANES), 0) % n_gates
                == lax.broadcasted_iota(jnp.int32, (4 * n_gates, LANES), 1)).astype(BF16)
        a_cols = _tn_dot(jnp.concatenate(_split3(a_rows) + fill, axis=0), pick)
    else:
        logf = _log_sigmoid(g_ref[0] + gb_ref[...])
        a_cols = jnp.dot(tri_ref[0], jnp.concatenate(_split3(logf), axis=1), preferred_element_type=F32)
        a_cols = (a_cols[:, :LANES] + a_cols[:, LANES:2 * LANES] + a_cols[:, 2 * LANES:]) * LOG2_E
        eye = row == col

    ones = jnp.ones((L, LANES), BF16)
    k_scale = jnp.asarray(D ** -0.5, BF16)

    for hd in range(ML_HEADS):
        sl = slice(hd * D, (hd + 1) * D)
        ig_col = gates[:, hd:hd + 1]
        a_col = a_cols[:, ML_HEADS + hd:ML_HEADS + hd + 1]
        if rows_from_input:
            ig_row = gates_t[hd:hd + 1, :]
            a_row = a_rows[ML_HEADS + hd:ML_HEADS + hd + 1, :]
        else:
            ig_row = jnp.sum(jnp.where(eye, ig_col, 0.0), axis=0, keepdims=True)
            a_row = jnp.sum(jnp.where(eye, a_col, 0.0), axis=0, keepdims=True)
        b = a_col[L - 1:L, :]
        m_prev = m_ref[0, hd:hd + 1, 0:1] * LOG2_E
        c_prev = c_ref[0, hd]
        n_prev = n_ref[0, hd:hd + 1, :]

        logd = jnp.where(causal, a_col - a_row + ig_row, NEG_BIG)
        inter = a_col + m_prev
        m_row = jnp.maximum(inter, jnp.max(logd, axis=1, keepdims=True))
        dmat = jnp.exp2(logd - m_row)
        w_inter = jnp.exp2(inter - m_row)

        qb = q_ref[0, :, sl].astype(BF16)
        kb = k_ref[0, :, sl].astype(BF16) * k_scale
        v1 = jnp.concatenate([v_ref[0, :, sl].astype(BF16), ones], axis=1)
        c1 = jnp.concatenate([c_prev.astype(BF16), jnp.broadcast_to(n_prev.astype(BF16), (LANES, D))], axis=0)

        s = _nt_dot(qb, kb) * dmat
        sv = jnp.dot(s.astype(BF16), v1, preferred_element_type=F32)
        qc = _nt_dot(qb, c1)
        num = sv[:, :D] + w_inter * qc[:, :D]
        den = sv[:, D:] + w_inter * qc[:, D:]
        r = 1.0 / jnp.maximum(jnp.abs(den), jnp.exp2(-m_row))
        h = num * jnp.concatenate([r] * (D // LANES), axis=1)

        g_col = b - a_col + ig_col
        m_new = jnp.maximum(b + m_prev, jnp.max(g_col, axis=0, keepdims=True))
        wk = jnp.exp2(g_col - m_new)
        decay = jnp.exp2(b + m_prev - m_new)
        kw = kb * wk.astype(BF16)
        upd = _tn_dot(v1, kw)
        c_ref[0, hd] = decay * c_prev + upd[:D]
        n_ref[0, hd:hd + 1, :] = decay * n_prev + upd[D:D + 1]
        m_ref[0, hd:hd + 1, :] = jnp.broadcast_to(m_new * LN_2, (1, LANES))

        hm = h * _sigmoid(o_ref[0, :, sl].astype(F32))
        hm = hm * lax.rsqrt(jnp.mean(hm * hm, axis=-1, keepdims=True) + EPS) * hn_ref[:, sl]
        y_ref[0, :, sl] = (hm * _silu(z_ref[0, :, sl].astype(F32))).astype(y_ref.dtype)


def _mlstm(proj4, gates3, gates_t, gate_bias, head_norm, c0, n0, m0, layer, state_layer, chunk, c_stack):
    depth = gate_bias.shape[0]
    _, b, t, w = proj4.shape
    h, d = ML_HEADS, ML_HEAD_DIM
    nc = t // chunk
    rows_from_input = chunk % LANES == 0
    col = lambda c: pl.BlockSpec((None, 1, chunk, w), lambda bi, i: (c, bi, i, 0))
    state_in = lambda shape: pl.BlockSpec((None, 1) + shape, lambda bi, i: (state_layer, bi) + (0,) * len(shape))
    state_out = lambda shape: pl.BlockSpec((1,) + shape, lambda bi, i: (bi,) + (0,) * len(shape))
    if rows_from_input:
        gates_t_spec = pl.BlockSpec((LANES, chunk), lambda bi, i: (0, bi * nc + i))
    else:
        gates_t_spec = pl.BlockSpec((LANES, LANES), lambda bi, i: (0, 0))
    lower = jnp.tril(jnp.ones((chunk, chunk), BF16))
    triangles = jnp.stack([lower, lower.T])
    gate_bias_t = jnp.swapaxes(gate_bias, 1, 2)
    creates_stack = c_stack is None
    kernel = functools.partial(_mlstm_kernel, chunk=chunk, rows_from_input=rows_from_input,
                               own_layer=layer if creates_stack else None)
    in_specs = [col(COL_MQ), col(COL_MK), col(COL_MV), col(COL_MO), col(COL_MZ),
                pl.BlockSpec((1, chunk, LANES), lambda bi, i: (bi, i, 0)),
                gates_t_spec,
                _layer_spec((1, LANES), layer),
                _layer_spec((LANES, 1), layer),
                _layer_spec((1, w), layer),
                pl.BlockSpec((2, chunk, chunk), lambda bi, i: (0, 0, 0)),
                state_in((h, d, d)), state_in((h, d)), state_in((h, LANES))]
    args = [proj4, proj4, proj4, proj4, proj4, gates3, gates_t, gate_bias, gate_bias_t, head_norm, triangles,
            c0, n0, m0]
    if creates_stack:
        aliases = {}
        c_spec = pl.BlockSpec((depth, 1, h, d, d), lambda bi, i: (0, bi, 0, 0, 0))
    else:
        aliases = {len(args): 1}
        in_specs.append(pl.BlockSpec(memory_space=pl.ANY))
        args.append(c_stack)
        c_spec = pl.BlockSpec((None, 1, h, d, d), lambda bi, i: (layer, bi, 0, 0, 0))
    return pl.pallas_call(
        kernel,
        grid=(b, nc),
        in_specs=in_specs,
        out_specs=[pl.BlockSpec((1, chunk, w), lambda bi, i: (bi, i, 0)),
                   c_spec, state_out((h, d)), state_out((h, LANES))],
        out_shape=[jax.ShapeDtypeStruct((b, t, w), BF16),
                   jax.ShapeDtypeStruct((depth, b, h, d, d), F32),
                   jax.ShapeDtypeStruct((b, h, d), F32),
                   jax.ShapeDtypeStruct((b, h, LANES), F32)],
        input_output_aliases=aliases,
        compiler_params=_params(("parallel", "arbitrary"), 48),
        name="mlstm",
    )(*args)


def _pool_prev_rows(hist_ref, halo_ref, first):
    hist = hist_ref[0].astype(F32)
    hist = jnp.concatenate([jnp.zeros((POOL_HALO - POOL_BUF, hist.shape[1]), F32), hist], axis=0)
    return jnp.where(first, hist, halo_ref[...].astype(F32))


def _pool_mix_tile(u, prev, z_ref, w_ref, sc_ref, pos):
    outs = []
    for gi, win in enumerate(POOL_WINDOWS):
        sl = slice(gi * POOL_GROUP_DIM, (gi + 1) * POOL_GROUP_DIM)
        ext = jnp.concatenate([prev[:, sl], u[:, sl]], axis=0)
        acc = ext
        span = 1
        while span < win:
            acc = acc + pltpu.roll(acc, span, 0)
            span *= 2
        cnt = jnp.minimum(pos + 1, win).astype(F32)
        mean = acc[POOL_HALO:, :] / cnt
        m = (mean - u[:, sl]).astype(BF16)
        y = jnp.dot(m, w_ref[gi], preferred_element_type=F32) * sc_ref[:, sl]
        outs.append((y * _silu(z_ref[:, sl].astype(F32))).astype(BF16))
    return jnp.concatenate(outs, axis=1)


def _pool_kernel(u_ref, halo_ref, hist_ref, z_ref, w_ref, sc_ref, y_ref, *, tp, pos0):
    i = pl.program_id(1)
    prev = _pool_prev_rows(hist_ref, halo_ref.at[0], i == 0)
    pos = pos0 + i * tp + lax.broadcasted_iota(jnp.int32, (tp, 1), 0)
    y_ref[0] = _pool_mix_tile(u_ref[0].astype(F32), prev, z_ref.at[0], w_ref, sc_ref, pos)


def _pool(proj4, hist, w_group, scale, layer, hist_layer, *, tp, pos0):
    _, b, t, w = proj4.shape
    halo_blocks = tp // POOL_HALO
    kernel = functools.partial(_pool_kernel, tp=tp, pos0=pos0)
    return pl.pallas_call(
        kernel,
        grid=(b, t // tp),
        in_specs=[pl.BlockSpec((None, 1, tp, w), lambda bi, i: (COL_PU, bi, i, 0)),
                  pl.BlockSpec((None, 1, POOL_HALO, w),
                               lambda bi, i: (COL_PU, bi, jnp.maximum(i * halo_blocks - 1, 0), 0)),
                  pl.BlockSpec((None, 1, POOL_BUF, w), lambda bi, i: (hist_layer, bi, 0, 0)),
                  pl.BlockSpec((None, 1, tp, w), lambda bi, i: (COL_PZ, bi, i, 0)),
                  _layer_spec((len(POOL_WINDOWS), POOL_GROUP_DIM, POOL_GROUP_DIM), layer),
                  _layer_spec((1, w), layer)],
        out_specs=pl.BlockSpec((1, tp, w), lambda bi, i: (bi, i, 0)),
        out_shape=jax.ShapeDtypeStruct((b, t, w), BF16),
        compiler_params=_params(("parallel", "arbitrary"), 32),
        name="pool",
    )(proj4, proj4, hist, proj4, w_group, scale)


def _merge_kernel(*refs):
    y_refs = refs[:N_BRANCH]
    gate_refs = refs[N_BRANCH:-4]
    x_ref, wb_ref, wo_ref, o_ref = refs[-4:]
    per_branch = len(gate_refs) // N_BRANCH
    merged = None
    for b in range(N_BRANCH):
        gate = jnp.concatenate([_sigmoid(r[...].astype(F32)) for r in gate_refs[b * per_branch:(b + 1) * per_branch]],
                               axis=1)
        term = gate * jnp.dot(y_refs[b][...], wb_ref[b], preferred_element_type=F32)
        merged = term if merged is None else merged + term
    o_ref[...] = x_ref[...] + jnp.dot(merged.astype(BF16), wo_ref[...], preferred_element_type=F32)


def _merge(y_pool, y_ml, y_att, proj, x2, w_branch, w_out, layer, tm):
    n, d = x2.shape
    w = BRANCH_WIDTH
    row = lambda width: pl.BlockSpec((tm, width), lambda i: (i, 0))
    n_gate_blocks = N_BRANCH * (d // w)
    gate_specs = [pl.BlockSpec((None, tm, w), lambda i, k=k: (COL_GATES + k, i, 0)) for k in range(n_gate_blocks)]
    return pl.pallas_call(
        _merge_kernel,
        grid=(n // tm,),
        in_specs=[row(w)] * N_BRANCH + gate_specs + [row(d), _layer_spec((N_BRANCH, w, d), layer, True),
                                                    _layer_spec((d, d), layer, True)],
        out_specs=row(d),
        out_shape=jax.ShapeDtypeStruct((n, d), F32),
        compiler_params=_params(("parallel",), 56),
        name="merge",
    )(y_pool, y_ml, y_att, *([proj] * n_gate_blocks), x2, w_branch, w_out)


def _ple_kernel(x_ref, p_ref, g_ref, wg_ref, wp_ref, o_ref):
    x = x_ref[...]
    h = (x * lax.rsqrt(jnp.mean(x * x, axis=-1, keepdims=True) + EPS) * g_ref[...]).astype(BF16)
    pg = _sigmoid(jnp.dot(h, wg_ref[...], preferred_element_type=F32))
    pp = jnp.dot(p_ref[...].astype(BF16), wp_ref[...], preferred_element_type=F32)
    o_ref[...] = x + pg * pp


def _ple(x2, p_all, gain, w_gate, w_proj, layer, tm):
    n, d = x2.shape
    dp = p_all.shape[2]
    return pl.pallas_call(
        _ple_kernel,
        grid=(n // tm,),
        in_specs=[pl.BlockSpec((tm, d), lambda i: (i, 0)),
                  pl.BlockSpec((None, tm, dp), lambda i: (layer, i, 0)),
                  _layer_spec((1, d), layer, True), _layer_spec((d, d), layer, True),
                  _layer_spec((dp, d), layer, True)],
        out_specs=pl.BlockSpec((tm, d), lambda i: (i, 0)),
        out_shape=jax.ShapeDtypeStruct((n, d), F32),
        compiler_params=_params(("parallel",), 48),
        name="ple",
    )(x2, p_all, gain, w_gate, w_proj)


def _tile(n, preferred):
    t = min(n, preferred)
    assert n % t == 0, (n, t)
    return t


def _prepare_weights(norm_mix, w_in, w_pool_group, pool_scale, b_ig, b_fg, ml_head_norm, att_q_norm, att_k_norm,
                     att_rel_bias, w_branch, w_out, ple_norm, w_ple_gate, w_ple_proj):
    depth, d, width = w_in.shape
    assert width == W_IN_GATES_END + (COL_GATES - COL_AQ) * BRANCH_WIDTH + N_BRANCH * d
    pad = LANES - W_IN_GATE_COLS
    w_in_t = jnp.swapaxes(w_in, 1, 2)
    w_gate = jnp.pad(w_in_t[:, W_IN_GATES_START:W_IN_GATES_END, :], ((0, 0), (0, pad), (0, 0))).astype(BF16)
    gate_bias = jnp.pad(jnp.concatenate([b_ig, b_fg], axis=1), ((0, 0), (0, pad)))[:, None, :].astype(F32)
    gq = att_q_norm * (ATT_HEAD_DIM ** -0.5 * LOG2_E)
    return dict(
        norm_g=norm_mix[:, None, :], w_main=w_in_t, w_gate=w_gate, gate_bias=gate_bias,
        w_pool=w_pool_group.astype(BF16), pool_scale=pool_scale[:, None, :], head_norm=ml_head_norm[:, None, :],
        gq=gq[:, None, :], gk=att_k_norm[:, None, :], rel_bias=att_rel_bias,
        w_branch=w_branch.astype(BF16), w_out=w_out.astype(BF16), ple_norm=ple_norm[:, None, :],
        w_ple_gate=w_ple_gate.astype(BF16), w_ple_proj=w_ple_proj.astype(BF16))


def _layer(x3, p_all, hist, c0, n0, m0, kv_cache, state_layer, pos0, layer, lw, c_stack):
    b, t, d = x3.shape
    n = b * t
    w = BRANCH_WIDTH
    prompt = kv_cache is None
    assert t % CHUNK == 0 and t >= POOL_BUF

    proj, gates, gates_t = _inproj(x3.reshape(n, d), lw["norm_g"], lw["w_main"], lw["w_gate"], lw["gq"], lw["gk"],
                                   layer, _tile(n, 1024))
    proj4 = proj.reshape(-1, b, t, w)

    if prompt:
        tq, group = _tile(t, ATT_WINDOW), 2 * CHUNK
        kprev, vprev, cache_layer = None, None, None
    else:
        tq, group = t, CHUNK
        assert t == CHUNK
        kprev, vprev = kv_cache
        cache_layer = state_layer
    bias = _attn_bias(lw["rel_bias"][layer], group)
    y_att = _attention(proj4, kprev, vprev, bias, tq=tq, group=group, cache_layer=cache_layer)

    m0b = jnp.broadcast_to(m0[..., None], m0.shape + (LANES,))
    y_ml, c1, n1, m1 = _mlstm(proj4, gates.reshape(b, t, LANES), gates_t, lw["gate_bias"], lw["head_norm"],
                              c0, n0, m0b, layer, state_layer, _tile(t, 256), c_stack)

    y_pool = _pool(proj4, hist, lw["w_pool"], lw["pool_scale"], layer, state_layer, tp=_tile(t, 512), pos0=pos0)

    x1 = _merge(y_pool.reshape(n, w), y_ml.reshape(n, w), y_att.reshape(n, w), proj, x3.reshape(n, d),
                lw["w_branch"], lw["w_out"], layer, _tile(n, 256))
    x2 = _ple(x1, p_all.reshape(p_all.shape[0], n, -1), lw["ple_norm"], lw["w_ple_gate"], lw["w_ple_proj"],
              layer, _tile(n, 512))

    keep = min(ATT_WINDOW, t) if prompt else t
    new_pool = proj4[COL_PU, :, t - POOL_BUF:].astype(F32)
    new_k = proj4[COL_AK, :, t - keep:].astype(F32).reshape(b, keep, ATT_HEADS, ATT_HEAD_DIM)
    new_v = proj4[COL_AV, :, t - keep:].astype(F32).reshape(b, keep, ATT_HEADS, ATT_HEAD_DIM)
    return x2.reshape(b, t, d), (new_pool, c1, n1, m1[:, :, 0], new_k, new_v)


def kernel(x_prompt, x_sample, cache_att_k, cache_att_v, state_pool, state_mlstm_c, state_mlstm_n, state_mlstm_m, p_prompt, p_sample, norm_mix, w_in, w_pool_group, pool_scale, b_ig, b_fg, ml_head_norm, att_q_norm, att_k_norm, att_rel_bias, w_branch, w_out, ple_norm, w_ple_gate, w_ple_proj):
    xp, xs = x_prompt, x_sample
    bp = x_prompt.shape[0]
    depth = w_in.shape[0]
    lw = _prepare_weights(norm_mix, w_in, w_pool_group, pool_scale, b_ig, b_fg, ml_head_norm, att_q_norm,
                          att_k_norm, att_rel_bias, w_branch, w_out, ple_norm, w_ple_gate, w_ple_proj)
    hist0 = jnp.zeros((1, bp, POOL_BUF, BRANCH_WIDTH), F32)
    c0 = jnp.zeros((1, bp, ML_HEADS, ML_HEAD_DIM, ML_HEAD_DIM), F32)
    n0 = jnp.zeros((1, bp, ML_HEADS, ML_HEAD_DIM), F32)
    m0 = jnp.zeros((1, bp, ML_HEADS), F32)
    sp = [[] for _ in range(6)]
    ss = [[] for _ in range(6)]
    c_p = c_s = None
    for i in range(depth):
        xp, st_p = _layer(xp, p_prompt, hist0, c0, n0, m0, None, 0, 0, i, lw, c_p)
        xs, st_s = _layer(xs, p_sample, state_pool, state_mlstm_c, state_mlstm_n, state_mlstm_m,
                          (cache_att_k, cache_att_v), i, PAST_LEN, i, lw, c_s)
        c_p, c_s = st_p[1], st_s[1]
        for j in range(6):
            sp[j].append(st_p[j])
            ss[j].append(st_s[j])
    pool_p, _, n_p, m_p, k_p, v_p = [jnp.stack(a) if j != 1 else None for j, a in enumerate(sp)]
    pool_s, _, n_s, m_s, k_s, v_s = [jnp.stack(a) if j != 1 else None for j, a in enumerate(ss)]
    return (xp, xs, pool_p, pool_s, c_p, c_s, n_p, n_s, m_p, m_s, k_p, k_s, v_p, v_s)
```
